```python
import math, functools
import jax, jax.numpy as jnp
from jax import lax
import numpy as np

D_MODEL = 2048
BATCH = 8
SEQ = 2048
DEPTH = 1
DEC_BATCH = 128
DEC_SEQ = 4
PAST_LEN = 8192
PAGE_SIZE = 128

N_HEADS = 16
N_KV_HEADS = 4
GROUP = N_HEADS // N_KV_HEADS
HEAD_DIM = 64
WINDOW = 128
ROT_DIM = HEAD_DIM // 4
ROPE_THETA = 500000.0
SWA_Q = N_HEADS * HEAD_DIM
SWA_KV = N_KV_HEADS * HEAD_DIM
GLA_HEADS = 4
GLA_DK_TOTAL = D_MODEL // 2
GLA_DV_TOTAL = D_MODEL
GLA_DK = GLA_DK_TOTAL // GLA_HEADS
GLA_DV = GLA_DV_TOTAL // GLA_HEADS
GLA_GATE_RANK = 16
GLA_GATE_NORM = 16.0
GLA_CHUNK = 16
D_FF = 4 * D_MODEL
EPS = 1e-6
SPLITS = (SWA_Q, SWA_KV, SWA_KV, GLA_DK_TOTAL, GLA_DK_TOTAL, GLA_DV_TOTAL, GLA_DV_TOTAL,
          GLA_GATE_RANK, D_MODEL, D_MODEL)
D_IN = sum(SPLITS)

kernel_name = "hybrid_swa_sink_gla_gated_merge_step"


def rms_norm(x, w):
    xf = x.astype(jnp.float32)
    y = xf * lax.rsqrt(jnp.mean(xf * xf, axis=-1, keepdims=True) + EPS)
    return (y * w.astype(jnp.float32)).astype(x.dtype)


def partial_rope(x, pos):
    half = ROT_DIM // 2
    inv = ROPE_THETA ** (-jnp.arange(half, dtype=jnp.float32) * 2.0 / ROT_DIM)
    ang = pos.astype(jnp.float32)[:, None] * inv[None, :]
    cos = jnp.cos(ang)[:, None, :]
    sin = jnp.sin(ang)[:, None, :]
    xf = x.astype(jnp.float32)
    x1 = xf[..., :half]
    x2 = xf[..., half:ROT_DIM]
    out = jnp.concatenate([x1 * cos - x2 * sin, x2 * cos + x1 * sin, xf[..., ROT_DIM:]], axis=-1)
    return out.astype(x.dtype)


def sink_attend(q, k, v, mask, sink):
    s = jnp.einsum('...qkgd,...skd->...kgqs', q, k, preferred_element_type=jnp.float32) * (HEAD_DIM ** -0.5)
    s = jnp.where(mask, s, -jnp.inf)
    sk = sink.astype(jnp.float32)[:, :, None, None]
    m = jnp.maximum(jnp.max(s, axis=-1, keepdims=True), sk)
    p = jnp.exp(s - m)
    denom = jnp.sum(p, axis=-1, keepdims=True) + jnp.exp(sk - m)
    return jnp.einsum('...kgqs,...skd->...qkgd', p / denom, v.astype(jnp.float32))


def swa_prompt(q, k, v, sink):
    B, T = q.shape[:2]
    nb = T // WINDOW
    qb = q.reshape(B, nb, WINDOW, N_KV_HEADS, GROUP, HEAD_DIM)
    kb = k.reshape(B, nb, WINDOW, N_KV_HEADS, HEAD_DIM)
    vb = v.reshape(B, nb, WINDOW, N_KV_HEADS, HEAD_DIM)
    prev = lambda a: jnp.concatenate([jnp.zeros_like(a[:, :1]), a[:, :-1]], axis=1)
    kc = jnp.concatenate([prev(kb), kb], axis=2)
    vc = jnp.concatenate([prev(vb), vb], axis=2)
    qpos = jnp.arange(T).reshape(nb, WINDOW)
    kpos = jnp.concatenate([qpos - WINDOW, qpos], axis=1)
    diff = qpos[:, :, None] - kpos[:, None, :]
    mask = (diff >= 0) & (diff < WINDOW) & (kpos[:, None, :] >= 0)
    o = sink_attend(qb, kc, vc, mask[None, :, None, None], sink)
    w_keep = min(WINDOW, T)
    return o.reshape(B, T, SWA_Q), (k[:, T - w_keep:], v[:, T - w_keep:])


def swa_sample(ck, cv, q, k, v, sink):
    Bd, S = q.shape[:2]
    Wc = ck.shape[1]
    kall = jnp.concatenate([ck, k.astype(ck.dtype)], axis=1)
    vall = jnp.concatenate([cv, v.astype(cv.dtype)], axis=1)
    qpos = PAST_LEN + jnp.arange(S)
    kpos = PAST_LEN - Wc + jnp.arange(Wc + S)
    diff = qpos[:, None] - kpos[None, :]
    mask = (diff >= 0) & (diff < WINDOW)
    o = sink_attend(q.reshape(Bd, S, N_KV_HEADS, GROUP, HEAD_DIM), kall, vall, mask, sink)
    return o.reshape(Bd, S, SWA_Q), (kall[:, S:], vall[:, S:])


def gla_recurrent(q, k, v, log_a, S0):
    B, T = q.shape[:2]
    L = math.gcd(T, GLA_CHUNK)
    N = T // L
    r = lambda a: a.reshape(B, N, L, *a.shape[2:])
    q, k, v, log_a = r(q), r(k), r(v), r(log_a)
    b = jnp.cumsum(log_a, axis=2)
    b_last = b[:, :, -1:]
    q_i = q * jnp.exp(b)
    k_i = k * jnp.exp(-b)
    k_d = k * jnp.exp(b_last - b)
    causal = jnp.tril(jnp.ones((L, L), dtype=bool))
    A = jnp.where(causal, jnp.einsum('bnlhk,bnmhk->bnhlm', q_i, k_i), 0.0)
    o_intra = jnp.einsum('bnhlm,bnmhv->bnlhv', A, v)
    decay = jnp.exp(b_last[:, :, 0])

    def step(S, xs):
        qc, kc, vc, dc = xs
        o = jnp.einsum('blhk,bhkv->blhv', qc, S)
        S = dc[..., None] * S + jnp.einsum('blhk,blhv->bhkv', kc, vc)
        return S, o

    xs = tuple(jnp.moveaxis(a, 1, 0) for a in (q_i, k_d, v, decay))
    S_fin, o_inter = lax.scan(step, S0.astype(jnp.float32), xs)
    o = o_intra + jnp.moveaxis(o_inter, 0, 1)
    return o.reshape(B, T, GLA_HEADS, GLA_DV), S_fin


def decoder_layer(x, pos, swa_fn, S0, norm1, w_in, w_a2, b_a, sink, gla_norm,
                  p_swa, p_gla, w_o, norm2, w_up, w_down):
    B, T, _ = x.shape
    f32 = jnp.float32
    h = rms_norm(x, norm1)
    z = h @ w_in
    qs, ks, vs, qg, kg, vg, rg, ag, gs, gg = jnp.split(z, np.cumsum(SPLITS)[:-1].tolist(), axis=-1)
    q = partial_rope(qs.reshape(B, T, N_HEADS, HEAD_DIM), pos)
    k = partial_rope(ks.reshape(B, T, N_KV_HEADS, HEAD_DIM), pos)
    v = vs.reshape(B, T, N_KV_HEADS, HEAD_DIM)
    o_swa, swa_state = swa_fn(q, k, v, sink.reshape(N_KV_HEADS, GROUP))
    log_a = jax.nn.log_sigmoid((ag @ w_a2 + b_a).astype(f32)) / GLA_GATE_NORM
    hd = lambda a, d: a.reshape(B, T, GLA_HEADS, d).astype(f32)
    o_g, S_new = gla_recurrent(hd(qg, GLA_DK) * (GLA_DK ** -0.5), hd(kg, GLA_DK), hd(vg, GLA_DV),
                               hd(log_a, GLA_DK), S0)
    o_g = o_g * lax.rsqrt(jnp.mean(o_g * o_g, axis=-1, keepdims=True) + EPS) * gla_norm.astype(f32)
    o_g = (o_g.reshape(B, T, GLA_DV_TOTAL) * jax.nn.silu(rg.astype(f32))).astype(x.dtype)
    y = jax.nn.sigmoid(gs) * (o_swa.astype(x.dtype) @ p_swa) + jax.nn.sigmoid(gg) * (o_g @ p_gla)
    x = x + y @ w_o
    h2 = rms_norm(x, norm2)
    x = x + jnp.square(jax.nn.relu(h2 @ w_up)) @ w_down
    return x, swa_state, S_new


def setup_inputs(seed: int = 0) -> dict:
    key = jax.random.key(seed)
    ks = jax.random.split(key, 20)
    nrm = lambda k, shape, scale: jax.random.normal(k, shape, jnp.float32) * scale
    w_buf = min(WINDOW, PAST_LEN)
    return {
        "x_prompt": nrm(ks[0], (BATCH, SEQ, D_MODEL), 1.0),
        "x_sample": nrm(ks[1], (DEC_BATCH, DEC_SEQ, D_MODEL), 1.0),
        "cache_swa_k": nrm(ks[2], (DEPTH, DEC_BATCH, w_buf, N_KV_HEADS, HEAD_DIM), 1.0),
        "cache_swa_v": nrm(ks[3], (DEPTH, DEC_BATCH, w_buf, N_KV_HEADS, HEAD_DIM), 1.0),
        "state_gla": nrm(ks[4], (DEPTH, DEC_BATCH, GLA_HEADS, GLA_DK, GLA_DV), 1.0),
        "norm1": 1.0 + nrm(ks[5], (DEPTH, D_MODEL), 0.02),
        "w_in": nrm(ks[6], (DEPTH, D_MODEL, D_IN), D_MODEL ** -0.5),
        "w_a2": nrm(ks[7], (DEPTH, GLA_GATE_RANK, GLA_DK_TOTAL), GLA_GATE_RANK ** -0.5),
        "b_a": nrm(ks[8], (DEPTH, GLA_DK_TOTAL), 0.1),
        "sink": nrm(ks[9], (DEPTH, N_HEADS), 0.5),
        "gla_norm": 1.0 + nrm(ks[10], (DEPTH, GLA_DV), 0.02),
        "p_swa": nrm(ks[11], (DEPTH, SWA_Q, D_MODEL), SWA_Q ** -0.5),
        "p_gla": nrm(ks[12], (DEPTH, GLA_DV_TOTAL, D_MODEL), GLA_DV_TOTAL ** -0.5),
        "w_o": nrm(ks[13], (DEPTH, D_MODEL, D_MODEL), D_MODEL ** -0.5),
        "norm2": 1.0 + nrm(ks[14], (DEPTH, D_MODEL), 0.02),
        "w_up": nrm(ks[15], (DEPTH, D_MODEL, D_FF), D_MODEL ** -0.5),
        "w_down": nrm(ks[16], (DEPTH, D_FF, D_MODEL), D_FF ** -0.5),
        "final_norm": 1.0 + nrm(ks[17], (D_MODEL,), 0.02),
    }


def reference(x_prompt, x_sample, cache_swa_k, cache_swa_v, state_gla, norm1, w_in, w_a2, b_a,
              sink, gla_norm, p_swa, p_gla, w_o, norm2, w_up, w_down, final_norm):
    xp, xs = x_prompt, x_sample
    pos_p = jnp.arange(xp.shape[1])
    pos_s = PAST_LEN + jnp.arange(xs.shape[1])
    S0_p = jnp.zeros((xp.shape[0], GLA_HEADS, GLA_DK, GLA_DV), jnp.float32)
    kp, vp, sp, ksn, vsn, ssn = [], [], [], [], [], []
    for l in range(DEPTH):
        lw = (norm1[l], w_in[l], w_a2[l], b_a[l], sink[l], gla_norm[l],
              p_swa[l], p_gla[l], w_o[l], norm2[l], w_up[l], w_down[l])
        xp, (k1, v1), S1 = decoder_layer(xp, pos_p, swa_prompt, S0_p, *lw)
        xs, (k2, v2), S2 = decoder_layer(
            xs, pos_s, functools.partial(swa_sample, cache_swa_k[l], cache_swa_v[l]), state_gla[l], *lw)
        kp.append(k1.astype(cache_swa_k.dtype)); vp.append(v1.astype(cache_swa_v.dtype))
        sp.append(S1.astype(state_gla.dtype))
        ksn.append(k2); vsn.append(v2); ssn.append(S2.astype(state_gla.dtype))
    y_prompt = rms_norm(xp, final_norm)
    y_sample = rms_norm(xs, final_norm)
    return (y_prompt, y_sample, jnp.stack(kp), jnp.stack(vp), jnp.stack(sp),
            jnp.stack(ksn), jnp.stack(vsn), jnp.stack(ssn))
```

```python
import functools

import jax
import jax.numpy as jnp
from jax import lax
from jax.experimental import pallas as pl
from jax.experimental.pallas import tpu as pltpu

F32 = jnp.float32
BF16 = jnp.bfloat16

D_MODEL = 2048
PAST_LEN = 8192
N_HEADS = 16
N_KV_HEADS = 4
GROUP = N_HEADS // N_KV_HEADS
HEAD_DIM = 64
WINDOW = 128
ROT_DIM = HEAD_DIM // 4
ROPE_THETA = 500000.0
SWA_Q = N_HEADS * HEAD_DIM
SWA_KV = N_KV_HEADS * HEAD_DIM
GLA_HEADS = 4
GLA_DK = 256
GLA_DV = 512
GLA_DK_TOTAL = GLA_HEADS * GLA_DK
GLA_DV_TOTAL = GLA_HEADS * GLA_DV
GLA_GATE_RANK = 16
GLA_GATE_NORM = 16.0
D_FF = 4 * D_MODEL
EPS = 1e-6

LANES = 128
SUBLANES = 8
VMEM_LIMIT = 56 * 1024 * 1024

OFF_GS = 0
OFF_GG = OFF_GS + D_MODEL
OFF_VG = OFF_GG + D_MODEL
OFF_RG = OFF_VG + GLA_DV_TOTAL
OFF_QG = OFF_RG + GLA_DV_TOTAL
OFF_KG = OFF_QG + GLA_DK_TOTAL
OFF_QS = OFF_KG + GLA_DK_TOTAL
OFF_KS = OFF_QS + SWA_Q
OFF_VS = OFF_KS + SWA_KV
Z_COLS = OFF_VS + SWA_KV

GLA_CHUNK = 64


def _cparams(sem):
    return pltpu.CompilerParams(dimension_semantics=sem, vmem_limit_bytes=VMEM_LIMIT)


def _rms(x, w):
    return x * lax.rsqrt(jnp.mean(x * x, axis=-1, keepdims=True) + EPS) * w


def _sigmoid(x):
    return 1.0 / (1.0 + jnp.exp(-x))


def _dot(a, b):
    return jnp.dot(a, b, preferred_element_type=F32)


def _dot_nt(a, b):
    return lax.dot_general(a, b, (((1,), (1,)), ((), ())), preferred_element_type=F32)


def _dot_tn(a, b):
    return lax.dot_general(a, b, (((0,), (0,)), ((), ())), preferred_element_type=F32)


def _split_bf16(x):
    hi = x.astype(BF16)
    lo = (x - hi.astype(F32)).astype(BF16)
    return hi, lo


def _in_proj_kernel(x_ref, nw_ref, w_ref, wag_ref, z_ref, zag_ref, h_ref):
    @pl.when(pl.program_id(1) == 0)
    def _():
        h = _rms(x_ref[...], nw_ref[...]).astype(BF16)
        h_ref[...] = h
        zag_ref[...] = _dot(h, wag_ref[...])

    z_ref[...] = _dot(h_ref[...], w_ref[...]).astype(z_ref.dtype)


def _in_proj(x2d, norm1, w_main, w_ag, tm, tn):
    n = x2d.shape[0]
    return pl.pallas_call(
        _in_proj_kernel,
        grid=(n // tm, Z_COLS // tn),
        in_specs=[
            pl.BlockSpec((tm, D_MODEL), lambda i, j: (i, 0)),
            pl.BlockSpec((1, D_MODEL), lambda i, j: (0, 0)),
            pl.BlockSpec((D_MODEL, tn), lambda i, j: (0, j)),
            pl.BlockSpec((D_MODEL, LANES), lambda i, j: (0, 0)),
        ],
        out_specs=[
            pl.BlockSpec((tm, tn), lambda i, j: (i, j)),
            pl.BlockSpec((tm, LANES), lambda i, j: (i, 0)),
        ],
        out_shape=[
            jax.ShapeDtypeStruct((n, Z_COLS), F32),
            jax.ShapeDtypeStruct((n, LANES), F32),
        ],
        scratch_shapes=[pltpu.VMEM((tm, D_MODEL), BF16)],
        compiler_params=_cparams(("arbitrary", "arbitrary")),
        name="in_proj",
    )(x2d, norm1, w_main, w_ag)


def _rope_tables(pos):
    half = ROT_DIM // 2
    inv = ROPE_THETA ** (-jnp.arange(half, dtype=F32) * 2.0 / ROT_DIM)
    ang = pos.astype(F32)[:, None] * inv[None, :]
    cos, sin = jnp.cos(ang), jnp.sin(ang)
    t = pos.shape[0]
    ones = jnp.ones((t, HEAD_DIM - ROT_DIM), F32)
    zeros = jnp.zeros((t, HEAD_DIM - ROT_DIM), F32)
    zh = jnp.zeros((t, half), F32)
    c = jnp.concatenate([cos, cos, ones], axis=1)
    s_lo = jnp.concatenate([zh, sin, zeros], axis=1)
    s_hi = jnp.concatenate([-sin, zh, zeros], axis=1)
    rep = LANES // HEAD_DIM
    return tuple(jnp.tile(a, (1, rep)) for a in (c, s_lo, s_hi))


def _rope(x, c, s_lo, s_hi):
    half = ROT_DIM // 2
    outs = []
    for j in range(x.shape[1] // LANES):
        xc = x[:, j * LANES:(j + 1) * LANES]
        outs.append(xc * c + pltpu.roll(xc, half, 1) * s_lo + pltpu.roll(xc, LANES - half, 1) * s_hi)
    return outs[0] if len(outs) == 1 else jnp.concatenate(outs, axis=1)


def _swa_prompt_kernel(sink_ref, q_ref, k_ref, v_ref, c_ref, slo_ref, shi_ref,
                       o_ref, klast_ref, vlast_ref, kprev_ref, vprev_ref):
    n = pl.program_id(1)
    nb = pl.num_programs(1)
    w = WINDOW

    @pl.when(n == 0)
    def _():
        kprev_ref[...] = jnp.zeros_like(kprev_ref)
        vprev_ref[...] = jnp.zeros_like(vprev_ref)

    c, s_lo, s_hi = c_ref[...], slo_ref[...], shi_ref[...]
    q = _rope(q_ref[...], c, s_lo, s_hi) * (HEAD_DIM ** -0.5)
    k = _rope(k_ref[...], c, s_lo, s_hi)
    v = v_ref[...]
    kc = jnp.concatenate([kprev_ref[...], k], axis=0).astype(BF16)
    vc = jnp.concatenate([vprev_ref[...], v], axis=0).astype(BF16)
    qb = q.astype(BF16)

    rows = lax.broadcasted_iota(jnp.int32, (GROUP * w, 2 * w), 0) & (w - 1)
    cols = lax.broadcasted_iota(jnp.int32, (GROUP * w, 2 * w), 1)
    diff = rows + w - cols
    mask = (diff >= 0) & (diff < w) & ((cols >= w) | (n > 0))

    for g in range(N_KV_HEADS):
        kg = kc[:, g * HEAD_DIM:(g + 1) * HEAD_DIM]
        vg = vc[:, g * HEAD_DIM:(g + 1) * HEAD_DIM]
        heads = [g * GROUP + h for h in range(GROUP)]
        qg = jnp.concatenate([qb[:, hh * HEAD_DIM:(hh + 1) * HEAD_DIM] for hh in heads], axis=0)
        sk = jnp.concatenate([jnp.full((w, 1), sink_ref[0, hh], F32) for hh in heads], axis=0)
        s = jnp.where(mask, _dot_nt(qg, kg), -jnp.inf)
        m = jnp.maximum(jnp.max(s, axis=-1, keepdims=True), sk)
        p = jnp.exp(s - m)
        denom = jnp.sum(p, axis=-1, keepdims=True) + jnp.exp(sk - m)
        o = _dot(p.astype(BF16), vg) / denom
        for i, hh in enumerate(heads):
            o_ref[:, hh * HEAD_DIM:(hh + 1) * HEAD_DIM] = o[i * w:(i + 1) * w].astype(o_ref.dtype)

    kprev_ref[...] = k
    vprev_ref[...] = v

    @pl.when(n == nb - 1)
    def _():
        klast_ref[...] = k
        vlast_ref[...] = v


def _swa_prompt(z3, sink, tables):
    b, t, _ = z3.shape
    w = WINDOW
    tab_spec = pl.BlockSpec((w, LANES), lambda i, n: (n, 0))
    return pl.pallas_call(
        _swa_prompt_kernel,
        grid=(b, t // w),
        in_specs=[
            pl.BlockSpec(memory_space=pltpu.SMEM),
            pl.BlockSpec((None, w, SWA_Q), lambda i, n: (i, n, OFF_QS // SWA_Q)),
            pl.BlockSpec((None, w, SWA_KV), lambda i, n: (i, n, OFF_KS // SWA_KV)),
            pl.BlockSpec((None, w, SWA_KV), lambda i, n: (i, n, OFF_VS // SWA_KV)),
            tab_spec, tab_spec, tab_spec,
        ],
        out_specs=[
            pl.BlockSpec((None, w, SWA_Q), lambda i, n: (i, n, 0)),
            pl.BlockSpec((None, w, SWA_KV), lambda i, n: (i, 0, 0)),
            pl.BlockSpec((None, w, SWA_KV), lambda i, n: (i, 0, 0)),
        ],
        out_shape=[
            jax.ShapeDtypeStruct((b, t, SWA_Q), BF16),
            jax.ShapeDtypeStruct((b, w, SWA_KV), F32),
            jax.ShapeDtypeStruct((b, w, SWA_KV), F32),
        ],
        scratch_shapes=[pltpu.VMEM((w, SWA_KV), F32), pltpu.VMEM((w, SWA_KV), F32)],
        compiler_params=_cparams(("arbitrary", "arbitrary")),
        name="swa_prompt",
    )(sink, z3, z3, z3, *tables)


SAMPLE_T = 4
PAIR_ROWS = 2 * SAMPLE_T
KPAD = 2 * WINDOW


def _swa_sample_kernel(q_ref, k_ref, v_ref, ck_ref, cv_ref, c_ref, slo_ref, shi_ref, sink_ref,
                       o_ref, nk_ref, nv_ref, *, pairs):
    w = WINDOW
    kvw = SWA_KV
    c, s_lo, s_hi = c_ref[...], slo_ref[...], shi_ref[...]
    nrow = N_HEADS * PAIR_ROWS
    row = lax.broadcasted_iota(jnp.int32, (nrow, KPAD), 0)
    col = lax.broadcasted_iota(jnp.int32, (nrow, KPAD), 1)
    t_shift = SAMPLE_T.bit_length() - 1
    sq = row & (SAMPLE_T - 1)
    par = (row >> t_shift) & 1
    jj = col - w
    mask = ((col < w) & (col > sq)) | (
        (jj >= 0) & (jj < PAIR_ROWS) & ((jj >> t_shift) == par) & ((jj & (SAMPLE_T - 1)) <= sq))
    lane = lax.broadcasted_iota(jnp.int32, (PAIR_ROWS, kvw), 1) >> (HEAD_DIM.bit_length() - 1)
    row8 = lax.broadcasted_iota(jnp.int32, (PAIR_ROWS, kvw), 0)
    sk = sink_ref[...][:, :1]
    zpad = jnp.zeros((KPAD - w - PAIR_ROWS, kvw), F32)

    for p in range(pairs):
        rs = slice(p * PAIR_ROWS, (p + 1) * PAIR_ROWS)
        q8 = _rope(q_ref[rs, :], c, s_lo, s_hi) * (HEAD_DIM ** -0.5)
        k8 = _rope(k_ref[rs, :], c, s_lo, s_hi)
        v8 = v_ref[rs, :]
        blocks = []
        for g in range(N_KV_HEADS):
            qg = q8[:, g * kvw:(g + 1) * kvw]
            for h in range(GROUP):
                sh = ((g - h) % GROUP) * HEAD_DIM
                rolled = qg if sh == 0 else pltpu.roll(qg, sh, 1)
                blocks.append(jnp.where(lane == g, rolled, 0.0))
        qall = jnp.concatenate(blocks, axis=0).astype(BF16)

        s_par, v_par = [], []
        for e in range(2):
            bd = 2 * p + e
            kall = jnp.concatenate([ck_ref[bd], k8, zpad], axis=0).astype(BF16)
            v_par.append(jnp.concatenate([cv_ref[bd], v8, zpad], axis=0).astype(BF16))
            s_par.append(_dot_nt(qall, kall))
        s = jnp.where(par == 1, s_par[1], s_par[0])
        s = jnp.where(mask, s, -jnp.inf)
        m = jnp.maximum(jnp.max(s, axis=-1, keepdims=True), sk)
        pr = jnp.exp(s - m)
        denom = jnp.sum(pr, axis=-1, keepdims=True) + jnp.exp(sk - m)
        p0 = jnp.where(par == 0, pr, 0.0).astype(BF16)
        p1 = jnp.where(par == 1, pr, 0.0).astype(BF16)
        oall = (_dot(p0, v_par[0]) + _dot(p1, v_par[1])) / denom

        outs = []
        for g in range(N_KV_HEADS):
            acc = None
            for h in range(GROUP):
                hh = g * GROUP + h
                blk = jnp.where(lane == g, oall[hh * PAIR_ROWS:(hh + 1) * PAIR_ROWS], 0.0)
                sh = ((h - g) % GROUP) * HEAD_DIM
                blk = blk if sh == 0 else pltpu.roll(blk, sh, 1)
                acc = blk if acc is None else acc + blk
            outs.append(acc)
        o_ref[rs, :] = jnp.concatenate(outs, axis=1).astype(o_ref.dtype)

        for e in range(2):
            bd = 2 * p + e
            for new8, cref, nref in ((k8, ck_ref, nk_ref), (v8, cv_ref, nv_ref)):
                shifted = pltpu.roll(cref[bd], w - SAMPLE_T, 0)
                tail8 = new8 if e == 1 else pltpu.roll(new8, SAMPLE_T, 0)
                tail = jnp.where(row8 >= SAMPLE_T, tail8, shifted[w - PAIR_ROWS:])
                nref[bd] = jnp.concatenate([shifted[:w - PAIR_ROWS], tail], axis=0)


def _swa_sample(zs, cache_k, cache_v, sink_rows, tables, pairs):
    n = zs.shape[0]
    bd, w, kvw = cache_k.shape
    rows = pairs * PAIR_ROWS
    tab_spec = pl.BlockSpec((PAIR_ROWS, LANES), lambda i: (0, 0))
    cache_spec = pl.BlockSpec((2 * pairs, w, kvw), lambda i: (i, 0, 0))
    return pl.pallas_call(
        functools.partial(_swa_sample_kernel, pairs=pairs),
        grid=(n // rows,),
        in_specs=[
            pl.BlockSpec((rows, SWA_Q), lambda i: (i, OFF_QS // SWA_Q)),
            pl.BlockSpec((rows, SWA_KV), lambda i: (i, OFF_KS // SWA_KV)),
            pl.BlockSpec((rows, SWA_KV), lambda i: (i, OFF_VS // SWA_KV)),
            cache_spec, cache_spec,
            tab_spec, tab_spec, tab_spec,
            pl.BlockSpec((N_HEADS * PAIR_ROWS, LANES), lambda i: (0, 0)),
        ],
        out_specs=[
            pl.BlockSpec((rows, SWA_Q), lambda i: (i, 0)),
            cache_spec, cache_spec,
        ],
        out_shape=[
            jax.ShapeDtypeStruct((n, SWA_Q), BF16),
            jax.ShapeDtypeStruct((bd, w, kvw), F32),
            jax.ShapeDtypeStruct((bd, w, kvw), F32),
        ],
        compiler_params=_cparams(("arbitrary",)),
        name="swa_sample",
    )(zs, zs, zs, cache_k, cache_v, *tables, sink_rows)


def _log_decay(ag, wa2, ba):
    x = _dot(ag.astype(BF16), wa2) + ba
    log_sig = jnp.minimum(x, 0.0) - jnp.log1p(jnp.exp(-jnp.abs(x)))
    return log_sig / GLA_GATE_NORM


def _gla_out(o, gnorm, rg):
    o = o * lax.rsqrt(jnp.mean(o * o, axis=-1, keepdims=True) + EPS) * gnorm
    return o * (rg * _sigmoid(rg))


def _gla_prompt_kernel(q_ref, k_ref, v_ref, rg_ref, ag_ref, wa2_ref, ba_ref, gn_ref,
                       o_ref, sout_ref, s_ref, *, chunk):
    t = pl.program_id(2)
    nt = pl.num_programs(2)
    tt = q_ref.shape[0]

    @pl.when(t == 0)
    def _():
        s_ref[...] = jnp.zeros_like(s_ref)

    la = _log_decay(ag_ref[...], wa2_ref[...], ba_ref[...])
    ri = lax.broadcasted_iota(jnp.int32, (chunk, chunk), 0)
    ci = lax.broadcasted_iota(jnp.int32, (chunk, chunk), 1)
    causal = ri >= ci
    tril = jnp.where(causal, 1.0, 0.0).astype(BF16)
    ones = jnp.ones((chunk, LANES), BF16)
    gn = gn_ref[...]

    for c in range(tt // chunk):
        sl = slice(c * chunk, (c + 1) * chunk)
        hi, lo = _split_bf16(la[sl])
        b = _dot(tril, hi) + _dot(tril, lo)
        b_last = b[chunk - 1:chunk, :]
        b_last_col = _dot_tn(hi, ones) + _dot_tn(lo, ones)
        decay = jnp.exp(b_last_col)
        q = q_ref[sl, :] * (GLA_DK ** -0.5)
        k = k_ref[sl, :]
        v = v_ref[sl, :].astype(BF16)
        q_i = (q * jnp.exp(b)).astype(BF16)
        k_i = (k * jnp.exp(-b)).astype(BF16)
        k_d = (k * jnp.exp(b_last - b)).astype(BF16)
        a = jnp.where(causal, _dot_nt(q_i, k_i), 0.0).astype(BF16)
        s_old = s_ref[...]
        o = _dot(a, v) + _dot(q_i, s_old.astype(BF16))
        decay_full = jnp.concatenate([decay] * (GLA_DV // LANES), axis=1)
        s_ref[...] = decay_full * s_old + _dot_tn(k_d, v)
        o_ref[sl, :] = _gla_out(o, gn, rg_ref[sl, :]).astype(o_ref.dtype)

    @pl.when(t == nt - 1)
    def _():
        sout_ref[...] = s_ref[...]


def _gla_prompt(z3, zag3, wa2, ba, gnorm, tt, chunk):
    b, t, _ = z3.shape
    h = GLA_HEADS
    return pl.pallas_call(
        functools.partial(_gla_prompt_kernel, chunk=chunk),
        grid=(b, h, t // tt),
        in_specs=[
            pl.BlockSpec((None, tt, GLA_DK), lambda i, j, s: (i, s, OFF_QG // GLA_DK + j)),
            pl.BlockSpec((None, tt, GLA_DK), lambda i, j, s: (i, s, OFF_KG // GLA_DK + j)),
            pl.BlockSpec((None, tt, GLA_DV), lambda i, j, s: (i, s, OFF_VG // GLA_DV + j)),
            pl.BlockSpec((None, tt, GLA_DV), lambda i, j, s: (i, s, OFF_RG // GLA_DV + j)),
            pl.BlockSpec((None, tt, LANES), lambda i, j, s: (i, s, 0)),
            pl.BlockSpec((LANES, GLA_DK), lambda i, j, s: (0, j)),
            pl.BlockSpec((1, GLA_DK), lambda i, j, s: (0, j)),
            pl.BlockSpec((1, GLA_DV), lambda i, j, s: (0, 0)),
        ],
        out_specs=[
            pl.BlockSpec((None, tt, GLA_DV), lambda i, j, s: (i, s, j)),
            pl.BlockSpec((None, None, GLA_DK, GLA_DV), lambda i, j, s: (i, j, 0, 0)),
        ],
        out_shape=[
            jax.ShapeDtypeStruct((b, t, GLA_DV_TOTAL), BF16),
            jax.ShapeDtypeStruct((b, h, GLA_DK, GLA_DV), F32),
        ],
        scratch_shapes=[pltpu.VMEM((GLA_DK, GLA_DV), F32)],
        compiler_params=_cparams(("arbitrary", "arbitrary", "arbitrary")),
        name="gla_prompt",
    )(z3, z3, z3, z3, zag3, wa2, ba, gnorm)


def _gla_sample_kernel(q_ref, k_ref, v_ref, rg_ref, ag_ref, wa2_ref, ba_ref, gn_ref, s0_ref,
                       o_ref, s1_ref):
    dk, dv = GLA_DK, GLA_DV
    r_k = lax.broadcasted_iota(jnp.int32, (PAIR_ROWS, dk), 0)
    r_v = lax.broadcasted_iota(jnp.int32, (PAIR_ROWS, dv), 0)
    step_k = r_k & (SAMPLE_T - 1)
    step_v = r_v & (SAMPLE_T - 1)
    odd_k = r_k >= SAMPLE_T
    odd_v = r_v >= SAMPLE_T
    la_all = _log_decay(ag_ref[...], wa2_ref[...], ba_ref[...])
    gn = gn_ref[...]
    kpad = jnp.zeros((LANES - PAIR_ROWS, dk), F32)
    vpad = jnp.zeros((LANES - PAIR_ROWS, dv), BF16)
    ones = jnp.ones((LANES, LANES), BF16)

    for h in range(GLA_HEADS):
        la = la_all[:, h * dk:(h + 1) * dk]
        b = la + jnp.where(step_k >= 1, pltpu.roll(la, 1, 0), 0.0)
        b = b + jnp.where(step_k >= 2, pltpu.roll(b, 2, 0), 0.0)
        b_last = jnp.where(odd_k, b[PAIR_ROWS - 1:PAIR_ROWS, :], b[SAMPLE_T - 1:SAMPLE_T, :])
        q = q_ref[:, h * dk:(h + 1) * dk] * (dk ** -0.5)
        k = k_ref[:, h * dk:(h + 1) * dk]
        v = v_ref[:, h * dv:(h + 1) * dv]
        q_i = q * jnp.exp(b)
        k_i = k * jnp.exp(-b)
        k_d = k * jnp.exp(b_last - b)
        o = jnp.sum(q_i * k_i, axis=-1, keepdims=True) * v
        for d in range(1, SAMPLE_T):
            a_d = jnp.sum(q_i * pltpu.roll(k_i, d, 0), axis=-1, keepdims=True)
            o = o + jnp.where(step_v >= d, a_d * pltpu.roll(v, d, 0), 0.0)
        q_b = q_i.astype(BF16)
        vb = jnp.concatenate([v.astype(BF16), vpad], axis=0)
        o_par = []
        for e in range(2):
            s_old = s0_ref[e, h]
            o_par.append(_dot(q_b, s_old.astype(BF16)))
            sel = (r_k >= SAMPLE_T) if e == 1 else (r_k < SAMPLE_T)
            kd_e = jnp.concatenate([jnp.where(sel, k_d, 0.0), kpad], axis=0).astype(BF16)
            la_e = jnp.concatenate([jnp.where(sel, la, 0.0), kpad], axis=0)
            hi, lo = _split_bf16(la_e)
            decay = jnp.exp(_dot_tn(hi, ones) + _dot_tn(lo, ones))
            decay_full = jnp.concatenate([decay] * (dv // LANES), axis=1)
            s1_ref[e, h] = decay_full * s_old + _dot_tn(kd_e, vb)
        o = o + jnp.where(odd_v, o_par[1], o_par[0])
        o_ref[:, h * dv:(h + 1) * dv] = _gla_out(o, gn, rg_ref[:, h * dv:(h + 1) * dv]).astype(o_ref.dtype)


def _gla_sample(zs, zag, wa2, ba, gnorm, state):
    n = zs.shape[0]
    bd, h, dk, dv = state.shape
    state_spec = pl.BlockSpec((2, h, dk, dv), lambda i: (i, 0, 0, 0))
    return pl.pallas_call(
        _gla_sample_kernel,
        grid=(n // PAIR_ROWS,),
        in_specs=[
            pl.BlockSpec((PAIR_ROWS, GLA_DK_TOTAL), lambda i: (i, OFF_QG // GLA_DK_TOTAL)),
            pl.BlockSpec((PAIR_ROWS, GLA_DK_TOTAL), lambda i: (i, OFF_KG // GLA_DK_TOTAL)),
            pl.BlockSpec((PAIR_ROWS, GLA_DV_TOTAL), lambda i: (i, OFF_VG // GLA_DV_TOTAL)),
            pl.BlockSpec((PAIR_ROWS, GLA_DV_TOTAL), lambda i: (i, OFF_RG // GLA_DV_TOTAL)),
            pl.BlockSpec((PAIR_ROWS, LANES), lambda i: (i, 0)),
            pl.BlockSpec((LANES, GLA_DK_TOTAL), lambda i: (0, 0)),
            pl.BlockSpec((1, GLA_DK_TOTAL), lambda i: (0, 0)),
            pl.BlockSpec((1, GLA_DV), lambda i: (0, 0)),
            state_spec,
        ],
        out_specs=[
            pl.BlockSpec((PAIR_ROWS, GLA_DV_TOTAL), lambda i: (i, 0)),
            state_spec,
        ],
        out_shape=[
            jax.ShapeDtypeStruct((n, GLA_DV_TOTAL), BF16),
            jax.ShapeDtypeStruct(state.shape, F32),
        ],
        compiler_params=_cparams(("arbitrary",)),
        name="gla_sample",
    )(zs, zs, zs, zs, zag, wa2, ba, gnorm, state)


def _merge_kernel(osw_ref, og_ref, gs_ref, gg_ref, x_ref, pswa_ref, pgla_ref, wo_ref, n2_ref,
                  x1_ref, h2_ref):
    a = _dot(osw_ref[...], pswa_ref[...])
    b = _dot(og_ref[...], pgla_ref[...])
    y = _sigmoid(gs_ref[...].astype(F32)) * a + _sigmoid(gg_ref[...].astype(F32)) * b
    x1 = x_ref[...] + _dot(y.astype(BF16), wo_ref[...])
    x1_ref[...] = x1
    h2_ref[...] = _rms(x1, n2_ref[...]).astype(BF16)


def _merge(o_swa, o_gla, z, x2d, p_swa, p_gla, w_o, norm2, tm):
    n = x2d.shape[0]
    resident = functools.partial(pl.BlockSpec, pipeline_mode=pl.Buffered(1))
    return pl.pallas_call(
        _merge_kernel,
        grid=(n // tm,),
        in_specs=[
            pl.BlockSpec((tm, SWA_Q), lambda i: (i, 0)),
            pl.BlockSpec((tm, GLA_DV_TOTAL), lambda i: (i, 0)),
            pl.BlockSpec((tm, D_MODEL), lambda i: (i, OFF_GS // D_MODEL)),
            pl.BlockSpec((tm, D_MODEL), lambda i: (i, OFF_GG // D_MODEL)),
            pl.BlockSpec((tm, D_MODEL), lambda i: (i, 0)),
            resident((SWA_Q, D_MODEL), lambda i: (0, 0)),
            resident((GLA_DV_TOTAL, D_MODEL), lambda i: (0, 0)),
            resident((D_MODEL, D_MODEL), lambda i: (0, 0)),
            pl.BlockSpec((1, D_MODEL), lambda i: (0, 0)),
        ],
        out_specs=[
            pl.BlockSpec((tm, D_MODEL), lambda i: (i, 0)),
            pl.BlockSpec((tm, D_MODEL), lambda i: (i, 0)),
        ],
        out_shape=[
            jax.ShapeDtypeStruct((n, D_MODEL), F32),
            jax.ShapeDtypeStruct((n, D_MODEL), BF16),
        ],
        compiler_params=_cparams(("arbitrary",)),
        name="merge",
    )(o_swa, o_gla, z, z, x2d, p_swa, p_gla, w_o, norm2)


def _mlp_kernel(h2_ref, x1_ref, wup_ref, wdn_ref, fn_ref, out_ref, acc_ref):
    f = pl.program_id(1)
    nf = pl.num_programs(1)
    u = _dot(h2_ref[...], wup_ref[...])
    u = jnp.square(jnp.maximum(u, 0.0)).astype(BF16)
    d = _dot(u, wdn_ref[...])

    @pl.when(f == 0)
    def _():
        acc_ref[...] = d

    @pl.when(f > 0)
    def _():
        acc_ref[...] += d

    @pl.when(f == nf - 1)
    def _():
        out_ref[...] = _rms(x1_ref[...] + acc_ref[...], fn_ref[...])


def _mlp(h2, x1, w_up, w_down, final_norm, tm, tf):
    n = h2.shape[0]
    return pl.pallas_call(
        _mlp_kernel,
        grid=(n // tm, D_FF // tf),
        in_specs=[
            pl.BlockSpec((tm, D_MODEL), lambda i, f: (i, 0)),
            pl.BlockSpec((tm, D_MODEL), lambda i, f: (i, 0)),
            pl.BlockSpec((D_MODEL, tf), lambda i, f: (0, f)),
            pl.BlockSpec((tf, D_MODEL), lambda i, f: (f, 0)),
            pl.BlockSpec((1, D_MODEL), lambda i, f: (0, 0)),
        ],
        out_specs=pl.BlockSpec((tm, D_MODEL), lambda i, f: (i, 0)),
        out_shape=jax.ShapeDtypeStruct((n, D_MODEL), F32),
        scratch_shapes=[pltpu.VMEM((tm, D_MODEL), F32)],
        compiler_params=_cparams(("arbitrary", "arbitrary")),
        name="mlp",
    )(h2, x1, w_up, w_down, final_norm)


def _pick_tile(n, pref):
    t = min(n, pref)
    while n % t:
        t //= 2
    return t


def _reorder_w_in(w_in):
    o = 0
    parts = {}
    for name, width in (("qs", SWA_Q), ("ks", SWA_KV), ("vs", SWA_KV), ("qg", GLA_DK_TOTAL),
                        ("kg", GLA_DK_TOTAL), ("vg", GLA_DV_TOTAL), ("rg", GLA_DV_TOTAL),
                        ("ag", GLA_GATE_RANK), ("gs", D_MODEL), ("gg", D_MODEL)):
        parts[name] = w_in[:, o:o + width]
        o += width
    w_main = jnp.concatenate([parts[nm] for nm in ("gs", "gg", "vg", "rg", "qg", "kg", "qs", "ks", "vs")],
                             axis=1).astype(BF16)
    w_ag = jnp.pad(parts["ag"], ((0, 0), (0, LANES - GLA_GATE_RANK))).astype(BF16)
    return w_main, w_ag


def kernel(x_prompt, x_sample, cache_swa_k, cache_swa_v, state_gla, norm1, w_in, w_a2, b_a, sink,
           gla_norm, p_swa, p_gla, w_o, norm2, w_up, w_down, final_norm):
    assert norm1.shape[0] == 1, "single-layer stack"
    bp, tp, d = x_prompt.shape
    bs, ts, _ = x_sample.shape
    assert ts == SAMPLE_T and bs % 2 == 0 and tp % WINDOW == 0
    w_buf = cache_swa_k.shape[2]
    assert w_buf == WINDOW

    w_main, w_ag = _reorder_w_in(w_in[0])
    wa2 = jnp.pad(w_a2[0], ((0, LANES - GLA_GATE_RANK), (0, 0))).astype(BF16)
    ba = b_a[0][None, :]
    n1, n2, fn = norm1[0][None, :], norm2[0][None, :], final_norm[None, :]
    gn = gla_norm[0][None, :]
    pswa, pgla, wo = p_swa[0].astype(BF16), p_gla[0].astype(BF16), w_o[0].astype(BF16)
    wup, wdn = w_up[0].astype(BF16), w_down[0].astype(BF16)
    sink_smem = sink[0][None, :]
    sink_rows = jnp.broadcast_to(jnp.repeat(sink[0], PAIR_ROWS)[:, None], (N_HEADS * PAIR_ROWS, LANES))

    xp = x_prompt.reshape(bp * tp, d)
    xs = x_sample.reshape(bs * ts, d)
    np_, ns = xp.shape[0], xs.shape[0]

    zp, zagp = _in_proj(xp, n1, w_main, w_ag, _pick_tile(np_, 1024), 512)
    zs, zags = _in_proj(xs, n1, w_main, w_ag, _pick_tile(ns, 512), 512)

    zp3 = zp.reshape(bp, tp, Z_COLS)
    o_swa_p, k_last, v_last = _swa_prompt(zp3, sink_smem, _rope_tables(jnp.arange(tp)))
    pos_s = PAST_LEN + jnp.arange(ts)
    tabs_s = _rope_tables(jnp.concatenate([pos_s, pos_s]))
    pairs = _pick_tile(bs // 2, 4)
    o_swa_s, nk_s, nv_s = _swa_sample(zs, cache_swa_k[0].reshape(bs, w_buf, SWA_KV),
                                      cache_swa_v[0].reshape(bs, w_buf, SWA_KV), sink_rows, tabs_s, pairs)

    tt = _pick_tile(tp, 256)
    o_gla_p, s_p = _gla_prompt(zp3, zagp.reshape(bp, tp, LANES), wa2, ba, gn, tt, min(GLA_CHUNK, tt))
    o_gla_s, s_s = _gla_sample(zs, zags, wa2, ba, gn, state_gla[0])

    x1p, h2p = _merge(o_swa_p.reshape(np_, SWA_Q), o_gla_p.reshape(np_, GLA_DV_TOTAL), zp, xp,
                      pswa, pgla, wo, n2, _pick_tile(np_, 256))
    x1s, h2s = _merge(o_swa_s, o_gla_s, zs, xs, pswa, pgla, wo, n2, _pick_tile(ns, 256))
    yp = _mlp(h2p, x1p, wup, wdn, fn, _pick_tile(np_, 512), 512)
    ys = _mlp(h2s, x1s, wup, wdn, fn, _pick_tile(ns, 512), 512)

    kv5 = lambda a, nb: a.reshape(1, nb, w_buf, N_KV_HEADS, HEAD_DIM)
    return (yp.reshape(bp, tp, d), ys.reshape(bs, ts, d),
            kv5(k_last, bp), kv5(v_last, bp), s_p[None],
            kv5(nk_s, bs), kv5(nv_s, bs), s_s[None])
```

```python
import functools

import jax
import jax.numpy as jnp
from jax import lax
from jax.experimental import pallas as pl
from jax.experimental.pallas import tpu as pltpu

F32 = jnp.float32
BF16 = jnp.bfloat16

D_MODEL = 2048
PAST_LEN = 8192
N_HEADS = 16
N_KV_HEADS = 4
GROUP = N_HEADS // N_KV_HEADS
HEAD_DIM = 64
WINDOW = 128
ROT_DIM = HEAD_DIM // 4
ROPE_THETA = 500000.0
SWA_Q = N_HEADS * HEAD_DIM
SWA_KV = N_KV_HEADS * HEAD_DIM
GLA_HEADS = 4
GLA_DK = 256
GLA_DV = 512
GLA_DK_TOTAL = GLA_HEADS * GLA_DK
GLA_DV_TOTAL = GLA_HEADS * GLA_DV
GLA_GATE_RANK = 16
GLA_GATE_NORM = 16.0
D_FF = 4 * D_MODEL
EPS = 1e-6

LANES = 128
SUBLANES = 8
VMEM_LIMIT = 56 * 1024 * 1024

OFF_GS = 0
OFF_GG = OFF_GS + D_MODEL
OFF_VG = OFF_GG + D_MODEL
OFF_RG = OFF_VG + GLA_DV_TOTAL
OFF_QG = OFF_RG + GLA_DV_TOTAL
OFF_KG = OFF_QG + GLA_DK_TOTAL
OFF_QS = OFF_KG + GLA_DK_TOTAL
OFF_KS = OFF_QS + SWA_Q
OFF_VS = OFF_KS + SWA_KV
Z_COLS = OFF_VS + SWA_KV

GLA_CHUNK = 64


def _cparams(sem):
    return pltpu.CompilerParams(dimension_semantics=sem, vmem_limit_bytes=VMEM_LIMIT)


def _rms(x, w):
    return x * lax.rsqrt(jnp.mean(x * x, axis=-1, keepdims=True) + EPS) * w


def _sigmoid(x):
    return 1.0 / (1.0 + jnp.exp(-x))


def _dot(a, b):
    return jnp.dot(a, b, preferred_element_type=F32)


def _dot_nt(a, b):
    return lax.dot_general(a, b, (((1,), (1,)), ((), ())), preferred_element_type=F32)


def _dot_tn(a, b):
    return lax.dot_general(a, b, (((0,), (0,)), ((), ())), preferred_element_type=F32)


def _split_bf16(x):
    hi = x.astype(BF16)
    lo = (x - hi.astype(F32)).astype(BF16)
    return hi, lo


def _in_proj_kernel(x_ref, nw_ref, w_ref, wag_ref, z_ref, zag_ref, h_ref):
    @pl.when(pl.program_id(1) == 0)
    def _():
        h = _rms(x_ref[...], nw_ref[...]).astype(BF16)
        h_ref[...] = h
        zag_ref[...] = _dot(h, wag_ref[...])

    z_ref[...] = _dot(h_ref[...], w_ref[...]).astype(z_ref.dtype)


def _in_proj(x2d, norm1, w_main, w_ag, tm, tn, z_dtype):
    n = x2d.shape[0]
    return pl.pallas_call(
        _in_proj_kernel,
        grid=(n // tm, Z_COLS // tn),
        in_specs=[
            pl.BlockSpec((tm, D_MODEL), lambda i, j: (i, 0)),
            pl.BlockSpec((1, D_MODEL), lambda i, j: (0, 0)),
            pl.BlockSpec((D_MODEL, tn), lambda i, j: (0, j)),
            pl.BlockSpec((D_MODEL, LANES), lambda i, j: (0, 0)),
        ],
        out_specs=[
            pl.BlockSpec((tm, tn), lambda i, j: (i, j)),
            pl.BlockSpec((tm, LANES), lambda i, j: (i, 0)),
        ],
        out_shape=[
            jax.ShapeDtypeStruct((n, Z_COLS), z_dtype),
            jax.ShapeDtypeStruct((n, LANES), F32),
        ],
        scratch_shapes=[pltpu.VMEM((tm, D_MODEL), BF16)],
        compiler_params=_cparams(("arbitrary", "arbitrary")),
        name="in_proj",
    )(x2d, norm1, w_main, w_ag)


def _rope_tables(pos):
    half = ROT_DIM // 2
    inv = ROPE_THETA ** (-jnp.arange(half, dtype=F32) * 2.0 / ROT_DIM)
    ang = pos.astype(F32)[:, None] * inv[None, :]
    cos, sin = jnp.cos(ang), jnp.sin(ang)
    t = pos.shape[0]
    ones = jnp.ones((t, HEAD_DIM - ROT_DIM), F32)
    zeros = jnp.zeros((t, HEAD_DIM - ROT_DIM), F32)
    zh = jnp.zeros((t, half), F32)
    c = jnp.concatenate([cos, cos, ones], axis=1)
    s_lo = jnp.concatenate([zh, sin, zeros], axis=1)
    s_hi = jnp.concatenate([-sin, zh, zeros], axis=1)
    rep = LANES // HEAD_DIM
    return tuple(jnp.tile(a, (1, rep)) for a in (c, s_lo, s_hi))


def _rope(x, c, s_lo, s_hi):
    half = ROT_DIM // 2
    outs = []
    for j in range(x.shape[1] // LANES):
        xc = x[:, j * LANES:(j + 1) * LANES]
        outs.append(xc * c + pltpu.roll(xc, half, 1) * s_lo + pltpu.roll(xc, LANES - half, 1) * s_hi)
    return outs[0] if len(outs) == 1 else jnp.concatenate(outs, axis=1)


def _swa_prompt_kernel(sink_ref, q_ref, k_ref, v_ref, c_ref, slo_ref, shi_ref,
                       o_ref, klast_ref, vlast_ref, kprev_ref, vprev_ref):
    n = pl.program_id(1)
    nb = pl.num_programs(1)
    w = WINDOW

    @pl.when(n == 0)
    def _():
        kprev_ref[...] = jnp.zeros_like(kprev_ref)
        vprev_ref[...] = jnp.zeros_like(vprev_ref)

    kvw = SWA_KV
    c, s_lo, s_hi = c_ref[...], slo_ref[...], shi_ref[...]
    q = (_rope(q_ref[...].astype(F32), c, s_lo, s_hi) * (HEAD_DIM ** -0.5)).astype(BF16)
    k = _rope(k_ref[...].astype(F32), c, s_lo, s_hi)
    v = v_ref[...].astype(F32)
    kc = jnp.concatenate([kprev_ref[...], k], axis=0).astype(BF16)
    vc = jnp.concatenate([vprev_ref[...], v], axis=0).astype(BF16)

    rows = lax.broadcasted_iota(jnp.int32, (w, 2 * w), 0)
    cols = lax.broadcasted_iota(jnp.int32, (w, 2 * w), 1)
    diff = rows + w - cols
    bias = jnp.where((diff >= 0) & (diff < w) & ((cols >= w) | (n > 0)), 0.0, -jnp.inf)
    bias = jnp.concatenate([bias] * GROUP, axis=0)
    head_shift = HEAD_DIM.bit_length() - 1
    lane_head_q = lax.broadcasted_iota(jnp.int32, (w, kvw), 1) >> head_shift
    lane_head = lax.broadcasted_iota(jnp.int32, (2 * w, kvw), 1) >> head_shift

    for g in range(N_KV_HEADS):
        in_g_q = jnp.where(lane_head_q == g, 1.0, 0.0).astype(BF16)
        in_g = jnp.where(lane_head == g, 1.0, 0.0).astype(BF16)
        qg = jnp.concatenate([q[:, h * kvw:(h + 1) * kvw] * in_g_q for h in range(GROUP)], axis=0)
        sk = jnp.concatenate([jnp.full((w, LANES), sink_ref[0, g * GROUP + h], F32) for h in range(GROUP)], axis=0)
        sk2 = jnp.concatenate([sk, sk], axis=1)
        s = _dot_nt(qg, kc) + bias
        m = jnp.broadcast_to(jnp.max(jnp.maximum(s, sk2), axis=-1, keepdims=True), sk.shape)
        p = jnp.exp(s - jnp.concatenate([m, m], axis=1)).astype(BF16)
        vg = vc * in_g + (1.0 - in_g)
        oa = _dot(p, vg)
        p_sink = jnp.exp(sk - m)
        denom = pltpu.roll(oa, 2 * HEAD_DIM, 1) + jnp.concatenate([p_sink, p_sink], axis=1)
        o = (oa / denom).astype(o_ref.dtype)
        for h in range(GROUP):
            lo = g * HEAD_DIM
            o_ref[:, h * kvw + lo:h * kvw + lo + HEAD_DIM] = o[h * w:(h + 1) * w, lo:lo + HEAD_DIM]

    kprev_ref[...] = k
    vprev_ref[...] = v

    @pl.when(n == nb - 1)
    def _():
        klast_ref[...] = k
        vlast_ref[...] = v


def _swa_prompt(z3, sink, tables):
    b, t, _ = z3.shape
    w = WINDOW
    tab_spec = pl.BlockSpec((w, LANES), lambda i, n: (n, 0))
    return pl.pallas_call(
        _swa_prompt_kernel,
        grid=(b, t // w),
        in_specs=[
            pl.BlockSpec(memory_space=pltpu.SMEM),
            pl.BlockSpec((None, w, SWA_Q), lambda i, n: (i, n, OFF_QS // SWA_Q)),
            pl.BlockSpec((None, w, SWA_KV), lambda i, n: (i, n, OFF_KS // SWA_KV)),
            pl.BlockSpec((None, w, SWA_KV), lambda i, n: (i, n, OFF_VS // SWA_KV)),
            tab_spec, tab_spec, tab_spec,
        ],
        out_specs=[
            pl.BlockSpec((None, w, SWA_Q), lambda i, n: (i, n, 0)),
            pl.BlockSpec((None, w, SWA_KV), lambda i, n: (i, 0, 0)),
            pl.BlockSpec((None, w, SWA_KV), lambda i, n: (i, 0, 0)),
        ],
        out_shape=[
            jax.ShapeDtypeStruct((b, t, SWA_Q), BF16),
            jax.ShapeDtypeStruct((b, w, SWA_KV), F32),
            jax.ShapeDtypeStruct((b, w, SWA_KV), F32),
        ],
        scratch_shapes=[pltpu.VMEM((w, SWA_KV), F32), pltpu.VMEM((w, SWA_KV), F32)],
        compiler_params=_cparams(("arbitrary", "arbitrary")),
        name="swa_prompt",
    )(sink, z3, z3, z3, *tables)


SAMPLE_T = 4
PAIR_ROWS = 2 * SAMPLE_T
KPAD = 2 * WINDOW


def _swa_sample_kernel(q_ref, k_ref, v_ref, ck_ref, cv_ref, c_ref, slo_ref, shi_ref, sink_ref,
                       o_ref, nk_ref, nv_ref, *, pairs):
    w = WINDOW
    kvw = SWA_KV
    c, s_lo, s_hi = c_ref[...], slo_ref[...], shi_ref[...]
    nrow = N_HEADS * PAIR_ROWS
    row = lax.broadcasted_iota(jnp.int32, (nrow, KPAD), 0)
    col = lax.broadcasted_iota(jnp.int32, (nrow, KPAD), 1)
    t_shift = SAMPLE_T.bit_length() - 1
    sq = row & (SAMPLE_T - 1)
    par = (row >> t_shift) & 1
    jj = col - w
    mask = ((col < w) & (col > sq)) | (
        (jj >= 0) & (jj < PAIR_ROWS) & ((jj >> t_shift) == par) & ((jj & (SAMPLE_T - 1)) <= sq))
    lane = lax.broadcasted_iota(jnp.int32, (PAIR_ROWS, kvw), 1) >> (HEAD_DIM.bit_length() - 1)
    row8 = lax.broadcasted_iota(jnp.int32, (PAIR_ROWS, kvw), 0)
    sk = sink_ref[...][:, :1]
    zpad = jnp.zeros((KPAD - w - PAIR_ROWS, kvw), F32)

    for p in range(pairs):
        rs = slice(p * PAIR_ROWS, (p + 1) * PAIR_ROWS)
        q8 = _rope(q_ref[rs, :], c, s_lo, s_hi) * (HEAD_DIM ** -0.5)
        k8 = _rope(k_ref[rs, :], c, s_lo, s_hi)
        v8 = v_ref[rs, :]
        blocks = [jnp.where(lane == g, q8[:, h * kvw:(h + 1) * kvw], 0.0)
                  for g in range(N_KV_HEADS) for h in range(GROUP)]
        qall = jnp.concatenate(blocks, axis=0).astype(BF16)

        s_par, v_par = [], []
        for e in range(2):
            bd = 2 * p + e
            kall = jnp.concatenate([ck_ref[bd], k8, zpad], axis=0).astype(BF16)
            v_par.append(jnp.concatenate([cv_ref[bd], v8, zpad], axis=0).astype(BF16))
            s_par.append(_dot_nt(qall, kall))
        s = jnp.where(par == 1, s_par[1], s_par[0])
        s = jnp.where(mask, s, -jnp.inf)
        m = jnp.maximum(jnp.max(s, axis=-1, keepdims=True), sk)
        pr = jnp.exp(s - m)
        denom = jnp.sum(pr, axis=-1, keepdims=True) + jnp.exp(sk - m)
        p0 = jnp.where(par == 0, pr, 0.0).astype(BF16)
        p1 = jnp.where(par == 1, pr, 0.0).astype(BF16)
        oall = (_dot(p0, v_par[0]) + _dot(p1, v_par[1])) / denom

        outs = []
        for h in range(GROUP):
            acc = None
            for g in range(N_KV_HEADS):
                hh = g * GROUP + h
                blk = jnp.where(lane == g, oall[hh * PAIR_ROWS:(hh + 1) * PAIR_ROWS], 0.0)
                acc = blk if acc is None else acc + blk
            outs.append(acc)
        o_ref[rs, :] = jnp.concatenate(outs, axis=1).astype(o_ref.dtype)

        for e in range(2):
            bd = 2 * p + e
            for new8, cref, nref in ((k8, ck_ref, nk_ref), (v8, cv_ref, nv_ref)):
                shifted = pltpu.roll(cref[bd], w - SAMPLE_T, 0)
                tail8 = new8 if e == 1 else pltpu.roll(new8, SAMPLE_T, 0)
                tail = jnp.where(row8 >= SAMPLE_T, tail8, shifted[w - PAIR_ROWS:])
                nref[bd] = jnp.concatenate([shifted[:w - PAIR_ROWS], tail], axis=0)


def _swa_sample(zs, cache_k, cache_v, sink_rows, tables, pairs):
    n = zs.shape[0]
    bd, w, kvw = cache_k.shape
    rows = pairs * PAIR_ROWS
    tab_spec = pl.BlockSpec((PAIR_ROWS, LANES), lambda i: (0, 0))
    cache_spec = pl.BlockSpec((2 * pairs, w, kvw), lambda i: (i, 0, 0))
    return pl.pallas_call(
        functools.partial(_swa_sample_kernel, pairs=pairs),
        grid=(n // rows,),
        in_specs=[
            pl.BlockSpec((rows, SWA_Q), lambda i: (i, OFF_QS // SWA_Q)),
            pl.BlockSpec((rows, SWA_KV), lambda i: (i, OFF_KS // SWA_KV)),
            pl.BlockSpec((rows, SWA_KV), lambda i: (i, OFF_VS // SWA_KV)),
            cache_spec, cache_spec,
            tab_spec, tab_spec, tab_spec,
            pl.BlockSpec((N_HEADS * PAIR_ROWS, LANES), lambda i: (0, 0)),
        ],
        out_specs=[
            pl.BlockSpec((rows, SWA_Q), lambda i: (i, 0)),
            cache_spec, cache_spec,
        ],
        out_shape=[
            jax.ShapeDtypeStruct((n, SWA_Q), BF16),
            jax.ShapeDtypeStruct((bd, w, kvw), F32),
            jax.ShapeDtypeStruct((bd, w, kvw), F32),
        ],
        compiler_params=_cparams(("arbitrary",)),
        name="swa_sample",
    )(zs, zs, zs, cache_k, cache_v, *tables, sink_rows)


def _log_decay(ag, wa2, ba):
    x = _dot(ag.astype(BF16), wa2) + ba
    log_sig = jnp.minimum(x, 0.0) - jnp.log1p(jnp.exp(-jnp.abs(x)))
    return log_sig / GLA_GATE_NORM


def _gla_out(o, gnorm, rg):
    o = o * lax.rsqrt(jnp.mean(o * o, axis=-1, keepdims=True) + EPS) * gnorm
    return o * (rg * _sigmoid(rg))


def _gla_prompt_kernel(q_ref, k_ref, v_ref, rg_ref, ag_ref, wa2_ref, ba_ref, gn_ref,
                       o_ref, sout_ref, s_ref, *, chunk):
    t = pl.program_id(1)
    nt = pl.num_programs(1)
    tt = q_ref.shape[0]
    dk, dv = GLA_DK, GLA_DV

    @pl.when(t == 0)
    def _():
        s_ref[...] = jnp.zeros_like(s_ref)

    la_all = _log_decay(ag_ref[...], wa2_ref[...], ba_ref[...])
    ri = lax.broadcasted_iota(jnp.int32, (chunk, chunk), 0)
    ci = lax.broadcasted_iota(jnp.int32, (chunk, chunk), 1)
    causal = ri >= ci
    tril = jnp.where(causal, 1.0, 0.0).astype(BF16)
    ones = jnp.ones((chunk, LANES), BF16)
    gn = gn_ref[...]
    state = [s_ref[h] for h in range(GLA_HEADS)]

    for c in range(tt // chunk):
        sl = slice(c * chunk, (c + 1) * chunk)
        for h in range(GLA_HEADS):
            ks = slice(h * dk, (h + 1) * dk)
            vs = slice(h * dv, (h + 1) * dv)
            hi, lo = _split_bf16(la_all[sl, ks])
            b = _dot(tril, hi) + _dot(tril, lo)
            b_last = b[chunk - 1:chunk, :]
            b_last_col = _dot_tn(hi, ones) + _dot_tn(lo, ones)
            decay = jnp.exp(b_last_col)
            q = q_ref[sl, ks].astype(F32) * (dk ** -0.5)
            k = k_ref[sl, ks].astype(F32)
            v = v_ref[sl, vs].astype(BF16)
            q_i = (q * jnp.exp(b)).astype(BF16)
            k_i = (k * jnp.exp(-b)).astype(BF16)
            k_d = (k * jnp.exp(b_last - b)).astype(BF16)
            a = jnp.where(causal, _dot_nt(q_i, k_i), 0.0).astype(BF16)
            s_old = state[h]
            o = _dot(a, v) + _dot(q_i, s_old.astype(BF16))
            decay_full = jnp.concatenate([decay] * (dv // LANES), axis=1)
            state[h] = decay_full * s_old + _dot_tn(k_d, v)
            o_ref[sl, vs] = _gla_out(o, gn, rg_ref[sl, vs].astype(F32)).astype(o_ref.dtype)

    for h in range(GLA_HEADS):
        s_ref[h] = state[h]

    @pl.when(t == nt - 1)
    def _():
        sout_ref[...] = s_ref[...]


def _gla_prompt(z3, zag3, wa2, ba, gnorm, tt, chunk):
    b, t, _ = z3.shape
    h = GLA_HEADS
    return pl.pallas_call(
        functools.partial(_gla_prompt_kernel, chunk=chunk),
        grid=(b, t // tt),
        in_specs=[
            pl.BlockSpec((None, tt, GLA_DK_TOTAL), lambda i, s: (i, s, OFF_QG // GLA_DK_TOTAL)),
            pl.BlockSpec((None, tt, GLA_DK_TOTAL), lambda i, s: (i, s, OFF_KG // GLA_DK_TOTAL)),
            pl.BlockSpec((None, tt, GLA_DV_TOTAL), lambda i, s: (i, s, OFF_VG // GLA_DV_TOTAL)),
            pl.BlockSpec((None, tt, GLA_DV_TOTAL), lambda i, s: (i, s, OFF_RG // GLA_DV_TOTAL)),
            pl.BlockSpec((None, tt, LANES), lambda i, s: (i, s, 0)),
            pl.BlockSpec((LANES, GLA_DK_TOTAL), lambda i, s: (0, 0)),
            pl.BlockSpec((1, GLA_DK_TOTAL), lambda i, s: (0, 0)),
            pl.BlockSpec((1, GLA_DV), lambda i, s: (0, 0)),
        ],
        out_specs=[
            pl.BlockSpec((None, tt, GLA_DV_TOTAL), lambda i, s: (i, s, 0)),
            pl.BlockSpec((None, h, GLA_DK, GLA_DV), lambda i, s: (i, 0, 0, 0)),
        ],
        out_shape=[
            jax.ShapeDtypeStruct((b, t, GLA_DV_TOTAL), BF16),
            jax.ShapeDtypeStruct((b, h, GLA_DK, GLA_DV), F32),
        ],
        scratch_shapes=[pltpu.VMEM((h, GLA_DK, GLA_DV), F32)],
        compiler_params=_cparams(("arbitrary", "arbitrary")),
        name="gla_prompt",
    )(z3, z3, z3, z3, zag3, wa2, ba, gnorm)


def _gla_sample_kernel(q_ref, k_ref, v_ref, rg_ref, ag_ref, wa2_ref, ba_ref, gn_ref, s0_ref,
                       o_ref, s1_ref):
    dk, dv = GLA_DK, GLA_DV
    r_k = lax.broadcasted_iota(jnp.int32, (PAIR_ROWS, dk), 0)
    r_v = lax.broadcasted_iota(jnp.int32, (PAIR_ROWS, dv), 0)
    step_k = r_k & (SAMPLE_T - 1)
    step_v = r_v & (SAMPLE_T - 1)
    odd_k = r_k >= SAMPLE_T
    odd_v = r_v >= SAMPLE_T
    la_all = _log_decay(ag_ref[...], wa2_ref[...], ba_ref[...])
    gn = gn_ref[...]
    kpad = jnp.zeros((LANES - PAIR_ROWS, dk), F32)
    vpad = jnp.zeros((LANES - PAIR_ROWS, dv), BF16)
    ones = jnp.ones((LANES, LANES), BF16)

    for h in range(GLA_HEADS):
        la = la_all[:, h * dk:(h + 1) * dk]
        b = la + jnp.where(step_k >= 1, pltpu.roll(la, 1, 0), 0.0)
        b = b + jnp.where(step_k >= 2, pltpu.roll(b, 2, 0), 0.0)
        b_last = jnp.where(odd_k, b[PAIR_ROWS - 1:PAIR_ROWS, :], b[SAMPLE_T - 1:SAMPLE_T, :])
        q = q_ref[:, h * dk:(h + 1) * dk] * (dk ** -0.5)
        k = k_ref[:, h * dk:(h + 1) * dk]
        v = v_ref[:, h * dv:(h + 1) * dv]
        q_i = q * jnp.exp(b)
        k_i = k * jnp.exp(-b)
        k_d = k * jnp.exp(b_last - b)
        o = jnp.sum(q_i * k_i, axis=-1, keepdims=True) * v
        for d in range(1, SAMPLE_T):
            a_d = jnp.sum(q_i * pltpu.roll(k_i, d, 0), axis=-1, keepdims=True)
            o = o + jnp.where(step_v >= d, a_d * pltpu.roll(v, d, 0), 0.0)
        q_b = q_i.astype(BF16)
        vb = jnp.concatenate([v.astype(BF16), vpad], axis=0)
        o_par = []
        for e in range(2):
            s_old = s0_ref[e, h]
            o_par.append(_dot(q_b, s_old.astype(BF16)))
            sel = (r_k >= SAMPLE_T) if e == 1 else (r_k < SAMPLE_T)
            kd_e = jnp.concatenate([jnp.where(sel, k_d, 0.0), kpad], axis=0).astype(BF16)
            la_e = jnp.concatenate([jnp.where(sel, la, 0.0), kpad], axis=0)
            hi, lo = _split_bf16(la_e)
            decay = jnp.exp(_dot_tn(hi, ones) + _dot_tn(lo, ones))
            decay_full = jnp.concatenate([decay] * (dv // LANES), axis=1)
            s1_ref[e, h] = decay_full * s_old + _dot_tn(kd_e, vb)
        o = o + jnp.where(odd_v, o_par[1], o_par[0])
        o_ref[:, h * dv:(h + 1) * dv] = _gla_out(o, gn, rg_ref[:, h * dv:(h + 1) * dv]).astype(o_ref.dtype)


def _gla_sample(zs, zag, wa2, ba, gnorm, state):
    n = zs.shape[0]
    bd, h, dk, dv = state.shape
    state_spec = pl.BlockSpec((2, h, dk, dv), lambda i: (i, 0, 0, 0))
    return pl.pallas_call(
        _gla_sample_kernel,
        grid=(n // PAIR_ROWS,),
        in_specs=[
            pl.BlockSpec((PAIR_ROWS, GLA_DK_TOTAL), lambda i: (i, OFF_QG // GLA_DK_TOTAL)),
            pl.BlockSpec((PAIR_ROWS, GLA_DK_TOTAL), lambda i: (i, OFF_KG // GLA_DK_TOTAL)),
            pl.BlockSpec((PAIR_ROWS, GLA_DV_TOTAL), lambda i: (i, OFF_VG // GLA_DV_TOTAL)),
            pl.BlockSpec((PAIR_ROWS, GLA_DV_TOTAL), lambda i: (i, OFF_RG // GLA_DV_TOTAL)),
            pl.BlockSpec((PAIR_ROWS, LANES), lambda i: (i, 0)),
            pl.BlockSpec((LANES, GLA_DK_TOTAL), lambda i: (0, 0)),
            pl.BlockSpec((1, GLA_DK_TOTAL), lambda i: (0, 0)),
            pl.BlockSpec((1, GLA_DV), lambda i: (0, 0)),
            state_spec,
        ],
        out_specs=[
            pl.BlockSpec((PAIR_ROWS, GLA_DV_TOTAL), lambda i: (i, 0)),
            state_spec,
        ],
        out_shape=[
            jax.ShapeDtypeStruct((n, GLA_DV_TOTAL), BF16),
            jax.ShapeDtypeStruct(state.shape, F32),
        ],
        compiler_params=_cparams(("arbitrary",)),
        name="gla_sample",
    )(zs, zs, zs, zs, zag, wa2, ba, gnorm, state)


def _merge_kernel(osw_ref, og_ref, gs_ref, gg_ref, x_ref, pswa_ref, pgla_ref, wo_ref, n2_ref,
                  x1_ref, h2_ref):
    a = _dot(osw_ref[...], pswa_ref[...])
    b = _dot(og_ref[...], pgla_ref[...])
    y = _sigmoid(gs_ref[...].astype(F32)) * a + _sigmoid(gg_ref[...].astype(F32)) * b
    x1 = x_ref[...] + _dot(y.astype(BF16), wo_ref[...])
    x1_ref[...] = x1
    h2_ref[...] = _rms(x1, n2_ref[...]).astype(BF16)


def _merge(o_swa, o_gla, z, x2d, p_swa, p_gla, w_o, norm2, tm):
    n = x2d.shape[0]
    resident = functools.partial(pl.BlockSpec, pipeline_mode=pl.Buffered(1))
    return pl.pallas_call(
        _merge_kernel,
        grid=(n // tm,),
        in_specs=[
            pl.BlockSpec((tm, SWA_Q), lambda i: (i, 0)),
            pl.BlockSpec((tm, GLA_DV_TOTAL), lambda i: (i, 0)),
            pl.BlockSpec((tm, D_MODEL), lambda i: (i, OFF_GS // D_MODEL)),
            pl.BlockSpec((tm, D_MODEL), lambda i: (i, OFF_GG // D_MODEL)),
            pl.BlockSpec((tm, D_MODEL), lambda i: (i, 0)),
            resident((SWA_Q, D_MODEL), lambda i: (0, 0)),
            resident((GLA_DV_TOTAL, D_MODEL), lambda i: (0, 0)),
            resident((D_MODEL, D_MODEL), lambda i: (0, 0)),
            pl.BlockSpec((1, D_MODEL), lambda i: (0, 0)),
        ],
        out_specs=[
            pl.BlockSpec((tm, D_MODEL), lambda i: (i, 0)),
            pl.BlockSpec((tm, D_MODEL), lambda i: (i, 0)),
        ],
        out_shape=[
            jax.ShapeDtypeStruct((n, D_MODEL), F32),
            jax.ShapeDtypeStruct((n, D_MODEL), BF16),
        ],
        compiler_params=_cparams(("arbitrary",)),
        name="merge",
    )(o_swa, o_gla, z, z, x2d, p_swa, p_gla, w_o, norm2)


def _mlp_kernel(h2_ref, x1_ref, wup_ref, wdn_ref, fn_ref, out_ref, acc_ref):
    f = pl.program_id(1)
    nf = pl.num_programs(1)
    @pl.when(f == 0)
    def _():
        acc_ref[...] = jnp.zeros_like(acc_ref)

    u = _dot(h2_ref[...], wup_ref[...])
    u = jnp.square(jnp.maximum(u, 0.0)).astype(BF16)
    acc_ref[...] += _dot(u, wdn_ref[...])

    @pl.when(f == nf - 1)
    def _():
        out_ref[...] = _rms(x1_ref[...] + acc_ref[...], fn_ref[...])


def _mlp(h2, x1, w_up, w_down, final_norm, tm, tf):
    n = h2.shape[0]
    return pl.pallas_call(
        _mlp_kernel,
        grid=(n // tm, D_FF // tf),
        in_specs=[
            pl.BlockSpec((tm, D_MODEL), lambda i, f: (i, 0)),
            pl.BlockSpec((tm, D_MODEL), lambda i, f: (i, 0)),
            pl.BlockSpec((D_MODEL, tf), lambda i, f: (0, f)),
            pl.BlockSpec((tf, D_MODEL), lambda i, f: (f, 0)),
            pl.BlockSpec((1, D_MODEL), lambda i, f: (0, 0)),
        ],
        out_specs=pl.BlockSpec((tm, D_MODEL), lambda i, f: (i, 0)),
        out_shape=jax.ShapeDtypeStruct((n, D_MODEL), F32),
        scratch_shapes=[pltpu.VMEM((tm, D_MODEL), F32)],
        compiler_params=_cparams(("arbitrary", "arbitrary")),
        name="mlp",
    )(h2, x1, w_up, w_down, final_norm)


def _pick_tile(n, pref):
    t = min(n, pref)
    while n % t:
        t //= 2
    return t


def _reorder_w_in(w_in):
    o = 0
    parts = {}
    for name, width in (("qs", SWA_Q), ("ks", SWA_KV), ("vs", SWA_KV), ("qg", GLA_DK_TOTAL),
                        ("kg", GLA_DK_TOTAL), ("vg", GLA_DV_TOTAL), ("rg", GLA_DV_TOTAL),
                        ("ag", GLA_GATE_RANK), ("gs", D_MODEL), ("gg", D_MODEL)):
        parts[name] = w_in[:, o:o + width]
        o += width
    parts["qs"] = parts["qs"].reshape(-1, N_KV_HEADS, GROUP, HEAD_DIM).transpose(0, 2, 1, 3).reshape(-1, SWA_Q)
    w_main = jnp.concatenate([parts[nm] for nm in ("gs", "gg", "vg", "rg", "qg", "kg", "qs", "ks", "vs")],
                             axis=1).astype(BF16)
    w_ag = jnp.pad(parts["ag"], ((0, 0), (0, LANES - GLA_GATE_RANK))).astype(BF16)
    return w_main, w_ag


def kernel(x_prompt, x_sample, cache_swa_k, cache_swa_v, state_gla, norm1, w_in, w_a2, b_a, sink,
           gla_norm, p_swa, p_gla, w_o, norm2, w_up, w_down, final_norm):
    assert norm1.shape[0] == 1, "single-layer stack"
    bp, tp, d = x_prompt.shape
    bs, ts, _ = x_sample.shape
    assert ts == SAMPLE_T and bs % 2 == 0 and tp % WINDOW == 0
    w_buf = cache_swa_k.shape[2]
    assert w_buf == WINDOW

    w_main, w_ag = _reorder_w_in(w_in[0])
    wa2 = jnp.pad(w_a2[0], ((0, LANES - GLA_GATE_RANK), (0, 0))).astype(BF16)
    ba = b_a[0][None, :]
    n1, n2, fn = norm1[0][None, :], norm2[0][None, :], final_norm[None, :]
    gn = gla_norm[0][None, :]
    pswa = p_swa[0].reshape(N_KV_HEADS, GROUP, HEAD_DIM, d).transpose(1, 0, 2, 3).reshape(SWA_Q, d).astype(BF16)
    pgla, wo = p_gla[0].astype(BF16), w_o[0].astype(BF16)
    wup, wdn = w_up[0].astype(BF16), w_down[0].astype(BF16)
    sink_smem = sink[0][None, :]
    sink_rows = jnp.broadcast_to(jnp.repeat(sink[0], PAIR_ROWS)[:, None], (N_HEADS * PAIR_ROWS, LANES))

    xp = x_prompt.reshape(bp * tp, d)
    xs = x_sample.reshape(bs * ts, d)
    np_, ns = xp.shape[0], xs.shape[0]

    zp, zagp = _in_proj(xp, n1, w_main, w_ag, _pick_tile(np_, 1024), 512, BF16)
    zs, zags = _in_proj(xs, n1, w_main, w_ag, _pick_tile(ns, 512), 512, F32)

    zp3 = zp.reshape(bp, tp, Z_COLS)
    o_swa_p, k_last, v_last = _swa_prompt(zp3, sink_smem, _rope_tables(jnp.arange(tp)))
    pos_s = PAST_LEN + jnp.arange(ts)
    tabs_s = _rope_tables(jnp.concatenate([pos_s, pos_s]))
    pairs = _pick_tile(bs // 2, 4)
    o_swa_s, nk_s, nv_s = _swa_sample(zs, cache_swa_k[0].reshape(bs, w_buf, SWA_KV),
                                      cache_swa_v[0].reshape(bs, w_buf, SWA_KV), sink_rows, tabs_s, pairs)

    tt = _pick_tile(tp, 256)
    o_gla_p, s_p = _gla_prompt(zp3, zagp.reshape(bp, tp, LANES), wa2, ba, gn, tt, min(GLA_CHUNK, tt))
    o_gla_s, s_s = _gla_sample(zs, zags, wa2, ba, gn, state_gla[0])

    x1p, h2p = _merge(o_swa_p.reshape(np_, SWA_Q), o_gla_p.reshape(np_, GLA_DV_TOTAL), zp, xp,
                      pswa, pgla, wo, n2, _pick_tile(np_, 256))
    x1s, h2s = _merge(o_swa_s, o_gla_s, zs, xs, pswa, pgla, wo, n2, _pick_tile(ns, 256))
    yp = _mlp(h2p, x1p, wup, wdn, fn, _pick_tile(np_, 512), 1024)
    ys = _mlp(h2s, x1s, wup, wdn, fn, _pick_tile(ns, 512), 1024)

    kv5 = lambda a, nb: a.reshape(1, nb, w_buf, N_KV_HEADS, HEAD_DIM)
    return (yp.reshape(bp, tp, d), ys.reshape(bs, ts, d),
            kv5(k_last, bp), kv5(v_last, bp), s_p[None],
            kv5(nk_s, bs), kv5(nv_s, bs), s_s[None])
```

```python
import functools

import jax
import jax.numpy as jnp
import numpy as np
from jax import lax
from jax.experimental import pallas as pl
from jax.experimental.pallas import tpu as pltpu

F32 = jnp.float32
BF16 = jnp.bfloat16

D_MODEL = 2048
PAST_LEN = 8192
N_HEADS = 16
N_KV_HEADS = 4
GROUP = N_HEADS // N_KV_HEADS
HEAD_DIM = 64
WINDOW = 128
ROT_DIM = HEAD_DIM // 4
ROPE_THETA = 500000.0
SWA_Q = N_HEADS * HEAD_DIM
SWA_KV = N_KV_HEADS * HEAD_DIM
GLA_HEADS = 4
GLA_DK = 256
GLA_DV = 512
GLA_DK_TOTAL = GLA_HEADS * GLA_DK
GLA_DV_TOTAL = GLA_HEADS * GLA_DV
GLA_GATE_RANK = 16
GLA_GATE_NORM = 16.0
D_FF = 4 * D_MODEL
EPS = 1e-6

LANES = 128
SUBLANES = 8
VMEM_LIMIT = 56 * 1024 * 1024

OFF_GS = 0
OFF_GG = OFF_GS + D_MODEL
OFF_VG = OFF_GG + D_MODEL
OFF_RG = OFF_VG + GLA_DV_TOTAL
OFF_QG = OFF_RG + GLA_DV_TOTAL
OFF_KG = OFF_QG + GLA_DK_TOTAL
OFF_QS = OFF_KG + GLA_DK_TOTAL
OFF_KS = OFF_QS + SWA_Q
OFF_VS = OFF_KS + SWA_KV
Z_COLS = OFF_VS + SWA_KV

GLA_BLOCK = 16
GLA_TILE = 256


def _cparams(sem):
    return pltpu.CompilerParams(dimension_semantics=sem, vmem_limit_bytes=VMEM_LIMIT)


def _rms(x, w):
    return x * lax.rsqrt(jnp.mean(x * x, axis=-1, keepdims=True) + EPS) * w


def _sigmoid(x):
    return 1.0 / (1.0 + jnp.exp(-x))


def _dot(a, b):
    return jnp.dot(a, b, preferred_element_type=F32)


def _dot_nt(a, b):
    return lax.dot_general(a, b, (((1,), (1,)), ((), ())), preferred_element_type=F32)


def _dot_tn(a, b):
    return lax.dot_general(a, b, (((0,), (0,)), ((), ())), preferred_element_type=F32)


def _split_bf16(x):
    hi = x.astype(BF16)
    lo = (x - hi.astype(F32)).astype(BF16)
    return hi, lo


def _in_proj_kernel(x_ref, nw_ref, w_ref, wag_ref, z_ref, zag_ref, h_ref):
    @pl.when(pl.program_id(1) == 0)
    def _():
        h = _rms(x_ref[...], nw_ref[...]).astype(BF16)
        h_ref[...] = h
        zag_ref[...] = _dot(h, wag_ref[...])

    z_ref[...] = _dot(h_ref[...], w_ref[...]).astype(z_ref.dtype)


def _in_proj(x2d, norm1, w_main, w_ag, tm, tn, z_dtype):
    n = x2d.shape[0]
    return pl.pallas_call(
        _in_proj_kernel,
        grid=(n // tm, Z_COLS // tn),
        in_specs=[
            pl.BlockSpec((tm, D_MODEL), lambda i, j: (i, 0)),
            pl.BlockSpec((1, D_MODEL), lambda i, j: (0, 0)),
            pl.BlockSpec((D_MODEL, tn), lambda i, j: (0, j)),
            pl.BlockSpec((D_MODEL, LANES), lambda i, j: (0, 0)),
        ],
        out_specs=[
            pl.BlockSpec((tm, tn), lambda i, j: (i, j)),
            pl.BlockSpec((tm, LANES), lambda i, j: (i, 0)),
        ],
        out_shape=[
            jax.ShapeDtypeStruct((n, Z_COLS), z_dtype),
            jax.ShapeDtypeStruct((n, LANES), F32),
        ],
        scratch_shapes=[pltpu.VMEM((tm, D_MODEL), BF16)],
        compiler_params=_cparams(("arbitrary", "arbitrary")),
        name="in_proj",
    )(x2d, norm1, w_main, w_ag)


def _rope_tables(pos):
    half = ROT_DIM // 2
    inv = ROPE_THETA ** (-jnp.arange(half, dtype=F32) * 2.0 / ROT_DIM)
    ang = pos.astype(F32)[:, None] * inv[None, :]
    cos, sin = jnp.cos(ang), jnp.sin(ang)
    t = pos.shape[0]
    ones = jnp.ones((t, HEAD_DIM - ROT_DIM), F32)
    zeros = jnp.zeros((t, HEAD_DIM - ROT_DIM), F32)
    zh = jnp.zeros((t, half), F32)
    c = jnp.concatenate([cos, cos, ones], axis=1)
    s_lo = jnp.concatenate([zh, sin, zeros], axis=1)
    s_hi = jnp.concatenate([-sin, zh, zeros], axis=1)
    rep = LANES // HEAD_DIM
    return tuple(jnp.tile(a, (1, rep)) for a in (c, s_lo, s_hi))


def _rope(x, c, s_lo, s_hi):
    half = ROT_DIM // 2
    outs = []
    for j in range(x.shape[1] // LANES):
        xc = x[:, j * LANES:(j + 1) * LANES]
        outs.append(xc * c + pltpu.roll(xc, half, 1) * s_lo + pltpu.roll(xc, LANES - half, 1) * s_hi)
    return outs[0] if len(outs) == 1 else jnp.concatenate(outs, axis=1)


def _swa_prompt_kernel(sink_ref, q_ref, k_ref, v_ref, c_ref, slo_ref, shi_ref,
                       o_ref, klast_ref, vlast_ref, kprev_ref, vprev_ref):
    n = pl.program_id(1)
    nb = pl.num_programs(1)
    w = WINDOW

    @pl.when(n == 0)
    def _():
        kprev_ref[...] = jnp.zeros_like(kprev_ref)
        vprev_ref[...] = jnp.zeros_like(vprev_ref)

    kvw = SWA_KV
    c, s_lo, s_hi = c_ref[...], slo_ref[...], shi_ref[...]
    q = (_rope(q_ref[...].astype(F32), c, s_lo, s_hi) * (HEAD_DIM ** -0.5)).astype(BF16)
    k = _rope(k_ref[...].astype(F32), c, s_lo, s_hi)
    v = v_ref[...].astype(F32)
    kc = jnp.concatenate([kprev_ref[...], k], axis=0).astype(BF16)
    vc = jnp.concatenate([vprev_ref[...], v], axis=0).astype(BF16)

    rows = lax.broadcasted_iota(jnp.int32, (w, 2 * w), 0)
    cols = lax.broadcasted_iota(jnp.int32, (w, 2 * w), 1)
    diff = rows + w - cols
    bias = jnp.where((diff >= 0) & (diff < w) & ((cols >= w) | (n > 0)), 0.0, -jnp.inf)
    bias = jnp.concatenate([bias] * GROUP, axis=0)
    head_shift = HEAD_DIM.bit_length() - 1
    lane_head_q = lax.broadcasted_iota(jnp.int32, (w, kvw), 1) >> head_shift
    lane_head = lax.broadcasted_iota(jnp.int32, (2 * w, kvw), 1) >> head_shift

    for g in range(N_KV_HEADS):
        in_g_q = jnp.where(lane_head_q == g, 1.0, 0.0).astype(BF16)
        in_g = jnp.where(lane_head == g, 1.0, 0.0).astype(BF16)
        qg = jnp.concatenate([q[:, h * kvw:(h + 1) * kvw] * in_g_q for h in range(GROUP)], axis=0)
        sk = jnp.concatenate([jnp.full((w, LANES), sink_ref[0, g * GROUP + h], F32) for h in range(GROUP)], axis=0)
        sk2 = jnp.concatenate([sk, sk], axis=1)
        s = _dot_nt(qg, kc) + bias
        m = jnp.broadcast_to(jnp.max(jnp.maximum(s, sk2), axis=-1, keepdims=True), sk.shape)
        p = jnp.exp(s - jnp.concatenate([m, m], axis=1)).astype(BF16)
        vg = vc * in_g + (1.0 - in_g)
        oa = _dot(p, vg)
        p_sink = jnp.exp(sk - m)
        denom = pltpu.roll(oa, 2 * HEAD_DIM, 1) + jnp.concatenate([p_sink, p_sink], axis=1)
        o = (oa / denom).astype(o_ref.dtype)
        for h in range(GROUP):
            lo = g * HEAD_DIM
            o_ref[:, h * kvw + lo:h * kvw + lo + HEAD_DIM] = o[h * w:(h + 1) * w, lo:lo + HEAD_DIM]

    kprev_ref[...] = k
    vprev_ref[...] = v

    @pl.when(n == nb - 1)
    def _():
        klast_ref[...] = k
        vlast_ref[...] = v


def _swa_prompt(z3, sink, tables):
    b, t, _ = z3.shape
    w = WINDOW
    tab_spec = pl.BlockSpec((w, LANES), lambda i, n: (n, 0))
    return pl.pallas_call(
        _swa_prompt_kernel,
        grid=(b, t // w),
        in_specs=[
            pl.BlockSpec(memory_space=pltpu.SMEM),
            pl.BlockSpec((None, w, SWA_Q), lambda i, n: (i, n, OFF_QS // SWA_Q)),
            pl.BlockSpec((None, w, SWA_KV), lambda i, n: (i, n, OFF_KS // SWA_KV)),
            pl.BlockSpec((None, w, SWA_KV), lambda i, n: (i, n, OFF_VS // SWA_KV)),
            tab_spec, tab_spec, tab_spec,
        ],
        out_specs=[
            pl.BlockSpec((None, w, SWA_Q), lambda i, n: (i, n, 0)),
            pl.BlockSpec((None, w, SWA_KV), lambda i, n: (i, 0, 0)),
            pl.BlockSpec((None, w, SWA_KV), lambda i, n: (i, 0, 0)),
        ],
        out_shape=[
            jax.ShapeDtypeStruct((b, t, SWA_Q), BF16),
            jax.ShapeDtypeStruct((b, w, SWA_KV), F32),
            jax.ShapeDtypeStruct((b, w, SWA_KV), F32),
        ],
        scratch_shapes=[pltpu.VMEM((w, SWA_KV), F32), pltpu.VMEM((w, SWA_KV), F32)],
        compiler_params=_cparams(("arbitrary", "arbitrary")),
        name="swa_prompt",
    )(sink, z3, z3, z3, *tables)


SAMPLE_T = 4
PAIR_ROWS = 2 * SAMPLE_T
KPAD = 2 * WINDOW


def _swa_sample_kernel(q_ref, k_ref, v_ref, ck_ref, cv_ref, c_ref, slo_ref, shi_ref, sink_ref,
                       o_ref, nk_ref, nv_ref, *, pairs):
    w = WINDOW
    kvw = SWA_KV
    c, s_lo, s_hi = c_ref[...], slo_ref[...], shi_ref[...]
    nrow = N_HEADS * PAIR_ROWS
    row = lax.broadcasted_iota(jnp.int32, (nrow, KPAD), 0)
    col = lax.broadcasted_iota(jnp.int32, (nrow, KPAD), 1)
    t_shift = SAMPLE_T.bit_length() - 1
    sq = row & (SAMPLE_T - 1)
    par = (row >> t_shift) & 1
    jj = col - w
    mask = ((col < w) & (col > sq)) | (
        (jj >= 0) & (jj < PAIR_ROWS) & ((jj >> t_shift) == par) & ((jj & (SAMPLE_T - 1)) <= sq))
    lane = lax.broadcasted_iota(jnp.int32, (PAIR_ROWS, kvw), 1) >> (HEAD_DIM.bit_length() - 1)
    row8 = lax.broadcasted_iota(jnp.int32, (PAIR_ROWS, kvw), 0)
    sk = sink_ref[...][:, :1]
    zpad = jnp.zeros((KPAD - w - PAIR_ROWS, kvw), F32)

    for p in range(pairs):
        rs = slice(p * PAIR_ROWS, (p + 1) * PAIR_ROWS)
        q8 = _rope(q_ref[rs, :], c, s_lo, s_hi) * (HEAD_DIM ** -0.5)
        k8 = _rope(k_ref[rs, :], c, s_lo, s_hi)
        v8 = v_ref[rs, :]
        blocks = [jnp.where(lane == g, q8[:, h * kvw:(h + 1) * kvw], 0.0)
                  for g in range(N_KV_HEADS) for h in range(GROUP)]
        qall = jnp.concatenate(blocks, axis=0).astype(BF16)

        s_par, v_par = [], []
        for e in range(2):
            bd = 2 * p + e
            kall = jnp.concatenate([ck_ref[bd], k8, zpad], axis=0).astype(BF16)
            v_par.append(jnp.concatenate([cv_ref[bd], v8, zpad], axis=0).astype(BF16))
            s_par.append(_dot_nt(qall, kall))
        s = jnp.where(par == 1, s_par[1], s_par[0])
        s = jnp.where(mask, s, -jnp.inf)
        m = jnp.maximum(jnp.max(s, axis=-1, keepdims=True), sk)
        pr = jnp.exp(s - m)
        denom = jnp.sum(pr, axis=-1, keepdims=True) + jnp.exp(sk - m)
        p0 = jnp.where(par == 0, pr, 0.0).astype(BF16)
        p1 = jnp.where(par == 1, pr, 0.0).astype(BF16)
        oall = (_dot(p0, v_par[0]) + _dot(p1, v_par[1])) / denom

        outs = []
        for h in range(GROUP):
            acc = None
            for g in range(N_KV_HEADS):
                hh = g * GROUP + h
                blk = jnp.where(lane == g, oall[hh * PAIR_ROWS:(hh + 1) * PAIR_ROWS], 0.0)
                acc = blk if acc is None else acc + blk
            outs.append(acc)
        o_ref[rs, :] = jnp.concatenate(outs, axis=1).astype(o_ref.dtype)

        for e in range(2):
            bd = 2 * p + e
            for new8, cref, nref in ((k8, ck_ref, nk_ref), (v8, cv_ref, nv_ref)):
                shifted = pltpu.roll(cref[bd], w - SAMPLE_T, 0)
                tail8 = new8 if e == 1 else pltpu.roll(new8, SAMPLE_T, 0)
                tail = jnp.where(row8 >= SAMPLE_T, tail8, shifted[w - PAIR_ROWS:])
                nref[bd] = jnp.concatenate([shifted[:w - PAIR_ROWS], tail], axis=0)


def _swa_sample(zs, cache_k, cache_v, sink_rows, tables, pairs):
    n = zs.shape[0]
    bd, w, kvw = cache_k.shape
    rows = pairs * PAIR_ROWS
    tab_spec = pl.BlockSpec((PAIR_ROWS, LANES), lambda i: (0, 0))
    cache_spec = pl.BlockSpec((2 * pairs, w, kvw), lambda i: (i, 0, 0))
    return pl.pallas_call(
        functools.partial(_swa_sample_kernel, pairs=pairs),
        grid=(n // rows,),
        in_specs=[
            pl.BlockSpec((rows, SWA_Q), lambda i: (i, OFF_QS // SWA_Q)),
            pl.BlockSpec((rows, SWA_KV), lambda i: (i, OFF_KS // SWA_KV)),
            pl.BlockSpec((rows, SWA_KV), lambda i: (i, OFF_VS // SWA_KV)),
            cache_spec, cache_spec,
            tab_spec, tab_spec, tab_spec,
            pl.BlockSpec((N_HEADS * PAIR_ROWS, LANES), lambda i: (0, 0)),
        ],
        out_specs=[
            pl.BlockSpec((rows, SWA_Q), lambda i: (i, 0)),
            cache_spec, cache_spec,
        ],
        out_shape=[
            jax.ShapeDtypeStruct((n, SWA_Q), BF16),
            jax.ShapeDtypeStruct((bd, w, kvw), F32),
            jax.ShapeDtypeStruct((bd, w, kvw), F32),
        ],
        compiler_params=_cparams(("arbitrary",)),
        name="swa_sample",
    )(zs, zs, zs, cache_k, cache_v, *tables, sink_rows)


def _log_decay(ag, wa2, ba):
    x = _dot(ag.astype(BF16), wa2) + ba
    log_sig = jnp.minimum(x, 0.0) - jnp.log(1.0 + jnp.exp(-jnp.abs(x)))
    return log_sig * (1.0 / GLA_GATE_NORM)


def _gla_out(o, gnorm, rg):
    dv = o.shape[-1]
    ms = jnp.broadcast_to(jnp.sum(o * o, axis=-1, keepdims=True), (o.shape[0], LANES)) * (1.0 / dv)
    r = lax.rsqrt(ms + EPS)
    half = rg * 0.5
    gate = half + half * jnp.tanh(half)
    return o * jnp.concatenate([r] * (dv // LANES), axis=1) * gnorm * gate


def _gla_constants(tt):
    nb = tt // GLA_BLOCK
    halves = [tt >> (l + 1) for l in range(nb.bit_length() - 1)]
    step = np.arange(tt)
    blk = step // GLA_BLOCK
    t16 = ((blk[:, None] == blk[None, :]) & (step[None, :] <= step[:, None])).astype(np.float32)
    rows = [(blk[None, :] < np.arange(nb)[:, None])]
    for hs in halves:
        t_b = (np.arange(nb) * GLA_BLOCK) // (2 * hs) * (2 * hs) + hs
        rows.append(step[None, :] < t_b[:, None])
    rows.append(np.ones((GLA_BLOCK, tt), bool))
    sel = np.concatenate(rows, axis=0).astype(np.float32)
    group = [(step[:, None] // (2 * hs)) == (step[None, :] // (2 * hs)) for hs in halves[1:]]
    mlev = np.stack(group).astype(np.float32)
    return (jnp.asarray(t16, BF16), jnp.asarray(sel, BF16), jnp.asarray(t16), jnp.asarray(mlev)), halves


def _gla_prompt_kernel(q_ref, k_ref, v_ref, rg_ref, ag_ref, wa2_ref, ba_ref, gn_ref,
                       t16_ref, sel_ref, mdiag_ref, mlev_ref, o_ref, sout_ref, st_ref, *, halves):
    t = pl.program_id(1)
    nt = pl.num_programs(1)
    tt = q_ref.shape[0]
    nb = tt // GLA_BLOCK
    nl = len(halves)
    dk, dv, width = GLA_DK, GLA_DV, GLA_DK_TOTAL

    @pl.when(t == 0)
    def _():
        st_ref[...] = jnp.zeros_like(st_ref)

    la = _log_decay(ag_ref[...], wa2_ref[...], ba_ref[...])
    hi, lo = _split_bf16(la)
    c = _dot(t16_ref[...], hi) + _dot(t16_ref[...], lo)
    cum = _dot(sel_ref[...], hi) + _dot(sel_ref[...], lo)
    p_start = cum[:nb]
    total = cum[(1 + nl) * nb:(1 + nl) * nb + 1]
    exp_p = jnp.exp(p_start)
    d_last = total - p_start
    after, before = [], []
    for l in range(nl):
        d = p_start - cum[(1 + l) * nb:(2 + l) * nb]
        after.append(jnp.exp(jnp.minimum(d, 0.0)))
        before.append(-d)

    zeros = jnp.zeros((GLA_BLOCK, width), BF16)
    names = ["qs", "kinv", "qt", "kd"] + [f"q{l}" for l in range(nl)] + [f"k{l}" for l in range(nl)]
    parts = {nm: [] for nm in names}
    for i in range(nb):
        rs = slice(i * GLA_BLOCK, (i + 1) * GLA_BLOCK)
        row = slice(i, i + 1)
        c_b = c[rs]
        k_b = k_ref[rs, :].astype(F32)
        qs = q_ref[rs, :].astype(F32) * (dk ** -0.5) * jnp.exp(c_b)
        parts["qs"].append(qs.astype(BF16))
        parts["kinv"].append((k_b * jnp.exp(-c_b)).astype(BF16))
        parts["qt"].append((qs * exp_p[row]).astype(BF16))
        parts["kd"].append((k_b * jnp.exp(d_last[row] - c_b)).astype(BF16))
        for l, hs in enumerate(halves):
            if (i * GLA_BLOCK) % (2 * hs) >= hs:
                parts[f"q{l}"].append((qs * after[l][row]).astype(BF16))
                parts[f"k{l}"].append(zeros)
            else:
                parts[f"q{l}"].append(zeros)
                parts[f"k{l}"].append((k_b * jnp.exp(before[l][row] - c_b)).astype(BF16))
    full = {nm: jnp.concatenate(parts[nm], axis=0) for nm in names}

    gn = gn_ref[...]
    in_block = mdiag_ref[...] != 0.0
    for h in range(GLA_HEADS):
        ks = slice(h * dk, (h + 1) * dk)
        vs = slice(h * dv, (h + 1) * dv)
        a = jnp.where(in_block, _dot_nt(full["qs"][:, ks], full["kinv"][:, ks]), 0.0)
        for l in range(nl):
            x = _dot_nt(full[f"q{l}"][:, ks], full[f"k{l}"][:, ks])
            a = a + (x if l == 0 else jnp.where(mlev_ref[l - 1] != 0.0, x, 0.0))
        v_h = v_ref[:, vs]
        st = st_ref[h]
        o = _dot(a.astype(BF16), v_h) + _dot_nt(full["qt"][:, ks], st.astype(BF16))
        st_ref[h] = jnp.exp(total[:, ks]) * st + _dot_tn(v_h, full["kd"][:, ks])
        o_ref[:, vs] = _gla_out(o, gn, rg_ref[:, vs].astype(F32)).astype(o_ref.dtype)

    @pl.when(t == nt - 1)
    def _():
        for h in range(GLA_HEADS):
            sout_ref[h] = st_ref[h].T


def _gla_prompt(z3, zag3, wa2, ba, gnorm, tt):
    b, t, _ = z3.shape
    h = GLA_HEADS
    consts, halves = _gla_constants(tt)
    const_specs = [pl.BlockSpec(c.shape, lambda i, s, nd=c.ndim: (0,) * nd) for c in consts]
    return pl.pallas_call(
        functools.partial(_gla_prompt_kernel, halves=tuple(halves)),
        grid=(b, t // tt),
        in_specs=[
            pl.BlockSpec((None, tt, GLA_DK_TOTAL), lambda i, s: (i, s, OFF_QG // GLA_DK_TOTAL)),
            pl.BlockSpec((None, tt, GLA_DK_TOTAL), lambda i, s: (i, s, OFF_KG // GLA_DK_TOTAL)),
            pl.BlockSpec((None, tt, GLA_DV_TOTAL), lambda i, s: (i, s, OFF_VG // GLA_DV_TOTAL)),
            pl.BlockSpec((None, tt, GLA_DV_TOTAL), lambda i, s: (i, s, OFF_RG // GLA_DV_TOTAL)),
            pl.BlockSpec((None, tt, LANES), lambda i, s: (i, s, 0)),
            pl.BlockSpec((LANES, GLA_DK_TOTAL), lambda i, s: (0, 0)),
            pl.BlockSpec((1, GLA_DK_TOTAL), lambda i, s: (0, 0)),
            pl.BlockSpec((1, GLA_DV), lambda i, s: (0, 0)),
            *const_specs,
        ],
        out_specs=[
            pl.BlockSpec((None, tt, GLA_DV_TOTAL), lambda i, s: (i, s, 0)),
            pl.BlockSpec((None, h, GLA_DK, GLA_DV), lambda i, s: (i, 0, 0, 0)),
        ],
        out_shape=[
            jax.ShapeDtypeStruct((b, t, GLA_DV_TOTAL), BF16),
            jax.ShapeDtypeStruct((b, h, GLA_DK, GLA_DV), F32),
        ],
        scratch_shapes=[pltpu.VMEM((h, GLA_DV, GLA_DK), F32)],
        compiler_params=_cparams(("arbitrary", "arbitrary")),
        name="gla_prompt",
    )(z3, z3, z3, z3, zag3, wa2, ba, gnorm, *consts)


def _gla_sample_kernel(q_ref, k_ref, v_ref, rg_ref, ag_ref, wa2_ref, ba_ref, gn_ref, s0_ref,
                       o_ref, s1_ref):
    dk, dv = GLA_DK, GLA_DV
    r_k = lax.broadcasted_iota(jnp.int32, (PAIR_ROWS, dk), 0)
    r_v = lax.broadcasted_iota(jnp.int32, (PAIR_ROWS, dv), 0)
    step_k = r_k & (SAMPLE_T - 1)
    step_v = r_v & (SAMPLE_T - 1)
    odd_k = r_k >= SAMPLE_T
    odd_v = r_v >= SAMPLE_T
    la_all = _log_decay(ag_ref[...], wa2_ref[...], ba_ref[...])
    gn = gn_ref[...]
    kpad = jnp.zeros((LANES - PAIR_ROWS, dk), F32)
    vpad = jnp.zeros((LANES - PAIR_ROWS, dv), BF16)
    ones = jnp.ones((LANES, LANES), BF16)

    for h in range(GLA_HEADS):
        la = la_all[:, h * dk:(h + 1) * dk]
        b = la + jnp.where(step_k >= 1, pltpu.roll(la, 1, 0), 0.0)
        b = b + jnp.where(step_k >= 2, pltpu.roll(b, 2, 0), 0.0)
        b_last = jnp.where(odd_k, b[PAIR_ROWS - 1:PAIR_ROWS, :], b[SAMPLE_T - 1:SAMPLE_T, :])
        q = q_ref[:, h * dk:(h + 1) * dk] * (dk ** -0.5)
        k = k_ref[:, h * dk:(h + 1) * dk]
        v = v_ref[:, h * dv:(h + 1) * dv]
        q_i = q * jnp.exp(b)
        k_i = k * jnp.exp(-b)
        k_d = k * jnp.exp(b_last - b)
        o = jnp.sum(q_i * k_i, axis=-1, keepdims=True) * v
        for d in range(1, SAMPLE_T):
            a_d = jnp.sum(q_i * pltpu.roll(k_i, d, 0), axis=-1, keepdims=True)
            o = o + jnp.where(step_v >= d, a_d * pltpu.roll(v, d, 0), 0.0)
        q_b = q_i.astype(BF16)
        vb = jnp.concatenate([v.astype(BF16), vpad], axis=0)
        o_par = []
        for e in range(2):
            s_old = s0_ref[e, h]
            o_par.append(_dot(q_b, s_old.astype(BF16)))
            sel = (r_k >= SAMPLE_T) if e == 1 else (r_k < SAMPLE_T)
            kd_e = jnp.concatenate([jnp.where(sel, k_d, 0.0), kpad], axis=0).astype(BF16)
            la_e = jnp.concatenate([jnp.where(sel, la, 0.0), kpad], axis=0)
            hi, lo = _split_bf16(la_e)
            decay = jnp.exp(_dot_tn(hi, ones) + _dot_tn(lo, ones))
            decay_full = jnp.concatenate([decay] * (dv // LANES), axis=1)
            s1_ref[e, h] = decay_full * s_old + _dot_tn(kd_e, vb)
        o = o + jnp.where(odd_v, o_par[1], o_par[0])
        o_ref[:, h * dv:(h + 1) * dv] = _gla_out(o, gn, rg_ref[:, h * dv:(h + 1) * dv]).astype(o_ref.dtype)


def _gla_sample(zs, zag, wa2, ba, gnorm, state):
    n = zs.shape[0]
    bd, h, dk, dv = state.shape
    state_spec = pl.BlockSpec((2, h, dk, dv), lambda i: (i, 0, 0, 0))
    return pl.pallas_call(
        _gla_sample_kernel,
        grid=(n // PAIR_ROWS,),
        in_specs=[
            pl.BlockSpec((PAIR_ROWS, GLA_DK_TOTAL), lambda i: (i, OFF_QG // GLA_DK_TOTAL)),
            pl.BlockSpec((PAIR_ROWS, GLA_DK_TOTAL), lambda i: (i, OFF_KG // GLA_DK_TOTAL)),
            pl.BlockSpec((PAIR_ROWS, GLA_DV_TOTAL), lambda i: (i, OFF_VG // GLA_DV_TOTAL)),
            pl.BlockSpec((PAIR_ROWS, GLA_DV_TOTAL), lambda i: (i, OFF_RG // GLA_DV_TOTAL)),
            pl.BlockSpec((PAIR_ROWS, LANES), lambda i: (i, 0)),
            pl.BlockSpec((LANES, GLA_DK_TOTAL), lambda i: (0, 0)),
            pl.BlockSpec((1, GLA_DK_TOTAL), lambda i: (0, 0)),
            pl.BlockSpec((1, GLA_DV), lambda i: (0, 0)),
            state_spec,
        ],
        out_specs=[
            pl.BlockSpec((PAIR_ROWS, GLA_DV_TOTAL), lambda i: (i, 0)),
            state_spec,
        ],
        out_shape=[
            jax.ShapeDtypeStruct((n, GLA_DV_TOTAL), BF16),
            jax.ShapeDtypeStruct(state.shape, F32),
        ],
        compiler_params=_cparams(("arbitrary",)),
        name="gla_sample",
    )(zs, zs, zs, zs, zag, wa2, ba, gnorm, state)


def _merge_kernel(osw_ref, og_ref, gs_ref, gg_ref, x_ref, pswa_ref, pgla_ref, wo_ref, n2_ref,
                  x1_ref, h2_ref):
    a = _dot(osw_ref[...], pswa_ref[...])
    b = _dot(og_ref[...], pgla_ref[...])
    y = _sigmoid(gs_ref[...].astype(F32)) * a + _sigmoid(gg_ref[...].astype(F32)) * b
    x1 = x_ref[...] + _dot(y.astype(BF16), wo_ref[...])
    x1_ref[...] = x1
    h2_ref[...] = _rms(x1, n2_ref[...]).astype(BF16)


def _merge(o_swa, o_gla, z, x2d, p_swa, p_gla, w_o, norm2, tm):
    n = x2d.shape[0]
    resident = functools.partial(pl.BlockSpec, pipeline_mode=pl.Buffered(1))
    return pl.pallas_call(
        _merge_kernel,
        grid=(n // tm,),
        in_specs=[
            pl.BlockSpec((tm, SWA_Q), lambda i: (i, 0)),
            pl.BlockSpec((tm, GLA_DV_TOTAL), lambda i: (i, 0)),
            pl.BlockSpec((tm, D_MODEL), lambda i: (i, OFF_GS // D_MODEL)),
            pl.BlockSpec((tm, D_MODEL), lambda i: (i, OFF_GG // D_MODEL)),
            pl.BlockSpec((tm, D_MODEL), lambda i: (i, 0)),
            resident((SWA_Q, D_MODEL), lambda i: (0, 0)),
            resident((GLA_DV_TOTAL, D_MODEL), lambda i: (0, 0)),
            resident((D_MODEL, D_MODEL), lambda i: (0, 0)),
            pl.BlockSpec((1, D_MODEL), lambda i: (0, 0)),
        ],
        out_specs=[
            pl.BlockSpec((tm, D_MODEL), lambda i: (i, 0)),
            pl.BlockSpec((tm, D_MODEL), lambda i: (i, 0)),
        ],
        out_shape=[
            jax.ShapeDtypeStruct((n, D_MODEL), F32),
            jax.ShapeDtypeStruct((n, D_MODEL), BF16),
        ],
        compiler_params=_cparams(("arbitrary",)),
        name="merge",
    )(o_swa, o_gla, z, z, x2d, p_swa, p_gla, w_o, norm2)


def _mlp_kernel(h2_ref, x1_ref, wup_ref, wdn_ref, fn_ref, out_ref, acc_ref):
    f = pl.program_id(1)
    nf = pl.num_programs(1)
    @pl.when(f == 0)
    def _():
        acc_ref[...] = jnp.zeros_like(acc_ref)

    u = _dot(h2_ref[...], wup_ref[...])
    u = jnp.square(jnp.maximum(u, 0.0)).astype(BF16)
    acc_ref[...] += _dot(u, wdn_ref[...])

    @pl.when(f == nf - 1)
    def _():
        out_ref[...] = _rms(x1_ref[...] + acc_ref[...], fn_ref[...])


def _mlp(h2, x1, w_up, w_down, final_norm, tm, tf):
    n = h2.shape[0]
    return pl.pallas_call(
        _mlp_kernel,
        grid=(n // tm, D_FF // tf),
        in_specs=[
            pl.BlockSpec((tm, D_MODEL), lambda i, f: (i, 0)),
            pl.BlockSpec((tm, D_MODEL), lambda i, f: (i, 0)),
            pl.BlockSpec((D_MODEL, tf), lambda i, f: (0, f)),
            pl.BlockSpec((tf, D_MODEL), lambda i, f: (f, 0)),
            pl.BlockSpec((1, D_MODEL), lambda i, f: (0, 0)),
        ],
        out_specs=pl.BlockSpec((tm, D_MODEL), lambda i, f: (i, 0)),
        out_shape=jax.ShapeDtypeStruct((n, D_MODEL), F32),
        scratch_shapes=[pltpu.VMEM((tm, D_MODEL), F32)],
        compiler_params=_cparams(("arbitrary", "arbitrary")),
        name="mlp",
    )(h2, x1, w_up, w_down, final_norm)


def _pick_tile(n, pref):
    t = min(n, pref)
    while n % t:
        t //= 2
    return t


def _reorder_w_in(w_in):
    o = 0
    parts = {}
    for name, width in (("qs", SWA_Q), ("ks", SWA_KV), ("vs", SWA_KV), ("qg", GLA_DK_TOTAL),
                        ("kg", GLA_DK_TOTAL), ("vg", GLA_DV_TOTAL), ("rg", GLA_DV_TOTAL),
                        ("ag", GLA_GATE_RANK), ("gs", D_MODEL), ("gg", D_MODEL)):
        parts[name] = w_in[:, o:o + width]
        o += width
    parts["qs"] = parts["qs"].reshape(-1, N_KV_HEADS, GROUP, HEAD_DIM).transpose(0, 2, 1, 3).reshape(-1, SWA_Q)
    w_main = jnp.concatenate([parts[nm].astype(BF16) for nm in ("gs", "gg", "vg", "rg", "qg", "kg", "qs", "ks", "vs")],
                             axis=1)
    w_ag = jnp.pad(parts["ag"], ((0, 0), (0, LANES - GLA_GATE_RANK))).astype(BF16)
    return w_main, w_ag


def kernel(x_prompt, x_sample, cache_swa_k, cache_swa_v, state_gla, norm1, w_in, w_a2, b_a, sink,
           gla_norm, p_swa, p_gla, w_o, norm2, w_up, w_down, final_norm):
    assert norm1.shape[0] == 1, "single-layer stack"
    bp, tp, d = x_prompt.shape
    bs, ts, _ = x_sample.shape
    assert ts == SAMPLE_T and bs % 2 == 0 and tp % WINDOW == 0
    w_buf = cache_swa_k.shape[2]
    assert w_buf == WINDOW

    w_main, w_ag = _reorder_w_in(w_in[0])
    wa2 = jnp.pad(w_a2[0], ((0, LANES - GLA_GATE_RANK), (0, 0))).astype(BF16)
    ba = b_a[0][None, :]
    n1, n2, fn = norm1[0][None, :], norm2[0][None, :], final_norm[None, :]
    gn = gla_norm[0][None, :]
    pswa = p_swa[0].reshape(N_KV_HEADS, GROUP, HEAD_DIM, d).transpose(1, 0, 2, 3).reshape(SWA_Q, d).astype(BF16)
    pgla, wo = p_gla[0].astype(BF16), w_o[0].astype(BF16)
    wup, wdn = w_up[0].astype(BF16), w_down[0].astype(BF16)
    sink_smem = sink[0][None, :]
    sink_rows = jnp.broadcast_to(jnp.repeat(sink[0], PAIR_ROWS)[:, None], (N_HEADS * PAIR_ROWS, LANES))

    xp = x_prompt.reshape(bp * tp, d)
    xs = x_sample.reshape(bs * ts, d)
    np_, ns = xp.shape[0], xs.shape[0]

    zp, zagp = _in_proj(xp, n1, w_main, w_ag, _pick_tile(np_, 512), Z_COLS // 4, BF16)
    zs, zags = _in_proj(xs, n1, w_main, w_ag, _pick_tile(ns, 512), 512, F32)

    zp3 = zp.reshape(bp, tp, Z_COLS)
    o_swa_p, k_last, v_last = _swa_prompt(zp3, sink_smem, _rope_tables(jnp.arange(tp)))
    pos_s = PAST_LEN + jnp.arange(ts)
    tabs_s = _rope_tables(jnp.concatenate([pos_s, pos_s]))
    pairs = _pick_tile(bs // 2, 4)
    o_swa_s, nk_s, nv_s = _swa_sample(zs, cache_swa_k[0].reshape(bs, w_buf, SWA_KV),
                                      cache_swa_v[0].reshape(bs, w_buf, SWA_KV), sink_rows, tabs_s, pairs)

    o_gla_p, s_p = _gla_prompt(zp3, zagp.reshape(bp, tp, LANES), wa2, ba, gn, _pick_tile(tp, GLA_TILE))
    o_gla_s, s_s = _gla_sample(zs, zags, wa2, ba, gn, state_gla[0])

    x1p, h2p = _merge(o_swa_p.reshape(np_, SWA_Q), o_gla_p.reshape(np_, GLA_DV_TOTAL), zp, xp,
                      pswa, pgla, wo, n2, _pick_tile(np_, 256))
    x1s, h2s = _merge(o_swa_s, o_gla_s, zs, xs, pswa, pgla, wo, n2, _pick_tile(ns, 256))
    yp = _mlp(h2p, x1p, wup, wdn, fn, _pick_tile(np_, 512), 1024)
    ys = _mlp(h2s, x1s, wup, wdn, fn, _pick_tile(ns, 512), 1024)

    kv5 = lambda a, nb: a.reshape(1, nb, w_buf, N_KV_HEADS, HEAD_DIM)
    return (yp.reshape(bp, tp, d), ys.reshape(bs, ts, d),
            kv5(k_last, bp), kv5(v_last, bp), s_p[None],
            kv5(nk_s, bs), kv5(nv_s, bs), s_s[None])
```

```python
import functools

import jax
import jax.numpy as jnp
import numpy as np
from jax import lax
from jax.experimental import pallas as pl
from jax.experimental.pallas import tpu as pltpu

F32 = jnp.float32
BF16 = jnp.bfloat16

D_MODEL = 2048
PAST_LEN = 8192
N_HEADS = 16
N_KV_HEADS = 4
GROUP = N_HEADS // N_KV_HEADS
HEAD_DIM = 64
WINDOW = 128
ROT_DIM = HEAD_DIM // 4
ROPE_THETA = 500000.0
SWA_Q = N_HEADS * HEAD_DIM
SWA_KV = N_KV_HEADS * HEAD_DIM
GLA_HEADS = 4
GLA_DK = 256
GLA_DV = 512
GLA_DK_TOTAL = GLA_HEADS * GLA_DK
GLA_DV_TOTAL = GLA_HEADS * GLA_DV
GLA_GATE_RANK = 16
GLA_GATE_NORM = 16.0
D_FF = 4 * D_MODEL
EPS = 1e-6

LANES = 128
SUBLANES = 8
VMEM_LIMIT = 56 * 1024 * 1024

OFF_GS = 0
OFF_GG = OFF_GS + D_MODEL
OFF_VG = OFF_GG + D_MODEL
OFF_RG = OFF_VG + GLA_DV_TOTAL
OFF_QG = OFF_RG + GLA_DV_TOTAL
OFF_KG = OFF_QG + GLA_DK_TOTAL
OFF_QS = OFF_KG + GLA_DK_TOTAL
OFF_KS = OFF_QS + SWA_Q
OFF_VS = OFF_KS + SWA_KV
Z_COLS = OFF_VS + SWA_KV

GLA_BLOCK = 16
GLA_TILE = 256


def _cparams(sem):
    return pltpu.CompilerParams(dimension_semantics=sem, vmem_limit_bytes=VMEM_LIMIT)


def _rms(x, w):
    return x * lax.rsqrt(jnp.mean(x * x, axis=-1, keepdims=True) + EPS) * w


def _sigmoid(x):
    return 1.0 / (1.0 + jnp.exp(-x))


def _dot(a, b):
    return jnp.dot(a, b, preferred_element_type=F32)


def _dot_nt(a, b):
    return lax.dot_general(a, b, (((1,), (1,)), ((), ())), preferred_element_type=F32)


def _dot_tn(a, b):
    return lax.dot_general(a, b, (((0,), (0,)), ((), ())), preferred_element_type=F32)


def _split_bf16(x):
    hi = x.astype(BF16)
    lo = (x - hi.astype(F32)).astype(BF16)
    return hi, lo


def _in_proj_kernel(x_ref, nw_ref, w_ref, wag_ref, z_ref, zag_ref):
    h = _rms(x_ref[...], nw_ref[...]).astype(BF16)
    z_ref[...] = _dot(h, w_ref[...]).astype(z_ref.dtype)

    @pl.when(pl.program_id(0) == 0)
    def _():
        zag_ref[...] = _dot(h, wag_ref[...])


def _in_proj(x2d, norm1, w_main, w_ag, tm, tn, z_dtype):
    n = x2d.shape[0]
    nrow = n // tm
    return pl.pallas_call(
        _in_proj_kernel,
        grid=(Z_COLS // tn, nrow),
        in_specs=[
            pl.BlockSpec((tm, D_MODEL), lambda j, i: (i, 0)),
            pl.BlockSpec((1, D_MODEL), lambda j, i: (0, 0)),
            pl.BlockSpec((D_MODEL, tn), lambda j, i: (0, j), pipeline_mode=pl.Buffered(1)),
            pl.BlockSpec((D_MODEL, LANES), lambda j, i: (0, 0)),
        ],
        out_specs=[
            pl.BlockSpec((tm, tn), lambda j, i: (i, j)),
            pl.BlockSpec((tm, LANES), lambda j, i: (jnp.where(j == 0, i, nrow - 1), 0)),
        ],
        out_shape=[
            jax.ShapeDtypeStruct((n, Z_COLS), z_dtype),
            jax.ShapeDtypeStruct((n, LANES), F32),
        ],
        compiler_params=_cparams(("arbitrary", "arbitrary")),
        name="in_proj",
    )(x2d, norm1, w_main, w_ag)


def _rope_tables(pos):
    half = ROT_DIM // 2
    inv = ROPE_THETA ** (-jnp.arange(half, dtype=F32) * 2.0 / ROT_DIM)
    ang = pos.astype(F32)[:, None] * inv[None, :]
    cos, sin = jnp.cos(ang), jnp.sin(ang)
    t = pos.shape[0]
    ones = jnp.ones((t, HEAD_DIM - ROT_DIM), F32)
    zeros = jnp.zeros((t, HEAD_DIM - ROT_DIM), F32)
    zh = jnp.zeros((t, half), F32)
    c = jnp.concatenate([cos, cos, ones], axis=1)
    s_lo = jnp.concatenate([zh, sin, zeros], axis=1)
    s_hi = jnp.concatenate([-sin, zh, zeros], axis=1)
    rep = LANES // HEAD_DIM
    return tuple(jnp.tile(a, (1, rep)) for a in (c, s_lo, s_hi))


def _rope(x, c, s_lo, s_hi):
    half = ROT_DIM // 2
    outs = []
    for j in range(x.shape[1] // LANES):
        xc = x[:, j * LANES:(j + 1) * LANES]
        outs.append(xc * c + pltpu.roll(xc, half, 1) * s_lo + pltpu.roll(xc, LANES - half, 1) * s_hi)
    return outs[0] if len(outs) == 1 else jnp.concatenate(outs, axis=1)


def _swa_prompt_kernel(sink_ref, q_ref, k_ref, v_ref, c_ref, slo_ref, shi_ref,
                       o_ref, klast_ref, vlast_ref, kprev_ref, vprev_ref):
    n = pl.program_id(1)
    nb = pl.num_programs(1)
    w = WINDOW

    @pl.when(n == 0)
    def _():
        kprev_ref[...] = jnp.zeros_like(kprev_ref)
        vprev_ref[...] = jnp.zeros_like(vprev_ref)

    kvw = SWA_KV
    c, s_lo, s_hi = c_ref[...], slo_ref[...], shi_ref[...]
    q = (_rope(q_ref[...].astype(F32), c, s_lo, s_hi) * (HEAD_DIM ** -0.5)).astype(BF16)
    k = _rope(k_ref[...].astype(F32), c, s_lo, s_hi)
    v = v_ref[...].astype(F32)
    kc = jnp.concatenate([kprev_ref[...], k], axis=0).astype(BF16)
    vc = jnp.concatenate([vprev_ref[...], v], axis=0).astype(BF16)

    rows = lax.broadcasted_iota(jnp.int32, (w, 2 * w), 0)
    cols = lax.broadcasted_iota(jnp.int32, (w, 2 * w), 1)
    diff = rows + w - cols
    bias = jnp.where((diff >= 0) & (diff < w) & ((cols >= w) | (n > 0)), 0.0, -jnp.inf)
    bias = jnp.concatenate([bias] * GROUP, axis=0)
    head_shift = HEAD_DIM.bit_length() - 1
    lane_head_q = lax.broadcasted_iota(jnp.int32, (w, kvw), 1) >> head_shift
    lane_head = lax.broadcasted_iota(jnp.int32, (2 * w, kvw), 1) >> head_shift

    for g in range(N_KV_HEADS):
        in_g_q = jnp.where(lane_head_q == g, 1.0, 0.0).astype(BF16)
        in_g = jnp.where(lane_head == g, 1.0, 0.0).astype(BF16)
        qg = jnp.concatenate([q[:, h * kvw:(h + 1) * kvw] * in_g_q for h in range(GROUP)], axis=0)
        sk = jnp.concatenate([jnp.full((w, LANES), sink_ref[0, g * GROUP + h], F32) for h in range(GROUP)], axis=0)
        sk2 = jnp.concatenate([sk, sk], axis=1)
        s = _dot_nt(qg, kc) + bias
        m = jnp.broadcast_to(jnp.max(jnp.maximum(s, sk2), axis=-1, keepdims=True), sk.shape)
        p = jnp.exp(s - jnp.concatenate([m, m], axis=1)).astype(BF16)
        vg = vc * in_g + (1.0 - in_g)
        oa = _dot(p, vg)
        p_sink = jnp.exp(sk - m)
        denom = pltpu.roll(oa, 2 * HEAD_DIM, 1) + jnp.concatenate([p_sink, p_sink], axis=1)
        o = (oa / denom).astype(o_ref.dtype)
        for h in range(GROUP):
            lo = g * HEAD_DIM
            o_ref[:, h * kvw + lo:h * kvw + lo + HEAD_DIM] = o[h * w:(h + 1) * w, lo:lo + HEAD_DIM]

    kprev_ref[...] = k
    vprev_ref[...] = v

    @pl.when(n == nb - 1)
    def _():
        klast_ref[...] = k
        vlast_ref[...] = v


def _swa_prompt(z3, sink, tables):
    b, t, _ = z3.shape
    w = WINDOW
    tab_spec = pl.BlockSpec((w, LANES), lambda i, n: (n, 0))
    return pl.pallas_call(
        _swa_prompt_kernel,
        grid=(b, t // w),
        in_specs=[
            pl.BlockSpec(memory_space=pltpu.SMEM),
            pl.BlockSpec((None, w, SWA_Q), lambda i, n: (i, n, OFF_QS // SWA_Q)),
            pl.BlockSpec((None, w, SWA_KV), lambda i, n: (i, n, OFF_KS // SWA_KV)),
            pl.BlockSpec((None, w, SWA_KV), lambda i, n: (i, n, OFF_VS // SWA_KV)),
            tab_spec, tab_spec, tab_spec,
        ],
        out_specs=[
            pl.BlockSpec((None, w, SWA_Q), lambda i, n: (i, n, 0)),
            pl.BlockSpec((None, w, SWA_KV), lambda i, n: (i, 0, 0)),
            pl.BlockSpec((None, w, SWA_KV), lambda i, n: (i, 0, 0)),
        ],
        out_shape=[
            jax.ShapeDtypeStruct((b, t, SWA_Q), BF16),
            jax.ShapeDtypeStruct((b, w, SWA_KV), F32),
            jax.ShapeDtypeStruct((b, w, SWA_KV), F32),
        ],
        scratch_shapes=[pltpu.VMEM((w, SWA_KV), F32), pltpu.VMEM((w, SWA_KV), F32)],
        compiler_params=_cparams(("arbitrary", "arbitrary")),
        name="swa_prompt",
    )(sink, z3, z3, z3, *tables)


SAMPLE_T = 4
PAIR_ROWS = 2 * SAMPLE_T
KPAD = 2 * WINDOW


def _swa_sample_kernel(q_ref, k_ref, v_ref, ck_ref, cv_ref, c_ref, slo_ref, shi_ref, sink_ref,
                       o_ref, nk_ref, nv_ref, *, pairs):
    w = WINDOW
    kvw = SWA_KV
    c, s_lo, s_hi = c_ref[...], slo_ref[...], shi_ref[...]
    nrow = N_HEADS * PAIR_ROWS
    row = lax.broadcasted_iota(jnp.int32, (nrow, KPAD), 0)
    col = lax.broadcasted_iota(jnp.int32, (nrow, KPAD), 1)
    t_shift = SAMPLE_T.bit_length() - 1
    sq = row & (SAMPLE_T - 1)
    par = (row >> t_shift) & 1
    jj = col - w
    mask = ((col < w) & (col > sq)) | (
        (jj >= 0) & (jj < PAIR_ROWS) & ((jj >> t_shift) == par) & ((jj & (SAMPLE_T - 1)) <= sq))
    lane = lax.broadcasted_iota(jnp.int32, (PAIR_ROWS, kvw), 1) >> (HEAD_DIM.bit_length() - 1)
    row8 = lax.broadcasted_iota(jnp.int32, (PAIR_ROWS, kvw), 0)
    sk = sink_ref[...][:, :1]
    zpad = jnp.zeros((KPAD - w - PAIR_ROWS, kvw), F32)

    for p in range(pairs):
        rs = slice(p * PAIR_ROWS, (p + 1) * PAIR_ROWS)
        q8 = _rope(q_ref[rs, :], c, s_lo, s_hi) * (HEAD_DIM ** -0.5)
        k8 = _rope(k_ref[rs, :], c, s_lo, s_hi)
        v8 = v_ref[rs, :]
        blocks = [jnp.where(lane == g, q8[:, h * kvw:(h + 1) * kvw], 0.0)
                  for g in range(N_KV_HEADS) for h in range(GROUP)]
        qall = jnp.concatenate(blocks, axis=0).astype(BF16)

        s_par, v_par = [], []
        for e in range(2):
            bd = 2 * p + e
            kall = jnp.concatenate([ck_ref[bd], k8, zpad], axis=0).astype(BF16)
            v_par.append(jnp.concatenate([cv_ref[bd], v8, zpad], axis=0).astype(BF16))
            s_par.append(_dot_nt(qall, kall))
        s = jnp.where(par == 1, s_par[1], s_par[0])
        s = jnp.where(mask, s, -jnp.inf)
        m = jnp.maximum(jnp.max(s, axis=-1, keepdims=True), sk)
        pr = jnp.exp(s - m)
        denom = jnp.sum(pr, axis=-1, keepdims=True) + jnp.exp(sk - m)
        p0 = jnp.where(par == 0, pr, 0.0).astype(BF16)
        p1 = jnp.where(par == 1, pr, 0.0).astype(BF16)
        oall = (_dot(p0, v_par[0]) + _dot(p1, v_par[1])) / denom

        outs = []
        for h in range(GROUP):
            acc = None
            for g in range(N_KV_HEADS):
                hh = g * GROUP + h
                blk = jnp.where(lane == g, oall[hh * PAIR_ROWS:(hh + 1) * PAIR_ROWS], 0.0)
                acc = blk if acc is None else acc + blk
            outs.append(acc)
        o_ref[rs, :] = jnp.concatenate(outs, axis=1).astype(o_ref.dtype)

        for e in range(2):
            bd = 2 * p + e
            for new8, cref, nref in ((k8, ck_ref, nk_ref), (v8, cv_ref, nv_ref)):
                shifted = pltpu.roll(cref[bd], w - SAMPLE_T, 0)
                tail8 = new8 if e == 1 else pltpu.roll(new8, SAMPLE_T, 0)
                tail = jnp.where(row8 >= SAMPLE_T, tail8, shifted[w - PAIR_ROWS:])
                nref[bd] = jnp.concatenate([shifted[:w - PAIR_ROWS], tail], axis=0)


def _swa_sample(zs, cache_k, cache_v, sink_rows, tables, pairs):
    n = zs.shape[0]
    bd, w, kvw = cache_k.shape
    rows = pairs * PAIR_ROWS
    tab_spec = pl.BlockSpec((PAIR_ROWS, LANES), lambda i: (0, 0))
    cache_spec = pl.BlockSpec((2 * pairs, w, kvw), lambda i: (i, 0, 0))
    return pl.pallas_call(
        functools.partial(_swa_sample_kernel, pairs=pairs),
        grid=(n // rows,),
        in_specs=[
            pl.BlockSpec((rows, SWA_Q), lambda i: (i, OFF_QS // SWA_Q)),
            pl.BlockSpec((rows, SWA_KV), lambda i: (i, OFF_KS // SWA_KV)),
            pl.BlockSpec((rows, SWA_KV), lambda i: (i, OFF_VS // SWA_KV)),
            cache_spec, cache_spec,
            tab_spec, tab_spec, tab_spec,
            pl.BlockSpec((N_HEADS * PAIR_ROWS, LANES), lambda i: (0, 0)),
        ],
        out_specs=[
            pl.BlockSpec((rows, SWA_Q), lambda i: (i, 0)),
            cache_spec, cache_spec,
        ],
        out_shape=[
            jax.ShapeDtypeStruct((n, SWA_Q), BF16),
            jax.ShapeDtypeStruct((bd, w, kvw), F32),
            jax.ShapeDtypeStruct((bd, w, kvw), F32),
        ],
        compiler_params=_cparams(("arbitrary",)),
        name="swa_sample",
    )(zs, zs, zs, cache_k, cache_v, *tables, sink_rows)


def _log_decay(ag, wa2, ba):
    x = _dot(ag.astype(BF16), wa2) + ba
    log_sig = jnp.minimum(x, 0.0) - jnp.log(1.0 + jnp.exp(-jnp.abs(x)))
    return log_sig * (1.0 / GLA_GATE_NORM)


def _gla_out(o, gnorm, rg):
    dv = o.shape[-1]
    ms = jnp.broadcast_to(jnp.sum(o * o, axis=-1, keepdims=True), (o.shape[0], LANES)) * (1.0 / dv)
    r = lax.rsqrt(ms + EPS)
    half = rg * 0.5
    gate = half + half * jnp.tanh(half)
    return o * jnp.concatenate([r] * (dv // LANES), axis=1) * gnorm * gate


def _gla_constants(tt):
    nb = tt // GLA_BLOCK
    halves = [tt >> (l + 1) for l in range(nb.bit_length() - 1)]
    step = np.arange(tt)
    blk = step // GLA_BLOCK
    t16 = ((blk[:, None] == blk[None, :]) & (step[None, :] <= step[:, None])).astype(np.float32)
    rows = [(blk[None, :] < np.arange(nb)[:, None])]
    for hs in halves:
        t_b = (np.arange(nb) * GLA_BLOCK) // (2 * hs) * (2 * hs) + hs
        rows.append(step[None, :] < t_b[:, None])
    rows.append(np.ones((GLA_BLOCK, tt), bool))
    sel = np.concatenate(rows, axis=0).astype(np.float32)
    group = [(step[:, None] // (2 * hs)) == (step[None, :] // (2 * hs)) for hs in halves[1:]]
    mlev = np.stack(group).astype(np.float32)
    return (jnp.asarray(t16, BF16), jnp.asarray(sel, BF16), jnp.asarray(t16), jnp.asarray(mlev)), halves


def _gla_prompt_kernel(q_ref, k_ref, v_ref, rg_ref, ag_ref, wa2_ref, ba_ref, gn_ref,
                       t16_ref, sel_ref, mdiag_ref, mlev_ref, o_ref, sout_ref, st_ref, *, halves):
    t = pl.program_id(1)
    nt = pl.num_programs(1)
    tt = q_ref.shape[0]
    nb = tt // GLA_BLOCK
    nl = len(halves)
    dk, dv, width = GLA_DK, GLA_DV, GLA_DK_TOTAL

    @pl.when(t == 0)
    def _():
        st_ref[...] = jnp.zeros_like(st_ref)

    la = _log_decay(ag_ref[...], wa2_ref[...], ba_ref[...])
    hi, lo = _split_bf16(la)
    c = _dot(t16_ref[...], hi) + _dot(t16_ref[...], lo)
    cum = _dot(sel_ref[...], hi) + _dot(sel_ref[...], lo)
    p_start = cum[:nb]
    total = cum[(1 + nl) * nb:(1 + nl) * nb + 1]
    exp_p = jnp.exp(p_start)
    d_last = total - p_start
    after, before = [], []
    for l in range(nl):
        d = p_start - cum[(1 + l) * nb:(2 + l) * nb]
        after.append(jnp.exp(jnp.minimum(d, 0.0)))
        before.append(-d)

    zeros = jnp.zeros((GLA_BLOCK, width), BF16)
    names = ["qs", "kinv", "qt", "kd"] + [f"q{l}" for l in range(nl)] + [f"k{l}" for l in range(nl)]
    parts = {nm: [] for nm in names}
    for i in range(nb):
        rs = slice(i * GLA_BLOCK, (i + 1) * GLA_BLOCK)
        row = slice(i, i + 1)
        c_b = c[rs]
        k_b = k_ref[rs, :].astype(F32)
        qs = q_ref[rs, :].astype(F32) * (dk ** -0.5) * jnp.exp(c_b)
        parts["qs"].append(qs.astype(BF16))
        parts["kinv"].append((k_b * jnp.exp(-c_b)).astype(BF16))
        parts["qt"].append((qs * exp_p[row]).astype(BF16))
        parts["kd"].append((k_b * jnp.exp(d_last[row] - c_b)).astype(BF16))
        for l, hs in enumerate(halves):
            if (i * GLA_BLOCK) % (2 * hs) >= hs:
                parts[f"q{l}"].append((qs * after[l][row]).astype(BF16))
                parts[f"k{l}"].append(zeros)
            else:
                parts[f"q{l}"].append(zeros)
                parts[f"k{l}"].append((k_b * jnp.exp(before[l][row] - c_b)).astype(BF16))
    full = {nm: jnp.concatenate(parts[nm], axis=0) for nm in names}

    gn = gn_ref[...]
    in_block = mdiag_ref[...] != 0.0
    for h in range(GLA_HEADS):
        ks = slice(h * dk, (h + 1) * dk)
        vs = slice(h * dv, (h + 1) * dv)
        a = jnp.where(in_block, _dot_nt(full["qs"][:, ks], full["kinv"][:, ks]), 0.0)
        for l in range(nl):
            x = _dot_nt(full[f"q{l}"][:, ks], full[f"k{l}"][:, ks])
            a = a + (x if l == 0 else jnp.where(mlev_ref[l - 1] != 0.0, x, 0.0))
        v_h = v_ref[:, vs]
        st = st_ref[h]
        o = _dot(a.astype(BF16), v_h) + _dot_nt(full["qt"][:, ks], st.astype(BF16))
        st_ref[h] = jnp.exp(total[:, ks]) * st + _dot_tn(v_h, full["kd"][:, ks])
        o_ref[:, vs] = _gla_out(o, gn, rg_ref[:, vs].astype(F32)).astype(o_ref.dtype)

    @pl.when(t == nt - 1)
    def _():
        for h in range(GLA_HEADS):
            sout_ref[h] = st_ref[h].T


def _gla_prompt(z3, zag3, wa2, ba, gnorm, tt):
    b, t, _ = z3.shape
    h = GLA_HEADS
    consts, halves = _gla_constants(tt)
    const_specs = [pl.BlockSpec(c.shape, lambda i, s, nd=c.ndim: (0,) * nd) for c in consts]
    return pl.pallas_call(
        functools.partial(_gla_prompt_kernel, halves=tuple(halves)),
        grid=(b, t // tt),
        in_specs=[
            pl.BlockSpec((None, tt, GLA_DK_TOTAL), lambda i, s: (i, s, OFF_QG // GLA_DK_TOTAL)),
            pl.BlockSpec((None, tt, GLA_DK_TOTAL), lambda i, s: (i, s, OFF_KG // GLA_DK_TOTAL)),
            pl.BlockSpec((None, tt, GLA_DV_TOTAL), lambda i, s: (i, s, OFF_VG // GLA_DV_TOTAL)),
            pl.BlockSpec((None, tt, GLA_DV_TOTAL), lambda i, s: (i, s, OFF_RG // GLA_DV_TOTAL)),
            pl.BlockSpec((None, tt, LANES), lambda i, s: (i, s, 0)),
            pl.BlockSpec((LANES, GLA_DK_TOTAL), lambda i, s: (0, 0)),
            pl.BlockSpec((1, GLA_DK_TOTAL), lambda i, s: (0, 0)),
            pl.BlockSpec((1, GLA_DV), lambda i, s: (0, 0)),
            *const_specs,
        ],
        out_specs=[
            pl.BlockSpec((None, tt, GLA_DV_TOTAL), lambda i, s: (i, s, 0)),
            pl.BlockSpec((None, h, GLA_DK, GLA_DV), lambda i, s: (i, 0, 0, 0)),
        ],
        out_shape=[
            jax.ShapeDtypeStruct((b, t, GLA_DV_TOTAL), BF16),
            jax.ShapeDtypeStruct((b, h, GLA_DK, GLA_DV), F32),
        ],
        scratch_shapes=[pltpu.VMEM((h, GLA_DV, GLA_DK), F32)],
        compiler_params=_cparams(("arbitrary", "arbitrary")),
        name="gla_prompt",
    )(z3, z3, z3, z3, zag3, wa2, ba, gnorm, *consts)


def _gla_sample_kernel(q_ref, k_ref, v_ref, rg_ref, ag_ref, wa2_ref, ba_ref, gn_ref, s0_ref,
                       o_ref, s1_ref):
    dk, dv = GLA_DK, GLA_DV
    r_k = lax.broadcasted_iota(jnp.int32, (PAIR_ROWS, dk), 0)
    r_v = lax.broadcasted_iota(jnp.int32, (PAIR_ROWS, dv), 0)
    step_k = r_k & (SAMPLE_T - 1)
    step_v = r_v & (SAMPLE_T - 1)
    odd_k = r_k >= SAMPLE_T
    odd_v = r_v >= SAMPLE_T
    la_all = _log_decay(ag_ref[...], wa2_ref[...], ba_ref[...])
    gn = gn_ref[...]
    kpad = jnp.zeros((LANES - PAIR_ROWS, dk), F32)
    vpad = jnp.zeros((LANES - PAIR_ROWS, dv), BF16)
    ones = jnp.ones((LANES, LANES), BF16)

    for h in range(GLA_HEADS):
        la = la_all[:, h * dk:(h + 1) * dk]
        b = la + jnp.where(step_k >= 1, pltpu.roll(la, 1, 0), 0.0)
        b = b + jnp.where(step_k >= 2, pltpu.roll(b, 2, 0), 0.0)
        b_last = jnp.where(odd_k, b[PAIR_ROWS - 1:PAIR_ROWS, :], b[SAMPLE_T - 1:SAMPLE_T, :])
        q = q_ref[:, h * dk:(h + 1) * dk] * (dk ** -0.5)
        k = k_ref[:, h * dk:(h + 1) * dk]
        v = v_ref[:, h * dv:(h + 1) * dv]
        q_i = q * jnp.exp(b)
        k_i = k * jnp.exp(-b)
        k_d = k * jnp.exp(b_last - b)
        o = jnp.sum(q_i * k_i, axis=-1, keepdims=True) * v
        for d in range(1, SAMPLE_T):
            a_d = jnp.sum(q_i * pltpu.roll(k_i, d, 0), axis=-1, keepdims=True)
            o = o + jnp.where(step_v >= d, a_d * pltpu.roll(v, d, 0), 0.0)
        q_b = q_i.astype(BF16)
        vb = jnp.concatenate([v.astype(BF16), vpad], axis=0)
        o_par = []
        for e in range(2):
            s_old = s0_ref[e, h]
            o_par.append(_dot(q_b, s_old.astype(BF16)))
            sel = (r_k >= SAMPLE_T) if e == 1 else (r_k < SAMPLE_T)
            kd_e = jnp.concatenate([jnp.where(sel, k_d, 0.0), kpad], axis=0).astype(BF16)
            la_e = jnp.concatenate([jnp.where(sel, la, 0.0), kpad], axis=0)
            hi, lo = _split_bf16(la_e)
            decay = jnp.exp(_dot_tn(hi, ones) + _dot_tn(lo, ones))
            decay_full = jnp.concatenate([decay] * (dv // LANES), axis=1)
            s1_ref[e, h] = decay_full * s_old + _dot_tn(kd_e, vb)
        o = o + jnp.where(odd_v, o_par[1], o_par[0])
        o_ref[:, h * dv:(h + 1) * dv] = _gla_out(o, gn, rg_ref[:, h * dv:(h + 1) * dv]).astype(o_ref.dtype)


def _gla_sample(zs, zag, wa2, ba, gnorm, state):
    n = zs.shape[0]
    bd, h, dk, dv = state.shape
    state_spec = pl.BlockSpec((2, h, dk, dv), lambda i: (i, 0, 0, 0))
    return pl.pallas_call(
        _gla_sample_kernel,
        grid=(n // PAIR_ROWS,),
        in_specs=[
            pl.BlockSpec((PAIR_ROWS, GLA_DK_TOTAL), lambda i: (i, OFF_QG // GLA_DK_TOTAL)),
            pl.BlockSpec((PAIR_ROWS, GLA_DK_TOTAL), lambda i: (i, OFF_KG // GLA_DK_TOTAL)),
            pl.BlockSpec((PAIR_ROWS, GLA_DV_TOTAL), lambda i: (i, OFF_VG // GLA_DV_TOTAL)),
            pl.BlockSpec((PAIR_ROWS, GLA_DV_TOTAL), lambda i: (i, OFF_RG // GLA_DV_TOTAL)),
            pl.BlockSpec((PAIR_ROWS, LANES), lambda i: (i, 0)),
            pl.BlockSpec((LANES, GLA_DK_TOTAL), lambda i: (0, 0)),
            pl.BlockSpec((1, GLA_DK_TOTAL), lambda i: (0, 0)),
            pl.BlockSpec((1, GLA_DV), lambda i: (0, 0)),
            state_spec,
        ],
        out_specs=[
            pl.BlockSpec((PAIR_ROWS, GLA_DV_TOTAL), lambda i: (i, 0)),
            state_spec,
        ],
        out_shape=[
            jax.ShapeDtypeStruct((n, GLA_DV_TOTAL), BF16),
            jax.ShapeDtypeStruct(state.shape, F32),
        ],
        compiler_params=_cparams(("arbitrary",)),
        name="gla_sample",
    )(zs, zs, zs, zs, zag, wa2, ba, gnorm, state)


def _merge_kernel(osw_ref, og_ref, gs_ref, gg_ref, x_ref, pswa_ref, pgla_ref, wo_ref, n2_ref,
                  x1_ref, h2_ref):
    a = _dot(osw_ref[...], pswa_ref[...])
    b = _dot(og_ref[...], pgla_ref[...])
    y = _sigmoid(gs_ref[...].astype(F32)) * a + _sigmoid(gg_ref[...].astype(F32)) * b
    x1 = x_ref[...] + _dot(y.astype(BF16), wo_ref[...])
    x1_ref[...] = x1
    h2_ref[...] = _rms(x1, n2_ref[...]).astype(BF16)


def _merge(o_swa, o_gla, z, x2d, p_swa, p_gla, w_o, norm2, tm):
    n = x2d.shape[0]
    resident = functools.partial(pl.BlockSpec, pipeline_mode=pl.Buffered(1))
    return pl.pallas_call(
        _merge_kernel,
        grid=(n // tm,),
        in_specs=[
            pl.BlockSpec((tm, SWA_Q), lambda i: (i, 0)),
            pl.BlockSpec((tm, GLA_DV_TOTAL), lambda i: (i, 0)),
            pl.BlockSpec((tm, D_MODEL), lambda i: (i, OFF_GS // D_MODEL)),
            pl.BlockSpec((tm, D_MODEL), lambda i: (i, OFF_GG // D_MODEL)),
            pl.BlockSpec((tm, D_MODEL), lambda i: (i, 0)),
            resident((SWA_Q, D_MODEL), lambda i: (0, 0)),
            resident((GLA_DV_TOTAL, D_MODEL), lambda i: (0, 0)),
            resident((D_MODEL, D_MODEL), lambda i: (0, 0)),
            pl.BlockSpec((1, D_MODEL), lambda i: (0, 0)),
        ],
        out_specs=[
            pl.BlockSpec((tm, D_MODEL), lambda i: (i, 0)),
            pl.BlockSpec((tm, D_MODEL), lambda i: (i, 0)),
        ],
        out_shape=[
            jax.ShapeDtypeStruct((n, D_MODEL), F32),
            jax.ShapeDtypeStruct((n, D_MODEL), BF16),
        ],
        compiler_params=_cparams(("arbitrary",)),
        name="merge",
    )(o_swa, o_gla, z, z, x2d, p_swa, p_gla, w_o, norm2)


def _mlp_kernel(h2_ref, x1_ref, wup_ref, wdn_ref, fn_ref, out_ref, acc_ref):
    f = pl.program_id(1)
    nf = pl.num_programs(1)
    @pl.when(f == 0)
    def _():
        acc_ref[...] = jnp.zeros_like(acc_ref)

    u = _dot(h2_ref[...], wup_ref[...])
    u = jnp.square(jnp.maximum(u, 0.0)).astype(BF16)
    acc_ref[...] += _dot(u, wdn_ref[...])

    @pl.when(f == nf - 1)
    def _():
        out_ref[...] = _rms(x1_ref[...] + acc_ref[...], fn_ref[...])


def _mlp(h2, x1, w_up, w_down, final_norm, tm, tf):
    n = h2.shape[0]
    return pl.pallas_call(
        _mlp_kernel,
        grid=(n // tm, D_FF // tf),
        in_specs=[
            pl.BlockSpec((tm, D_MODEL), lambda i, f: (i, 0)),
            pl.BlockSpec((tm, D_MODEL), lambda i, f: (i, 0)),
            pl.BlockSpec((D_MODEL, tf), lambda i, f: (0, f)),
            pl.BlockSpec((tf, D_MODEL), lambda i, f: (f, 0)),
            pl.BlockSpec((1, D_MODEL), lambda i, f: (0, 0)),
        ],
        out_specs=pl.BlockSpec((tm, D_MODEL), lambda i, f: (i, 0)),
        out_shape=jax.ShapeDtypeStruct((n, D_MODEL), F32),
        scratch_shapes=[pltpu.VMEM((tm, D_MODEL), F32)],
        compiler_params=_cparams(("arbitrary", "arbitrary")),
        name="mlp",
    )(h2, x1, w_up, w_down, final_norm)


def _pick_tile(n, pref):
    t = min(n, pref)
    while n % t:
        t //= 2
    return t


W_IN_SPLITS = (("qs", SWA_Q), ("ks", SWA_KV), ("vs", SWA_KV), ("qg", GLA_DK_TOTAL), ("kg", GLA_DK_TOTAL),
               ("vg", GLA_DV_TOTAL), ("rg", GLA_DV_TOTAL), ("ag", GLA_GATE_RANK), ("gs", D_MODEL), ("gg", D_MODEL))
W_IN_COLS = sum(width for _, width in W_IN_SPLITS)


def _reorder_w_in_kernel(w_ref, wm_ref, wag_ref):
    src = {}
    o = 0
    for name, width in W_IN_SPLITS:
        src[name] = o
        o += width
    cast = lambda lo, width: w_ref[:, lo:lo + width].astype(BF16)
    for name, dst in (("gs", OFF_GS), ("gg", OFF_GG), ("vg", OFF_VG), ("rg", OFF_RG), ("qg", OFF_QG),
                      ("kg", OFF_KG), ("ks", OFF_KS), ("vs", OFF_VS)):
        width = dict(W_IN_SPLITS)[name]
        wm_ref[:, dst:dst + width] = cast(src[name], width)
    for h in range(GROUP):
        slab = [cast(src["qs"] + (g * GROUP + h) * HEAD_DIM, HEAD_DIM) for g in range(N_KV_HEADS)]
        wm_ref[:, OFF_QS + h * SWA_KV:OFF_QS + (h + 1) * SWA_KV] = jnp.concatenate(slab, axis=1)
    ag = cast(src["ag"], GLA_GATE_RANK)
    wag_ref[...] = jnp.concatenate([ag, jnp.zeros((ag.shape[0], LANES - GLA_GATE_RANK), BF16)], axis=1)


def _reorder_w_in(w_in, tk=256):
    d = w_in.shape[0]
    return pl.pallas_call(
        _reorder_w_in_kernel,
        grid=(d // tk,),
        in_specs=[pl.BlockSpec((tk, W_IN_COLS), lambda i: (i, 0))],
        out_specs=[pl.BlockSpec((tk, Z_COLS), lambda i: (i, 0)), pl.BlockSpec((tk, LANES), lambda i: (i, 0))],
        out_shape=[jax.ShapeDtypeStruct((d, Z_COLS), BF16), jax.ShapeDtypeStruct((d, LANES), BF16)],
        compiler_params=_cparams(("arbitrary",)),
        name="reorder_w_in",
    )(w_in)


def kernel(x_prompt, x_sample, cache_swa_k, cache_swa_v, state_gla, norm1, w_in, w_a2, b_a, sink,
           gla_norm, p_swa, p_gla, w_o, norm2, w_up, w_down, final_norm):
    assert norm1.shape[0] == 1, "single-layer stack"
    bp, tp, d = x_prompt.shape
    bs, ts, _ = x_sample.shape
    assert ts == SAMPLE_T and bs % 2 == 0 and tp % WINDOW == 0
    w_buf = cache_swa_k.shape[2]
    assert w_buf == WINDOW

    w_main, w_ag = _reorder_w_in(w_in[0])
    wa2 = jnp.pad(w_a2[0], ((0, LANES - GLA_GATE_RANK), (0, 0))).astype(BF16)
    ba = b_a[0][None, :]
    n1, n2, fn = norm1[0][None, :], norm2[0][None, :], final_norm[None, :]
    gn = gla_norm[0][None, :]
    pswa = p_swa[0].reshape(N_KV_HEADS, GROUP, HEAD_DIM, d).transpose(1, 0, 2, 3).reshape(SWA_Q, d).astype(BF16)
    pgla, wo = p_gla[0].astype(BF16), w_o[0].astype(BF16)
    wup, wdn = w_up[0].astype(BF16), w_down[0].astype(BF16)
    sink_smem = sink[0][None, :]
    sink_rows = jnp.broadcast_to(jnp.repeat(sink[0], PAIR_ROWS)[:, None], (N_HEADS * PAIR_ROWS, LANES))

    xp = x_prompt.reshape(bp * tp, d)
    xs = x_sample.reshape(bs * ts, d)
    np_, ns = xp.shape[0], xs.shape[0]

    zp, zagp = _in_proj(xp, n1, w_main, w_ag, _pick_tile(np_, 512), Z_COLS // 2, BF16)
    zs, zags = _in_proj(xs, n1, w_main, w_ag, _pick_tile(ns, 512), Z_COLS // 4, F32)

    zp3 = zp.reshape(bp, tp, Z_COLS)
    o_swa_p, k_last, v_last = _swa_prompt(zp3, sink_smem, _rope_tables(jnp.arange(tp)))
    pos_s = PAST_LEN + jnp.arange(ts)
    tabs_s = _rope_tables(jnp.concatenate([pos_s, pos_s]))
    pairs = _pick_tile(bs // 2, 4)
    o_swa_s, nk_s, nv_s = _swa_sample(zs, cache_swa_k[0].reshape(bs, w_buf, SWA_KV),
                                      cache_swa_v[0].reshape(bs, w_buf, SWA_KV), sink_rows, tabs_s, pairs)

    o_gla_p, s_p = _gla_prompt(zp3, zagp.reshape(bp, tp, LANES), wa2, ba, gn, _pick_tile(tp, GLA_TILE))
    o_gla_s, s_s = _gla_sample(zs, zags, wa2, ba, gn, state_gla[0])

    x1p, h2p = _merge(o_swa_p.reshape(np_, SWA_Q), o_gla_p.reshape(np_, GLA_DV_TOTAL), zp, xp,
                      pswa, pgla, wo, n2, _pick_tile(np_, 256))
    x1s, h2s = _merge(o_swa_s, o_gla_s, zs, xs, pswa, pgla, wo, n2, _pick_tile(ns, 256))
    yp = _mlp(h2p, x1p, wup, wdn, fn, _pick_tile(np_, 512), 1024)
    ys = _mlp(h2s, x1s, wup, wdn, fn, _pick_tile(ns, 512), 1024)

    kv5 = lambda a, nb: a.reshape(1, nb, w_buf, N_KV_HEADS, HEAD_DIM)
    return (yp.reshape(bp, tp, d), ys.reshape(bs, ts, d),
            kv5(k_last, bp), kv5(v_last, bp), s_p[None],
            kv5(nk_s, bs), kv5(nv_s, bs), s_s[None])
```

```python
import functools

import jax
import jax.numpy as jnp
import numpy as np
from jax import lax
from jax.experimental import pallas as pl
from jax.experimental.pallas import tpu as pltpu

F32 = jnp.float32
BF16 = jnp.bfloat16

D_MODEL = 2048
PAST_LEN = 8192
N_HEADS = 16
N_KV_HEADS = 4
GROUP = N_HEADS // N_KV_HEADS
HEAD_DIM = 64
WINDOW = 128
ROT_DIM = HEAD_DIM // 4
ROPE_THETA = 500000.0
SWA_Q = N_HEADS * HEAD_DIM
SWA_KV = N_KV_HEADS * HEAD_DIM
GLA_HEADS = 4
GLA_DK = 256
GLA_DV = 512
GLA_DK_TOTAL = GLA_HEADS * GLA_DK
GLA_DV_TOTAL = GLA_HEADS * GLA_DV
GLA_GATE_RANK = 16
GLA_GATE_NORM = 16.0
D_FF = 4 * D_MODEL
EPS = 1e-6
LOG2E = 1.4426950408889634

LANES = 128
SUBLANES = 8
VMEM_LIMIT = 56 * 1024 * 1024

OFF_GS = 0
OFF_GG = OFF_GS + D_MODEL
OFF_VG = OFF_GG + D_MODEL
OFF_RG = OFF_VG + GLA_DV_TOTAL
OFF_QG = OFF_RG + GLA_DV_TOTAL
OFF_KG = OFF_QG + GLA_DK_TOTAL
OFF_QS = OFF_KG + GLA_DK_TOTAL
OFF_KS = OFF_QS + SWA_Q
OFF_VS = OFF_KS + SWA_KV
Z_COLS = OFF_VS + SWA_KV

GLA_BLOCK = 16
GLA_TILE = 256


def _cparams(sem):
    return pltpu.CompilerParams(dimension_semantics=sem, vmem_limit_bytes=VMEM_LIMIT)


def _rms(x, w):
    return x * lax.rsqrt(jnp.mean(x * x, axis=-1, keepdims=True) + EPS) * w


def _sigmoid(x):
    return 1.0 / (1.0 + jnp.exp(-x))


def _dot(a, b):
    return jnp.dot(a, b, preferred_element_type=F32)


def _dot_nt(a, b):
    return lax.dot_general(a, b, (((1,), (1,)), ((), ())), preferred_element_type=F32)


def _dot_tn(a, b):
    return lax.dot_general(a, b, (((0,), (0,)), ((), ())), preferred_element_type=F32)


def _split_bf16(x):
    hi = x.astype(BF16)
    lo = (x - hi.astype(F32)).astype(BF16)
    return hi, lo


def _in_proj_kernel(x_ref, nw_ref, w_ref, wag_ref, z_ref, zag_ref):
    h = _rms(x_ref[...], nw_ref[...]).astype(BF16)
    z_ref[...] = _dot(h, w_ref[...]).astype(z_ref.dtype)

    @pl.when(pl.program_id(0) == 0)
    def _():
        zag_ref[...] = _dot(h, wag_ref[...])


def _in_proj(x2d, norm1, w_main, w_ag, tm, tn, z_dtype, resident_w):
    n = x2d.shape[0]
    nrow = n // tm
    w_mode = {"pipeline_mode": pl.Buffered(1)} if resident_w else {}
    return pl.pallas_call(
        _in_proj_kernel,
        grid=(Z_COLS // tn, nrow),
        in_specs=[
            pl.BlockSpec((tm, D_MODEL), lambda j, i: (i, 0)),
            pl.BlockSpec((1, D_MODEL), lambda j, i: (0, 0)),
            pl.BlockSpec((D_MODEL, tn), lambda j, i: (0, j), **w_mode),
            pl.BlockSpec((D_MODEL, LANES), lambda j, i: (0, 0)),
        ],
        out_specs=[
            pl.BlockSpec((tm, tn), lambda j, i: (i, j)),
            pl.BlockSpec((tm, LANES), lambda j, i: (jnp.where(j == 0, i, nrow - 1), 0)),
        ],
        out_shape=[
            jax.ShapeDtypeStruct((n, Z_COLS), z_dtype),
            jax.ShapeDtypeStruct((n, LANES), F32),
        ],
        compiler_params=_cparams(("arbitrary", "arbitrary")),
        name="in_proj",
    )(x2d, norm1, w_main, w_ag)


def _rope_tables(pos):
    half = ROT_DIM // 2
    inv = ROPE_THETA ** (-jnp.arange(half, dtype=F32) * 2.0 / ROT_DIM)
    ang = pos.astype(F32)[:, None] * inv[None, :]
    cos, sin = jnp.cos(ang), jnp.sin(ang)
    t = pos.shape[0]
    ones = jnp.ones((t, HEAD_DIM - ROT_DIM), F32)
    zeros = jnp.zeros((t, HEAD_DIM - ROT_DIM), F32)
    zh = jnp.zeros((t, half), F32)
    c = jnp.concatenate([cos, cos, ones], axis=1)
    s_lo = jnp.concatenate([zh, sin, zeros], axis=1)
    s_hi = jnp.concatenate([-sin, zh, zeros], axis=1)
    rep = LANES // HEAD_DIM
    return tuple(jnp.tile(a, (1, rep)) for a in (c, s_lo, s_hi))


def _rope(x, c, s_lo, s_hi):
    half = ROT_DIM // 2
    outs = []
    for j in range(x.shape[1] // LANES):
        xc = x[:, j * LANES:(j + 1) * LANES]
        outs.append(xc * c + pltpu.roll(xc, half, 1) * s_lo + pltpu.roll(xc, LANES - half, 1) * s_hi)
    return outs[0] if len(outs) == 1 else jnp.concatenate(outs, axis=1)


def _swa_prompt_kernel(sink_ref, q_ref, k_ref, v_ref, c_ref, slo_ref, shi_ref,
                       o_ref, klast_ref, vlast_ref, kprev_ref, vprev_ref):
    n = pl.program_id(1)
    nb = pl.num_programs(1)
    w = WINDOW

    @pl.when(n == 0)
    def _():
        kprev_ref[...] = jnp.zeros_like(kprev_ref)
        vprev_ref[...] = jnp.zeros_like(vprev_ref)

    kvw = SWA_KV
    c, s_lo, s_hi = c_ref[...], slo_ref[...], shi_ref[...]
    q = (_rope(q_ref[...].astype(F32), c, s_lo, s_hi) * (HEAD_DIM ** -0.5 * LOG2E)).astype(BF16)
    k = _rope(k_ref[...].astype(F32), c, s_lo, s_hi)
    v = v_ref[...].astype(F32)
    kc = jnp.concatenate([kprev_ref[...], k], axis=0).astype(BF16)
    vc = jnp.concatenate([vprev_ref[...], v], axis=0).astype(BF16)

    rows = lax.broadcasted_iota(jnp.int32, (w, 2 * w), 0)
    cols = lax.broadcasted_iota(jnp.int32, (w, 2 * w), 1)
    diff = rows + w - cols
    bias = jnp.where((diff >= 0) & (diff < w) & ((cols >= w) | (n > 0)), 0.0, -jnp.inf)
    bias = jnp.concatenate([bias] * GROUP, axis=0)
    head_shift = HEAD_DIM.bit_length() - 1
    lane_head_q = lax.broadcasted_iota(jnp.int32, (w, kvw), 1) >> head_shift
    lane_head = lax.broadcasted_iota(jnp.int32, (2 * w, kvw), 1) >> head_shift

    for g in range(N_KV_HEADS):
        in_g_q = jnp.where(lane_head_q == g, 1.0, 0.0).astype(BF16)
        in_g = jnp.where(lane_head == g, 1.0, 0.0).astype(BF16)
        qg = jnp.concatenate([q[:, h * kvw:(h + 1) * kvw] * in_g_q for h in range(GROUP)], axis=0)
        sk = jnp.concatenate([jnp.full((w, LANES), sink_ref[0, g * GROUP + h] * LOG2E, F32)
                              for h in range(GROUP)], axis=0)
        s = _dot_nt(qg, kc) + bias
        m = jnp.maximum(jnp.broadcast_to(jnp.max(s, axis=-1, keepdims=True), sk.shape), sk)
        p = jnp.exp2(s - jnp.concatenate([m, m], axis=1)).astype(BF16)
        vg = vc * in_g + (1.0 - in_g)
        oa = _dot(p, vg)
        p_sink = jnp.exp2(sk - m)
        denom = pltpu.roll(oa, 2 * HEAD_DIM, 1) + jnp.concatenate([p_sink, p_sink], axis=1)
        o = (oa / denom).astype(o_ref.dtype)
        for h in range(GROUP):
            lo = g * HEAD_DIM
            o_ref[:, h * kvw + lo:h * kvw + lo + HEAD_DIM] = o[h * w:(h + 1) * w, lo:lo + HEAD_DIM]

    kprev_ref[...] = k
    vprev_ref[...] = v

    @pl.when(n == nb - 1)
    def _():
        klast_ref[...] = k
        vlast_ref[...] = v


def _swa_prompt(z3, sink, tables):
    b, t, _ = z3.shape
    w = WINDOW
    tab_spec = pl.BlockSpec((w, LANES), lambda i, n: (n, 0))
    return pl.pallas_call(
        _swa_prompt_kernel,
        grid=(b, t // w),
        in_specs=[
            pl.BlockSpec(memory_space=pltpu.SMEM),
            pl.BlockSpec((None, w, SWA_Q), lambda i, n: (i, n, OFF_QS // SWA_Q)),
            pl.BlockSpec((None, w, SWA_KV), lambda i, n: (i, n, OFF_KS // SWA_KV)),
            pl.BlockSpec((None, w, SWA_KV), lambda i, n: (i, n, OFF_VS // SWA_KV)),
            tab_spec, tab_spec, tab_spec,
        ],
        out_specs=[
            pl.BlockSpec((None, w, SWA_Q), lambda i, n: (i, n, 0)),
            pl.BlockSpec((None, w, SWA_KV), lambda i, n: (i, 0, 0)),
            pl.BlockSpec((None, w, SWA_KV), lambda i, n: (i, 0, 0)),
        ],
        out_shape=[
            jax.ShapeDtypeStruct((b, t, SWA_Q), BF16),
            jax.ShapeDtypeStruct((b, w, SWA_KV), F32),
            jax.ShapeDtypeStruct((b, w, SWA_KV), F32),
        ],
        scratch_shapes=[pltpu.VMEM((w, SWA_KV), F32), pltpu.VMEM((w, SWA_KV), F32)],
        compiler_params=_cparams(("arbitrary", "arbitrary")),
        name="swa_prompt",
    )(sink, z3, z3, z3, *tables)


SAMPLE_T = 4
PAIR_ROWS = 2 * SAMPLE_T
KPAD = 2 * WINDOW


def _swa_sample_kernel(q_ref, k_ref, v_ref, ck_ref, cv_ref, c_ref, slo_ref, shi_ref, sink_ref,
                       o_ref, nk_ref, nv_ref, *, pairs):
    w = WINDOW
    kvw = SWA_KV
    c, s_lo, s_hi = c_ref[...], slo_ref[...], shi_ref[...]
    nrow = N_HEADS * PAIR_ROWS
    row = lax.broadcasted_iota(jnp.int32, (nrow, KPAD), 0)
    col = lax.broadcasted_iota(jnp.int32, (nrow, KPAD), 1)
    t_shift = SAMPLE_T.bit_length() - 1
    sq = row & (SAMPLE_T - 1)
    par = (row >> t_shift) & 1
    jj = col - w
    mask = ((col < w) & (col > sq)) | (
        (jj >= 0) & (jj < PAIR_ROWS) & ((jj >> t_shift) == par) & ((jj & (SAMPLE_T - 1)) <= sq))
    lane = lax.broadcasted_iota(jnp.int32, (PAIR_ROWS, kvw), 1) >> (HEAD_DIM.bit_length() - 1)
    row8 = lax.broadcasted_iota(jnp.int32, (PAIR_ROWS, kvw), 0)
    sk = sink_ref[...][:, :1]
    zpad = jnp.zeros((KPAD - w - PAIR_ROWS, kvw), F32)

    for p in range(pairs):
        rs = slice(p * PAIR_ROWS, (p + 1) * PAIR_ROWS)
        q8 = _rope(q_ref[rs, :], c, s_lo, s_hi) * (HEAD_DIM ** -0.5)
        k8 = _rope(k_ref[rs, :], c, s_lo, s_hi)
        v8 = v_ref[rs, :]
        blocks = [jnp.where(lane == g, q8[:, h * kvw:(h + 1) * kvw], 0.0)
                  for g in range(N_KV_HEADS) for h in range(GROUP)]
        qall = jnp.concatenate(blocks, axis=0).astype(BF16)

        s_par, v_par = [], []
        for e in range(2):
            bd = 2 * p + e
            kall = jnp.concatenate([ck_ref[bd], k8, zpad], axis=0).astype(BF16)
            v_par.append(jnp.concatenate([cv_ref[bd], v8, zpad], axis=0).astype(BF16))
            s_par.append(_dot_nt(qall, kall))
        s = jnp.where(par == 1, s_par[1], s_par[0])
        s = jnp.where(mask, s, -jnp.inf)
        m = jnp.maximum(jnp.max(s, axis=-1, keepdims=True), sk)
        pr = jnp.exp(s - m)
        denom = jnp.sum(pr, axis=-1, keepdims=True) + jnp.exp(sk - m)
        p0 = jnp.where(par == 0, pr, 0.0).astype(BF16)
        p1 = jnp.where(par == 1, pr, 0.0).astype(BF16)
        oall = (_dot(p0, v_par[0]) + _dot(p1, v_par[1])) / denom

        outs = []
        for h in range(GROUP):
            acc = None
            for g in range(N_KV_HEADS):
                hh = g * GROUP + h
                blk = jnp.where(lane == g, oall[hh * PAIR_ROWS:(hh + 1) * PAIR_ROWS], 0.0)
                acc = blk if acc is None else acc + blk
            outs.append(acc)
        o_ref[rs, :] = jnp.concatenate(outs, axis=1).astype(o_ref.dtype)

        for e in range(2):
            bd = 2 * p + e
            for new8, cref, nref in ((k8, ck_ref, nk_ref), (v8, cv_ref, nv_ref)):
                shifted = pltpu.roll(cref[bd], w - SAMPLE_T, 0)
                tail8 = new8 if e == 1 else pltpu.roll(new8, SAMPLE_T, 0)
                tail = jnp.where(row8 >= SAMPLE_T, tail8, shifted[w - PAIR_ROWS:])
                nref[bd] = jnp.concatenate([shifted[:w - PAIR_ROWS], tail], axis=0)


def _swa_sample(zs, cache_k, cache_v, sink_rows, tables, pairs):
    n = zs.shape[0]
    bd, w, kvw = cache_k.shape
    rows = pairs * PAIR_ROWS
    tab_spec = pl.BlockSpec((PAIR_ROWS, LANES), lambda i: (0, 0))
    cache_spec = pl.BlockSpec((2 * pairs, w, kvw), lambda i: (i, 0, 0))
    return pl.pallas_call(
        functools.partial(_swa_sample_kernel, pairs=pairs),
        grid=(n // rows,),
        in_specs=[
            pl.BlockSpec((rows, SWA_Q), lambda i: (i, OFF_QS // SWA_Q)),
            pl.BlockSpec((rows, SWA_KV), lambda i: (i, OFF_KS // SWA_KV)),
            pl.BlockSpec((rows, SWA_KV), lambda i: (i, OFF_VS // SWA_KV)),
            cache_spec, cache_spec,
            tab_spec, tab_spec, tab_spec,
            pl.BlockSpec((N_HEADS * PAIR_ROWS, LANES), lambda i: (0, 0)),
        ],
        out_specs=[
            pl.BlockSpec((rows, SWA_Q), lambda i: (i, 0)),
            cache_spec, cache_spec,
        ],
        out_shape=[
            jax.ShapeDtypeStruct((n, SWA_Q), BF16),
            jax.ShapeDtypeStruct((bd, w, kvw), F32),
            jax.ShapeDtypeStruct((bd, w, kvw), F32),
        ],
        compiler_params=_cparams(("arbitrary",)),
        name="swa_sample",
    )(zs, zs, zs, cache_k, cache_v, *tables, sink_rows)


def _log_decay(ag, wa2, ba):
    x = _dot(ag.astype(BF16), wa2) + ba
    log_sig = jnp.minimum(x, 0.0) - jnp.log(1.0 + jnp.exp(-jnp.abs(x)))
    return log_sig * (1.0 / GLA_GATE_NORM)


def _gla_out(o, gnorm, rg):
    dv = o.shape[-1]
    ms = jnp.broadcast_to(jnp.sum(o * o, axis=-1, keepdims=True), (o.shape[0], LANES)) * (1.0 / dv)
    r = lax.rsqrt(ms + EPS)
    half = rg * 0.5
    gate = half + half * jnp.tanh(half)
    return o * jnp.concatenate([r] * (dv // LANES), axis=1) * gnorm * gate


def _gla_constants(tt):
    nb = tt // GLA_BLOCK
    halves = [tt >> (l + 1) for l in range(nb.bit_length() - 1)]
    step = np.arange(tt)
    blk = step // GLA_BLOCK
    t16 = ((blk[:, None] == blk[None, :]) & (step[None, :] <= step[:, None])).astype(np.float32)
    rows = [(blk[None, :] < np.arange(nb)[:, None])]
    for hs in halves:
        t_b = (np.arange(nb) * GLA_BLOCK) // (2 * hs) * (2 * hs) + hs
        rows.append(step[None, :] < t_b[:, None])
    rows.append(np.ones((GLA_BLOCK, tt), bool))
    sel = np.concatenate(rows, axis=0).astype(np.float32)
    group = [(step[:, None] // (2 * hs)) == (step[None, :] // (2 * hs)) for hs in halves[1:]]
    mlev = np.stack(group).astype(np.float32)
    return (jnp.asarray(t16, BF16), jnp.asarray(sel, BF16), jnp.asarray(t16), jnp.asarray(mlev)), halves


def _gla_prompt_kernel(q_ref, k_ref, v_ref, ag_ref, wa2_ref, ba_ref,
                       t16_ref, sel_ref, mdiag_ref, mlev_ref, o_ref, sout_ref, st_ref, *, halves):
    t = pl.program_id(1)
    nt = pl.num_programs(1)
    tt = q_ref.shape[0]
    nb = tt // GLA_BLOCK
    nl = len(halves)
    dk, dv, width = GLA_DK, GLA_DV, GLA_DK_TOTAL

    @pl.when(t == 0)
    def _():
        st_ref[...] = jnp.zeros_like(st_ref)

    la = _log_decay(ag_ref[...], wa2_ref[...], ba_ref[...]) * LOG2E
    hi, lo = _split_bf16(la)
    c = _dot(t16_ref[...], hi) + _dot(t16_ref[...], lo)
    cum = _dot(sel_ref[...], hi) + _dot(sel_ref[...], lo)
    p_start = cum[:nb]
    total = cum[(1 + nl) * nb:(1 + nl) * nb + 1]
    exp_p = jnp.exp2(p_start)
    d_last = total - p_start
    after, before = [], []
    for l in range(nl):
        d = p_start - cum[(1 + l) * nb:(2 + l) * nb]
        after.append(jnp.exp2(jnp.minimum(d, 0.0)))
        before.append(-d)

    zeros = jnp.zeros((GLA_BLOCK, width), BF16)
    names = ["qs", "kinv", "qt", "kd"] + [f"q{l}" for l in range(nl)] + [f"k{l}" for l in range(nl)]
    parts = {nm: [] for nm in names}
    for i in range(nb):
        rs = slice(i * GLA_BLOCK, (i + 1) * GLA_BLOCK)
        row = slice(i, i + 1)
        c_b = c[rs]
        k_b = k_ref[rs, :].astype(F32)
        qs = q_ref[rs, :].astype(F32) * (dk ** -0.5) * jnp.exp2(c_b)
        parts["qs"].append(qs.astype(BF16))
        parts["kinv"].append((k_b * jnp.exp2(-c_b)).astype(BF16))
        parts["qt"].append((qs * exp_p[row]).astype(BF16))
        parts["kd"].append((k_b * jnp.exp2(d_last[row] - c_b)).astype(BF16))
        for l, hs in enumerate(halves):
            if (i * GLA_BLOCK) % (2 * hs) >= hs:
                parts[f"q{l}"].append((qs * after[l][row]).astype(BF16))
                parts[f"k{l}"].append(zeros)
            else:
                parts[f"q{l}"].append(zeros)
                parts[f"k{l}"].append((k_b * jnp.exp2(before[l][row] - c_b)).astype(BF16))
    full = {nm: jnp.concatenate(parts[nm], axis=0) for nm in names}

    in_block = mdiag_ref[...] != 0.0
    for h in range(GLA_HEADS):
        ks = slice(h * dk, (h + 1) * dk)
        vs = slice(h * dv, (h + 1) * dv)
        a = jnp.where(in_block, _dot_nt(full["qs"][:, ks], full["kinv"][:, ks]), 0.0)
        for l in range(nl):
            x = _dot_nt(full[f"q{l}"][:, ks], full[f"k{l}"][:, ks])
            a = a + (x if l == 0 else jnp.where(mlev_ref[l - 1] != 0.0, x, 0.0))
        v_h = v_ref[:, vs]
        st = st_ref[h]
        o = _dot(a.astype(BF16), v_h) + _dot_nt(full["qt"][:, ks], st.astype(BF16))
        st_ref[h] = jnp.exp2(total[:, ks]) * st + _dot_tn(v_h, full["kd"][:, ks])
        o_ref[:, vs] = o.astype(o_ref.dtype)

    @pl.when(t == nt - 1)
    def _():
        for h in range(GLA_HEADS):
            sout_ref[h] = st_ref[h].T


def _gla_prompt(z3, zag3, wa2, ba, tt):
    b, t, _ = z3.shape
    h = GLA_HEADS
    consts, halves = _gla_constants(tt)
    const_specs = [pl.BlockSpec(c.shape, lambda i, s, nd=c.ndim: (0,) * nd) for c in consts]
    return pl.pallas_call(
        functools.partial(_gla_prompt_kernel, halves=tuple(halves)),
        grid=(b, t // tt),
        in_specs=[
            pl.BlockSpec((None, tt, GLA_DK_TOTAL), lambda i, s: (i, s, OFF_QG // GLA_DK_TOTAL)),
            pl.BlockSpec((None, tt, GLA_DK_TOTAL), lambda i, s: (i, s, OFF_KG // GLA_DK_TOTAL)),
            pl.BlockSpec((None, tt, GLA_DV_TOTAL), lambda i, s: (i, s, OFF_VG // GLA_DV_TOTAL)),
            pl.BlockSpec((None, tt, LANES), lambda i, s: (i, s, 0)),
            pl.BlockSpec((LANES, GLA_DK_TOTAL), lambda i, s: (0, 0)),
            pl.BlockSpec((1, GLA_DK_TOTAL), lambda i, s: (0, 0)),
            *const_specs,
        ],
        out_specs=[
            pl.BlockSpec((None, tt, GLA_DV_TOTAL), lambda i, s: (i, s, 0)),
            pl.BlockSpec((None, h, GLA_DK, GLA_DV), lambda i, s: (i, 0, 0, 0)),
        ],
        out_shape=[
            jax.ShapeDtypeStruct((b, t, GLA_DV_TOTAL), BF16),
            jax.ShapeDtypeStruct((b, h, GLA_DK, GLA_DV), F32),
        ],
        scratch_shapes=[pltpu.VMEM((h, GLA_DV, GLA_DK), F32)],
        compiler_params=_cparams(("arbitrary", "arbitrary")),
        name="gla_prompt",
    )(z3, z3, z3, zag3, wa2, ba, *consts)


def _gla_sample_kernel(q_ref, k_ref, v_ref, ag_ref, wa2_ref, ba_ref, s0_ref, o_ref, s1_ref):
    dk, dv = GLA_DK, GLA_DV
    r_k = lax.broadcasted_iota(jnp.int32, (PAIR_ROWS, dk), 0)
    r_v = lax.broadcasted_iota(jnp.int32, (PAIR_ROWS, dv), 0)
    step_k = r_k & (SAMPLE_T - 1)
    step_v = r_v & (SAMPLE_T - 1)
    odd_k = r_k >= SAMPLE_T
    odd_v = r_v >= SAMPLE_T
    la_all = _log_decay(ag_ref[...], wa2_ref[...], ba_ref[...])
    kpad = jnp.zeros((LANES - PAIR_ROWS, dk), F32)
    vpad = jnp.zeros((LANES - PAIR_ROWS, dv), BF16)
    ones = jnp.ones((LANES, LANES), BF16)

    for h in range(GLA_HEADS):
        la = la_all[:, h * dk:(h + 1) * dk]
        b = la + jnp.where(step_k >= 1, pltpu.roll(la, 1, 0), 0.0)
        b = b + jnp.where(step_k >= 2, pltpu.roll(b, 2, 0), 0.0)
        b_last = jnp.where(odd_k, b[PAIR_ROWS - 1:PAIR_ROWS, :], b[SAMPLE_T - 1:SAMPLE_T, :])
        q = q_ref[:, h * dk:(h + 1) * dk] * (dk ** -0.5)
        k = k_ref[:, h * dk:(h + 1) * dk]
        v = v_ref[:, h * dv:(h + 1) * dv]
        q_i = q * jnp.exp(b)
        k_i = k * jnp.exp(-b)
        k_d = k * jnp.exp(b_last - b)
        o = jnp.sum(q_i * k_i, axis=-1, keepdims=True) * v
        for d in range(1, SAMPLE_T):
            a_d = jnp.sum(q_i * pltpu.roll(k_i, d, 0), axis=-1, keepdims=True)
            o = o + jnp.where(step_v >= d, a_d * pltpu.roll(v, d, 0), 0.0)
        q_b = q_i.astype(BF16)
        vb = jnp.concatenate([v.astype(BF16), vpad], axis=0)
        o_par = []
        for e in range(2):
            s_old = s0_ref[e, h]
            o_par.append(_dot(q_b, s_old.astype(BF16)))
            sel = (r_k >= SAMPLE_T) if e == 1 else (r_k < SAMPLE_T)
            kd_e = jnp.concatenate([jnp.where(sel, k_d, 0.0), kpad], axis=0).astype(BF16)
            la_e = jnp.concatenate([jnp.where(sel, la, 0.0), kpad], axis=0)
            hi, lo = _split_bf16(la_e)
            decay = jnp.exp(_dot_tn(hi, ones) + _dot_tn(lo, ones))
            decay_full = jnp.concatenate([decay] * (dv // LANES), axis=1)
            s1_ref[e, h] = decay_full * s_old + _dot_tn(kd_e, vb)
        o = o + jnp.where(odd_v, o_par[1], o_par[0])
        o_ref[:, h * dv:(h + 1) * dv] = o


def _gla_sample(zs, zag, wa2, ba, state):
    n = zs.shape[0]
    bd, h, dk, dv = state.shape
    state_spec = pl.BlockSpec((2, h, dk, dv), lambda i: (i, 0, 0, 0))
    return pl.pallas_call(
        _gla_sample_kernel,
        grid=(n // PAIR_ROWS,),
        in_specs=[
            pl.BlockSpec((PAIR_ROWS, GLA_DK_TOTAL), lambda i: (i, OFF_QG // GLA_DK_TOTAL)),
            pl.BlockSpec((PAIR_ROWS, GLA_DK_TOTAL), lambda i: (i, OFF_KG // GLA_DK_TOTAL)),
            pl.BlockSpec((PAIR_ROWS, GLA_DV_TOTAL), lambda i: (i, OFF_VG // GLA_DV_TOTAL)),
            pl.BlockSpec((PAIR_ROWS, LANES), lambda i: (i, 0)),
            pl.BlockSpec((LANES, GLA_DK_TOTAL), lambda i: (0, 0)),
            pl.BlockSpec((1, GLA_DK_TOTAL), lambda i: (0, 0)),
            state_spec,
        ],
        out_specs=[
            pl.BlockSpec((PAIR_ROWS, GLA_DV_TOTAL), lambda i: (i, 0)),
            state_spec,
        ],
        out_shape=[
            jax.ShapeDtypeStruct((n, GLA_DV_TOTAL), F32),
            jax.ShapeDtypeStruct(state.shape, F32),
        ],
        compiler_params=_cparams(("arbitrary",)),
        name="gla_sample",
    )(zs, zs, zs, zag, wa2, ba, state)


def _merge_kernel(osw_ref, og_ref, rg_ref, gs_ref, gg_ref, x_ref, gn_ref, pswa_ref, pgla_ref, wo_ref, n2_ref,
                  x1_ref, h2_ref):
    a = _dot(osw_ref[...], pswa_ref[...])
    gn = gn_ref[...]
    og = jnp.concatenate(
        [_gla_out(og_ref[:, h * GLA_DV:(h + 1) * GLA_DV].astype(F32), gn,
                  rg_ref[:, h * GLA_DV:(h + 1) * GLA_DV].astype(F32)).astype(BF16)
         for h in range(GLA_HEADS)], axis=1)
    b = _dot(og, pgla_ref[...])
    y = _sigmoid(gs_ref[...].astype(F32)) * a + _sigmoid(gg_ref[...].astype(F32)) * b
    x1 = x_ref[...] + _dot(y.astype(BF16), wo_ref[...])
    x1_ref[...] = x1
    h2_ref[...] = _rms(x1, n2_ref[...]).astype(BF16)


def _merge(o_swa, o_gla, z, x2d, gnorm, p_swa, p_gla, w_o, norm2, tm):
    n = x2d.shape[0]
    resident = functools.partial(pl.BlockSpec, pipeline_mode=pl.Buffered(1))
    return pl.pallas_call(
        _merge_kernel,
        grid=(n // tm,),
        in_specs=[
            pl.BlockSpec((tm, SWA_Q), lambda i: (i, 0)),
            pl.BlockSpec((tm, GLA_DV_TOTAL), lambda i: (i, 0)),
            pl.BlockSpec((tm, GLA_DV_TOTAL), lambda i: (i, OFF_RG // GLA_DV_TOTAL)),
            pl.BlockSpec((tm, D_MODEL), lambda i: (i, OFF_GS // D_MODEL)),
            pl.BlockSpec((tm, D_MODEL), lambda i: (i, OFF_GG // D_MODEL)),
            pl.BlockSpec((tm, D_MODEL), lambda i: (i, 0)),
            pl.BlockSpec((1, GLA_DV), lambda i: (0, 0)),
            resident((SWA_Q, D_MODEL), lambda i: (0, 0)),
            resident((GLA_DV_TOTAL, D_MODEL), lambda i: (0, 0)),
            resident((D_MODEL, D_MODEL), lambda i: (0, 0)),
            pl.BlockSpec((1, D_MODEL), lambda i: (0, 0)),
        ],
        out_specs=[
            pl.BlockSpec((tm, D_MODEL), lambda i: (i, 0)),
            pl.BlockSpec((tm, D_MODEL), lambda i: (i, 0)),
        ],
        out_shape=[
            jax.ShapeDtypeStruct((n, D_MODEL), F32),
            jax.ShapeDtypeStruct((n, D_MODEL), BF16),
        ],
        compiler_params=_cparams(("arbitrary",)),
        name="merge",
    )(o_swa, o_gla, z, z, z, x2d, gnorm, p_swa, p_gla, w_o, norm2)


def _mlp_kernel(h2_ref, x1_ref, wup_ref, wdn_ref, fn_ref, out_ref, acc_ref):
    f = pl.program_id(1)
    nf = pl.num_programs(1)
    @pl.when(f == 0)
    def _():
        acc_ref[...] = jnp.zeros_like(acc_ref)

    u = _dot(h2_ref[...], wup_ref[...])
    u = jnp.square(jnp.maximum(u, 0.0)).astype(BF16)
    acc_ref[...] += _dot(u, wdn_ref[...])

    @pl.when(f == nf - 1)
    def _():
        out_ref[...] = _rms(x1_ref[...] + acc_ref[...], fn_ref[...])


def _mlp(h2, x1, w_up, w_down, final_norm, tm, tf):
    n = h2.shape[0]
    return pl.pallas_call(
        _mlp_kernel,
        grid=(n // tm, D_FF // tf),
        in_specs=[
            pl.BlockSpec((tm, D_MODEL), lambda i, f: (i, 0)),
            pl.BlockSpec((tm, D_MODEL), lambda i, f: (i, 0)),
            pl.BlockSpec((D_MODEL, tf), lambda i, f: (0, f)),
            pl.BlockSpec((tf, D_MODEL), lambda i, f: (f, 0)),
            pl.BlockSpec((1, D_MODEL), lambda i, f: (0, 0)),
        ],
        out_specs=pl.BlockSpec((tm, D_MODEL), lambda i, f: (i, 0)),
        out_shape=jax.ShapeDtypeStruct((n, D_MODEL), F32),
        scratch_shapes=[pltpu.VMEM((tm, D_MODEL), F32)],
        compiler_params=_cparams(("arbitrary", "arbitrary")),
        name="mlp",
    )(h2, x1, w_up, w_down, final_norm)


def _pick_tile(n, pref):
    t = min(n, pref)
    while n % t:
        t //= 2
    return t


def _tile_plan(n_prompt, n_sample, t_prompt, b_sample):
    return {
        "proj_rows_p": _pick_tile(n_prompt, 512), "proj_rows_s": _pick_tile(n_sample, 512),
        "swa_pairs": _pick_tile(b_sample // 2, 4),
        "gla_rows": _pick_tile(t_prompt, GLA_TILE),
        "merge_rows_p": _pick_tile(n_prompt, 256), "merge_rows_s": _pick_tile(n_sample, 256),
        "mlp_rows_p": _pick_tile(n_prompt, 512), "mlp_rows_s": _pick_tile(n_sample, 512),
        "mlp_ff": 1024,
    }


W_IN_SPLITS = (("qs", SWA_Q), ("ks", SWA_KV), ("vs", SWA_KV), ("qg", GLA_DK_TOTAL), ("kg", GLA_DK_TOTAL),
               ("vg", GLA_DV_TOTAL), ("rg", GLA_DV_TOTAL), ("ag", GLA_GATE_RANK), ("gs", D_MODEL), ("gg", D_MODEL))
W_IN_COLS = sum(width for _, width in W_IN_SPLITS)


def _reorder_w_in_kernel(w_ref, wm_ref, wag_ref):
    src = {}
    o = 0
    for name, width in W_IN_SPLITS:
        src[name] = o
        o += width
    cast = lambda lo, width: w_ref[:, lo:lo + width].astype(BF16)
    for name, dst in (("gs", OFF_GS), ("gg", OFF_GG), ("vg", OFF_VG), ("rg", OFF_RG), ("qg", OFF_QG),
                      ("kg", OFF_KG), ("ks", OFF_KS), ("vs", OFF_VS)):
        width = dict(W_IN_SPLITS)[name]
        wm_ref[:, dst:dst + width] = cast(src[name], width)
    for h in range(GROUP):
        slab = [cast(src["qs"] + (g * GROUP + h) * HEAD_DIM, HEAD_DIM) for g in range(N_KV_HEADS)]
        wm_ref[:, OFF_QS + h * SWA_KV:OFF_QS + (h + 1) * SWA_KV] = jnp.concatenate(slab, axis=1)
    ag = cast(src["ag"], GLA_GATE_RANK)
    wag_ref[...] = jnp.concatenate([ag, jnp.zeros((ag.shape[0], LANES - GLA_GATE_RANK), BF16)], axis=1)


def _reorder_w_in(w_in, tk=256):
    d = w_in.shape[1]
    return pl.pallas_call(
        _reorder_w_in_kernel,
        grid=(d // tk,),
        in_specs=[pl.BlockSpec((None, tk, W_IN_COLS), lambda i: (0, i, 0))],
        out_specs=[pl.BlockSpec((tk, Z_COLS), lambda i: (i, 0)), pl.BlockSpec((tk, LANES), lambda i: (i, 0))],
        out_shape=[jax.ShapeDtypeStruct((d, Z_COLS), BF16), jax.ShapeDtypeStruct((d, LANES), BF16)],
        compiler_params=_cparams(("arbitrary",)),
        name="reorder_w_in",
    )(w_in)


def kernel(x_prompt, x_sample, cache_swa_k, cache_swa_v, state_gla, norm1, w_in, w_a2, b_a, sink,
           gla_norm, p_swa, p_gla, w_o, norm2, w_up, w_down, final_norm):
    assert norm1.shape[0] == 1, "single-layer stack"
    bp, tp, d = x_prompt.shape
    bs, ts, _ = x_sample.shape
    assert ts == SAMPLE_T and bs % 2 == 0 and tp % WINDOW == 0
    w_buf = cache_swa_k.shape[2]
    assert w_buf == WINDOW

    w_main, w_ag = _reorder_w_in(w_in)
    wa2 = jnp.pad(w_a2[0], ((0, LANES - GLA_GATE_RANK), (0, 0))).astype(BF16)
    ba = b_a[0][None, :]
    n1, n2, fn = norm1[0][None, :], norm2[0][None, :], final_norm[None, :]
    gn = gla_norm[0][None, :]
    pswa = p_swa[0].reshape(N_KV_HEADS, GROUP, HEAD_DIM, d).transpose(1, 0, 2, 3).reshape(SWA_Q, d).astype(BF16)
    pgla, wo = p_gla[0].astype(BF16), w_o[0].astype(BF16)
    wup, wdn = w_up[0].astype(BF16), w_down[0].astype(BF16)
    sink_smem = sink[0][None, :]
    sink_rows = jnp.broadcast_to(jnp.repeat(sink[0], PAIR_ROWS)[:, None], (N_HEADS * PAIR_ROWS, LANES))

    xp = x_prompt.reshape(bp * tp, d)
    xs = x_sample.reshape(bs * ts, d)
    np_, ns = xp.shape[0], xs.shape[0]

    tiles = _tile_plan(np_, ns, tp, bs)

    zp, zagp = _in_proj(xp, n1, w_main, w_ag, tiles["proj_rows_p"], Z_COLS // 2, BF16, resident_w=True)
    zs, zags = _in_proj(xs, n1, w_main, w_ag, tiles["proj_rows_s"], Z_COLS // 4, F32, resident_w=False)

    zp3 = zp.reshape(bp, tp, Z_COLS)
    o_swa_p, k_last, v_last = _swa_prompt(zp3, sink_smem, _rope_tables(jnp.arange(tp)))
    pos_s = PAST_LEN + jnp.arange(ts)
    tabs_s = _rope_tables(jnp.concatenate([pos_s, pos_s]))
    o_swa_s, nk_s, nv_s = _swa_sample(zs, cache_swa_k[0].reshape(bs, w_buf, SWA_KV),
                                      cache_swa_v[0].reshape(bs, w_buf, SWA_KV), sink_rows, tabs_s,
                                      tiles["swa_pairs"])

    o_gla_p, s_p = _gla_prompt(zp3, zagp.reshape(bp, tp, LANES), wa2, ba, tiles["gla_rows"])
    o_gla_s, s_s = _gla_sample(zs, zags, wa2, ba, state_gla[0])

    x1p, h2p = _merge(o_swa_p.reshape(np_, SWA_Q), o_gla_p.reshape(np_, GLA_DV_TOTAL), zp, xp, gn,
                      pswa, pgla, wo, n2, tiles["merge_rows_p"])
    x1s, h2s = _merge(o_swa_s, o_gla_s, zs, xs, gn, pswa, pgla, wo, n2, tiles["merge_rows_s"])
    yp = _mlp(h2p, x1p, wup, wdn, fn, tiles["mlp_rows_p"], tiles["mlp_ff"])
    ys = _mlp(h2s, x1s, wup, wdn, fn, tiles["mlp_rows_s"], tiles["mlp_ff"])

    kv5 = lambda a, nb: a.reshape(1, nb, w_buf, N_KV_HEADS, HEAD_DIM)
    return (yp.reshape(bp, tp, d), ys.reshape(bs, ts, d),
            kv5(k_last, bp), kv5(v_last, bp), s_p[None],
            kv5(nk_s, bs), kv5(nv_s, bs), s_s[None])
```

```python
import functools

import jax
import jax.numpy as jnp
import numpy as np
from jax import lax
from jax.experimental import pallas as pl
from jax.experimental.pallas import tpu as pltpu

F32 = jnp.float32
BF16 = jnp.bfloat16

D_MODEL = 2048
PAST_LEN = 8192
N_HEADS = 16
N_KV_HEADS = 4
GROUP = N_HEADS // N_KV_HEADS
HEAD_DIM = 64
WINDOW = 128
ROT_DIM = HEAD_DIM // 4
ROPE_THETA = 500000.0
SWA_Q = N_HEADS * HEAD_DIM
SWA_KV = N_KV_HEADS * HEAD_DIM
GLA_HEADS = 4
GLA_DK = 256
GLA_DV = 512
GLA_DK_TOTAL = GLA_HEADS * GLA_DK
GLA_DV_TOTAL = GLA_HEADS * GLA_DV
GLA_GATE_RANK = 16
GLA_GATE_NORM = 16.0
D_FF = 4 * D_MODEL
EPS = 1e-6
LOG2E = 1.4426950408889634

LANES = 128
SUBLANES = 8
VMEM_LIMIT = 56 * 1024 * 1024

OFF_GS = 0
OFF_GG = OFF_GS + D_MODEL
OFF_VG = OFF_GG + D_MODEL
OFF_RG = OFF_VG + GLA_DV_TOTAL
OFF_QG = OFF_RG + GLA_DV_TOTAL
OFF_KG = OFF_QG + GLA_DK_TOTAL
OFF_QS = OFF_KG + GLA_DK_TOTAL
OFF_KS = OFF_QS + SWA_Q
OFF_VS = OFF_KS + SWA_KV
Z_COLS = OFF_VS + SWA_KV

GLA_BLOCK = 16
GLA_TILE = 256


def _cparams(sem):
    return pltpu.CompilerParams(dimension_semantics=sem, vmem_limit_bytes=VMEM_LIMIT)


def _rms(x, w):
    return x * lax.rsqrt(jnp.mean(x * x, axis=-1, keepdims=True) + EPS) * w


def _sigmoid(x):
    return 1.0 / (1.0 + jnp.exp(-x))


def _dot(a, b):
    return jnp.dot(a, b, preferred_element_type=F32)


def _dot_nt(a, b):
    return lax.dot_general(a, b, (((1,), (1,)), ((), ())), preferred_element_type=F32)


def _dot_tn(a, b):
    return lax.dot_general(a, b, (((0,), (0,)), ((), ())), preferred_element_type=F32)


def _split_bf16(x):
    hi = x.astype(BF16)
    lo = (x - hi.astype(F32)).astype(BF16)
    return hi, lo


def _in_proj_kernel(x_ref, nw_ref, w_ref, wag_ref, z_ref, zag_ref):
    h = _rms(x_ref[...], nw_ref[...]).astype(BF16)
    z_ref[...] = _dot_nt(h, w_ref[...]).astype(z_ref.dtype)

    @pl.when(pl.program_id(0) == 0)
    def _():
        zag_ref[...] = _dot_nt(h, wag_ref[...])


def _in_proj(x2d, norm1, w_main, w_ag, tm, tn, z_dtype, resident_w):
    n = x2d.shape[0]
    nrow = n // tm
    w_mode = {"pipeline_mode": pl.Buffered(1)} if resident_w else {}
    return pl.pallas_call(
        _in_proj_kernel,
        grid=(Z_COLS // tn, nrow),
        in_specs=[
            pl.BlockSpec((tm, D_MODEL), lambda j, i: (i, 0)),
            pl.BlockSpec((1, D_MODEL), lambda j, i: (0, 0)),
            pl.BlockSpec((tn, D_MODEL), lambda j, i: (j, 0), **w_mode),
            pl.BlockSpec((LANES, D_MODEL), lambda j, i: (0, 0)),
        ],
        out_specs=[
            pl.BlockSpec((tm, tn), lambda j, i: (i, j)),
            pl.BlockSpec((tm, LANES), lambda j, i: (jnp.where(j == 0, i, nrow - 1), 0)),
        ],
        out_shape=[
            jax.ShapeDtypeStruct((n, Z_COLS), z_dtype),
            jax.ShapeDtypeStruct((n, LANES), F32),
        ],
        compiler_params=_cparams(("arbitrary", "arbitrary")),
        name="in_proj",
    )(x2d, norm1, w_main, w_ag)


def _rope_tables(pos):
    half = ROT_DIM // 2
    inv = ROPE_THETA ** (-jnp.arange(half, dtype=F32) * 2.0 / ROT_DIM)
    ang = pos.astype(F32)[:, None] * inv[None, :]
    cos, sin = jnp.cos(ang), jnp.sin(ang)
    t = pos.shape[0]
    ones = jnp.ones((t, HEAD_DIM - ROT_DIM), F32)
    zeros = jnp.zeros((t, HEAD_DIM - ROT_DIM), F32)
    zh = jnp.zeros((t, half), F32)
    c = jnp.concatenate([cos, cos, ones], axis=1)
    s_lo = jnp.concatenate([zh, sin, zeros], axis=1)
    s_hi = jnp.concatenate([-sin, zh, zeros], axis=1)
    rep = LANES // HEAD_DIM
    return tuple(jnp.tile(a, (1, rep)) for a in (c, s_lo, s_hi))


def _rope(x, c, s_lo, s_hi):
    half = ROT_DIM // 2
    outs = []
    for j in range(x.shape[1] // LANES):
        xc = x[:, j * LANES:(j + 1) * LANES]
        outs.append(xc * c + pltpu.roll(xc, half, 1) * s_lo + pltpu.roll(xc, LANES - half, 1) * s_hi)
    return outs[0] if len(outs) == 1 else jnp.concatenate(outs, axis=1)


def _swa_prompt_kernel(sink_ref, q_ref, k_ref, v_ref, c_ref, slo_ref, shi_ref,
                       o_ref, klast_ref, vlast_ref, kprev_ref, vprev_ref):
    n = pl.program_id(1)
    nb = pl.num_programs(1)
    w = WINDOW

    @pl.when(n == 0)
    def _():
        kprev_ref[...] = jnp.zeros_like(kprev_ref)
        vprev_ref[...] = jnp.zeros_like(vprev_ref)

    kvw = SWA_KV
    c, s_lo, s_hi = c_ref[...], slo_ref[...], shi_ref[...]
    q = (_rope(q_ref[...].astype(F32), c, s_lo, s_hi) * (HEAD_DIM ** -0.5 * LOG2E)).astype(BF16)
    k = _rope(k_ref[...].astype(F32), c, s_lo, s_hi)
    v = v_ref[...].astype(F32)
    kc = jnp.concatenate([kprev_ref[...], k], axis=0).astype(BF16)
    vc = jnp.concatenate([vprev_ref[...], v], axis=0).astype(BF16)

    rows = lax.broadcasted_iota(jnp.int32, (w, 2 * w), 0)
    cols = lax.broadcasted_iota(jnp.int32, (w, 2 * w), 1)
    diff = rows + w - cols
    bias = jnp.where((diff >= 0) & (diff < w) & ((cols >= w) | (n > 0)), 0.0, -jnp.inf)
    bias = jnp.concatenate([bias] * GROUP, axis=0)
    head_shift = HEAD_DIM.bit_length() - 1
    lane_head_q = lax.broadcasted_iota(jnp.int32, (w, kvw), 1) >> head_shift
    lane_head = lax.broadcasted_iota(jnp.int32, (2 * w, kvw), 1) >> head_shift

    for g in range(N_KV_HEADS):
        in_g_q = jnp.where(lane_head_q == g, 1.0, 0.0).astype(BF16)
        in_g = jnp.where(lane_head == g, 1.0, 0.0).astype(BF16)
        qg = jnp.concatenate([q[:, h * kvw:(h + 1) * kvw] * in_g_q for h in range(GROUP)], axis=0)
        sk = jnp.concatenate([jnp.full((w, LANES), sink_ref[0, g * GROUP + h] * LOG2E, F32)
                              for h in range(GROUP)], axis=0)
        s = _dot_nt(qg, kc) + bias
        m = jnp.maximum(jnp.broadcast_to(jnp.max(s, axis=-1, keepdims=True), sk.shape), sk)
        p = jnp.exp2(s - jnp.concatenate([m, m], axis=1)).astype(BF16)
        vg = vc * in_g + (1.0 - in_g)
        oa = _dot(p, vg)
        p_sink = jnp.exp2(sk - m)
        denom = pltpu.roll(oa, 2 * HEAD_DIM, 1) + jnp.concatenate([p_sink, p_sink], axis=1)
        o = (oa / denom).astype(o_ref.dtype)
        for h in range(GROUP):
            lo = g * HEAD_DIM
            o_ref[:, h * kvw + lo:h * kvw + lo + HEAD_DIM] = o[h * w:(h + 1) * w, lo:lo + HEAD_DIM]

    kprev_ref[...] = k
    vprev_ref[...] = v

    @pl.when(n == nb - 1)
    def _():
        klast_ref[...] = k
        vlast_ref[...] = v


def _swa_prompt(z3, sink, tables):
    b, t, _ = z3.shape
    w = WINDOW
    tab_spec = pl.BlockSpec((w, LANES), lambda i, n: (n, 0))
    return pl.pallas_call(
        _swa_prompt_kernel,
        grid=(b, t // w),
        in_specs=[
            pl.BlockSpec(memory_space=pltpu.SMEM),
            pl.BlockSpec((None, w, SWA_Q), lambda i, n: (i, n, OFF_QS // SWA_Q)),
            pl.BlockSpec((None, w, SWA_KV), lambda i, n: (i, n, OFF_KS // SWA_KV)),
            pl.BlockSpec((None, w, SWA_KV), lambda i, n: (i, n, OFF_VS // SWA_KV)),
            tab_spec, tab_spec, tab_spec,
        ],
        out_specs=[
            pl.BlockSpec((None, w, SWA_Q), lambda i, n: (i, n, 0)),
            pl.BlockSpec((None, w, SWA_KV), lambda i, n: (i, 0, 0)),
            pl.BlockSpec((None, w, SWA_KV), lambda i, n: (i, 0, 0)),
        ],
        out_shape=[
            jax.ShapeDtypeStruct((b, t, SWA_Q), BF16),
            jax.ShapeDtypeStruct((b, w, SWA_KV), F32),
            jax.ShapeDtypeStruct((b, w, SWA_KV), F32),
        ],
        scratch_shapes=[pltpu.VMEM((w, SWA_KV), F32), pltpu.VMEM((w, SWA_KV), F32)],
        compiler_params=_cparams(("arbitrary", "arbitrary")),
        name="swa_prompt",
    )(sink, z3, z3, z3, *tables)


SAMPLE_T = 4
PAIR_ROWS = 2 * SAMPLE_T


def _swa_sample_kernel(q_ref, k_ref, v_ref, ck_ref, cv_ref, c_ref, slo_ref, shi_ref, sink_ref,
                       o_ref, nk_ref, nv_ref, *, pairs):
    w = WINDOW
    kvw = SWA_KV
    c, s_lo, s_hi = c_ref[...], slo_ref[...], shi_ref[...]
    nrow = N_HEADS * PAIR_ROWS
    row = lax.broadcasted_iota(jnp.int32, (nrow, w), 0)
    col = lax.broadcasted_iota(jnp.int32, (nrow, w), 1)
    t_shift = SAMPLE_T.bit_length() - 1
    sq = row & (SAMPLE_T - 1)
    par = (row >> t_shift) & 1
    mask_cache = col > sq
    mask_new = (col < PAIR_ROWS) & ((col >> t_shift) == par) & ((col & (SAMPLE_T - 1)) <= sq)
    lane = lax.broadcasted_iota(jnp.int32, (PAIR_ROWS, kvw), 1) >> (HEAD_DIM.bit_length() - 1)
    pos = lax.broadcasted_iota(jnp.int32, (kvw, w), 1)
    sk = sink_ref[...][:, :1]
    zpad = jnp.zeros((w - PAIR_ROWS, kvw), F32)

    for p in range(pairs):
        rs = slice(p * PAIR_ROWS, (p + 1) * PAIR_ROWS)
        q8 = _rope(q_ref[rs, :], c, s_lo, s_hi) * (HEAD_DIM ** -0.5)
        k8 = _rope(k_ref[rs, :], c, s_lo, s_hi)
        v8 = v_ref[rs, :]
        blocks = [jnp.where(lane == g, q8[:, h * kvw:(h + 1) * kvw], 0.0)
                  for g in range(N_KV_HEADS) for h in range(GROUP)]
        qall = jnp.concatenate(blocks, axis=0).astype(BF16)

        k_new = jnp.concatenate([k8, zpad], axis=0)
        v_new = jnp.concatenate([v8, zpad], axis=0)
        s_cache = [_dot(qall, ck_ref[2 * p + e].astype(BF16)) for e in range(2)]
        s_cache = jnp.where(mask_cache, jnp.where(par == 1, s_cache[1], s_cache[0]), -jnp.inf)
        s_new = jnp.where(mask_new, _dot_nt(qall, k_new.astype(BF16)), -jnp.inf)
        s = jnp.concatenate([s_cache, s_new], axis=1)
        m = jnp.maximum(jnp.max(s, axis=-1, keepdims=True), sk)
        pr = jnp.exp(s - m)
        denom = jnp.sum(pr, axis=-1, keepdims=True) + jnp.exp(sk - m)
        p_cache, p_new = pr[:, :w], pr[:, w:]
        p0 = jnp.where(par == 0, p_cache, 0.0).astype(BF16)
        p1 = jnp.where(par == 1, p_cache, 0.0).astype(BF16)
        oall = (_dot_nt(p0, cv_ref[2 * p].astype(BF16)) + _dot_nt(p1, cv_ref[2 * p + 1].astype(BF16))
                + _dot(p_new.astype(BF16), v_new.astype(BF16))) / denom

        outs = []
        for h in range(GROUP):
            acc = None
            for g in range(N_KV_HEADS):
                hh = g * GROUP + h
                blk = jnp.where(lane == g, oall[hh * PAIR_ROWS:(hh + 1) * PAIR_ROWS], 0.0)
                acc = blk if acc is None else acc + blk
            outs.append(acc)
        o_ref[rs, :] = jnp.concatenate(outs, axis=1).astype(o_ref.dtype)

        for new, cref, nref in ((k_new, ck_ref, nk_ref), (v_new, cv_ref, nv_ref)):
            new_t = new.T
            for e in range(2):
                bd = 2 * p + e
                shifted = pltpu.roll(cref[bd], w - SAMPLE_T, 1)
                tail = pltpu.roll(new_t, w - SAMPLE_T - e * SAMPLE_T, 1)
                nref[bd] = jnp.where(pos >= w - SAMPLE_T, tail, shifted)


def _swa_sample(zs, cache_k, cache_v, sink_rows, tables, pairs):
    n = zs.shape[0]
    bd, kvw, w = cache_k.shape
    rows = pairs * PAIR_ROWS
    tab_spec = pl.BlockSpec((PAIR_ROWS, LANES), lambda i: (0, 0))
    cache_spec = pl.BlockSpec((2 * pairs, kvw, w), lambda i: (i, 0, 0))
    return pl.pallas_call(
        functools.partial(_swa_sample_kernel, pairs=pairs),
        grid=(n // rows,),
        in_specs=[
            pl.BlockSpec((rows, SWA_Q), lambda i: (i, OFF_QS // SWA_Q)),
            pl.BlockSpec((rows, SWA_KV), lambda i: (i, OFF_KS // SWA_KV)),
            pl.BlockSpec((rows, SWA_KV), lambda i: (i, OFF_VS // SWA_KV)),
            cache_spec, cache_spec,
            tab_spec, tab_spec, tab_spec,
            pl.BlockSpec((N_HEADS * PAIR_ROWS, LANES), lambda i: (0, 0)),
        ],
        out_specs=[
            pl.BlockSpec((rows, SWA_Q), lambda i: (i, 0)),
            cache_spec, cache_spec,
        ],
        out_shape=[
            jax.ShapeDtypeStruct((n, SWA_Q), BF16),
            jax.ShapeDtypeStruct((bd, kvw, w), F32),
            jax.ShapeDtypeStruct((bd, kvw, w), F32),
        ],
        compiler_params=_cparams(("arbitrary",)),
        name="swa_sample",
    )(zs, zs, zs, cache_k, cache_v, *tables, sink_rows)


def _log_decay(ag, wa2, ba):
    x = _dot(ag.astype(BF16), wa2) + ba
    log_sig = jnp.minimum(x, 0.0) - jnp.log(1.0 + jnp.exp(-jnp.abs(x)))
    return log_sig * (1.0 / GLA_GATE_NORM)


def _gla_out(o, gnorm, rg):
    dv = o.shape[-1]
    ms = jnp.broadcast_to(jnp.sum(o * o, axis=-1, keepdims=True), (o.shape[0], LANES)) * (1.0 / dv)
    r = lax.rsqrt(ms + EPS)
    half = rg * 0.5
    gate = half + half * jnp.tanh(half)
    return o * jnp.concatenate([r] * (dv // LANES), axis=1) * gnorm * gate


def _gla_constants(tt):
    nb = tt // GLA_BLOCK
    halves = [tt >> (l + 1) for l in range(nb.bit_length() - 1)]
    step = np.arange(tt)
    blk = step // GLA_BLOCK
    t16 = ((blk[:, None] == blk[None, :]) & (step[None, :] <= step[:, None])).astype(np.float32)
    rows = [(blk[None, :] < np.arange(nb)[:, None])]
    for hs in halves:
        t_b = (np.arange(nb) * GLA_BLOCK) // (2 * hs) * (2 * hs) + hs
        rows.append(step[None, :] < t_b[:, None])
    rows.append(np.ones((GLA_BLOCK, tt), bool))
    sel = np.concatenate(rows, axis=0).astype(np.float32)
    group = [(step[:, None] // (2 * hs)) == (step[None, :] // (2 * hs)) for hs in halves[1:]]
    mlev = np.stack(group).astype(np.float32)
    return (jnp.asarray(t16, BF16), jnp.asarray(sel, BF16), jnp.asarray(t16), jnp.asarray(mlev)), halves


def _gla_prompt_kernel(q_ref, k_ref, v_ref, ag_ref, wa2_ref, ba_ref,
                       t16_ref, sel_ref, mdiag_ref, mlev_ref, o_ref, sout_ref, st_ref, *, halves):
    t = pl.program_id(1)
    nt = pl.num_programs(1)
    tt = q_ref.shape[0]
    nb = tt // GLA_BLOCK
    nl = len(halves)
    dk, dv, width = GLA_DK, GLA_DV, GLA_DK_TOTAL

    @pl.when(t == 0)
    def _():
        st_ref[...] = jnp.zeros_like(st_ref)

    la = _log_decay(ag_ref[...], wa2_ref[...], ba_ref[...]) * LOG2E
    hi, lo = _split_bf16(la)
    c = _dot(t16_ref[...], hi) + _dot(t16_ref[...], lo)
    cum = _dot(sel_ref[...], hi) + _dot(sel_ref[...], lo)
    p_start = cum[:nb]
    total = cum[(1 + nl) * nb:(1 + nl) * nb + 1]
    exp_p = jnp.exp2(p_start)
    d_last = total - p_start
    after, before = [], []
    for l in range(nl):
        d = p_start - cum[(1 + l) * nb:(2 + l) * nb]
        after.append(jnp.exp2(jnp.minimum(d, 0.0)))
        before.append(-d)

    zeros = jnp.zeros((GLA_BLOCK, width), BF16)
    names = ["qs", "kinv", "qt", "kd"] + [f"q{l}" for l in range(nl)] + [f"k{l}" for l in range(nl)]
    parts = {nm: [] for nm in names}
    for i in range(nb):
        rs = slice(i * GLA_BLOCK, (i + 1) * GLA_BLOCK)
        row = slice(i, i + 1)
        c_b = c[rs]
        k_b = k_ref[rs, :].astype(F32)
        qs = q_ref[rs, :].astype(F32) * (dk ** -0.5) * jnp.exp2(c_b)
        parts["qs"].append(qs.astype(BF16))
        parts["kinv"].append((k_b * jnp.exp2(-c_b)).astype(BF16))
        parts["qt"].append((qs * exp_p[row]).astype(BF16))
        parts["kd"].append((k_b * jnp.exp2(d_last[row] - c_b)).astype(BF16))
        for l, hs in enumerate(halves):
            if (i * GLA_BLOCK) % (2 * hs) >= hs:
                parts[f"q{l}"].append((qs * after[l][row]).astype(BF16))
                parts[f"k{l}"].append(zeros)
            else:
                parts[f"q{l}"].append(zeros)
                parts[f"k{l}"].append((k_b * jnp.exp2(before[l][row] - c_b)).astype(BF16))
    full = {nm: jnp.concatenate(parts[nm], axis=0) for nm in names}

    in_block = mdiag_ref[...] != 0.0
    for h in range(GLA_HEADS):
        ks = slice(h * dk, (h + 1) * dk)
        vs = slice(h * dv, (h + 1) * dv)
        a = jnp.where(in_block, _dot_nt(full["qs"][:, ks], full["kinv"][:, ks]), 0.0)
        for l in range(nl):
            x = _dot_nt(full[f"q{l}"][:, ks], full[f"k{l}"][:, ks])
            a = a + (x if l == 0 else jnp.where(mlev_ref[l - 1] != 0.0, x, 0.0))
        v_h = v_ref[:, vs]
        st = st_ref[h]
        o = _dot(a.astype(BF16), v_h) + _dot_nt(full["qt"][:, ks], st.astype(BF16))
        st_ref[h] = jnp.exp2(total[:, ks]) * st + _dot_tn(v_h, full["kd"][:, ks])
        o_ref[:, vs] = o.astype(o_ref.dtype)

    @pl.when(t == nt - 1)
    def _():
        for h in range(GLA_HEADS):
            sout_ref[h] = st_ref[h].T


def _gla_prompt(z3, zag3, wa2, ba, tt):
    b, t, _ = z3.shape
    h = GLA_HEADS
    consts, halves = _gla_constants(tt)
    const_specs = [pl.BlockSpec(c.shape, lambda i, s, nd=c.ndim: (0,) * nd) for c in consts]
    return pl.pallas_call(
        functools.partial(_gla_prompt_kernel, halves=tuple(halves)),
        grid=(b, t // tt),
        in_specs=[
            pl.BlockSpec((None, tt, GLA_DK_TOTAL), lambda i, s: (i, s, OFF_QG // GLA_DK_TOTAL)),
            pl.BlockSpec((None, tt, GLA_DK_TOTAL), lambda i, s: (i, s, OFF_KG // GLA_DK_TOTAL)),
            pl.BlockSpec((None, tt, GLA_DV_TOTAL), lambda i, s: (i, s, OFF_VG // GLA_DV_TOTAL)),
            pl.BlockSpec((None, tt, LANES), lambda i, s: (i, s, 0)),
            pl.BlockSpec((LANES, GLA_DK_TOTAL), lambda i, s: (0, 0)),
            pl.BlockSpec((1, GLA_DK_TOTAL), lambda i, s: (0, 0)),
            *const_specs,
        ],
        out_specs=[
            pl.BlockSpec((None, tt, GLA_DV_TOTAL), lambda i, s: (i, s, 0)),
            pl.BlockSpec((None, h, GLA_DK, GLA_DV), lambda i, s: (i, 0, 0, 0)),
        ],
        out_shape=[
            jax.ShapeDtypeStruct((b, t, GLA_DV_TOTAL), BF16),
            jax.ShapeDtypeStruct((b, h, GLA_DK, GLA_DV), F32),
        ],
        scratch_shapes=[pltpu.VMEM((h, GLA_DV, GLA_DK), F32)],
        compiler_params=_cparams(("arbitrary", "arbitrary")),
        name="gla_prompt",
    )(z3, z3, z3, zag3, wa2, ba, *consts)


def _gla_sample_kernel(q_ref, k_ref, v_ref, ag_ref, wa2_ref, ba_ref, s0_ref, o_ref, s1_ref):
    dk, dv = GLA_DK, GLA_DV
    r_k = lax.broadcasted_iota(jnp.int32, (PAIR_ROWS, dk), 0)
    r_v = lax.broadcasted_iota(jnp.int32, (PAIR_ROWS, dv), 0)
    step_k = r_k & (SAMPLE_T - 1)
    step_v = r_v & (SAMPLE_T - 1)
    odd_k = r_k >= SAMPLE_T
    odd_v = r_v >= SAMPLE_T
    la_all = _log_decay(ag_ref[...], wa2_ref[...], ba_ref[...])
    kpad = jnp.zeros((LANES - PAIR_ROWS, dk), F32)
    vpad = jnp.zeros((LANES - PAIR_ROWS, dv), BF16)
    ones = jnp.ones((LANES, LANES), BF16)

    for h in range(GLA_HEADS):
        la = la_all[:, h * dk:(h + 1) * dk]
        b = la + jnp.where(step_k >= 1, pltpu.roll(la, 1, 0), 0.0)
        b = b + jnp.where(step_k >= 2, pltpu.roll(b, 2, 0), 0.0)
        b_last = jnp.where(odd_k, b[PAIR_ROWS - 1:PAIR_ROWS, :], b[SAMPLE_T - 1:SAMPLE_T, :])
        q = q_ref[:, h * dk:(h + 1) * dk] * (dk ** -0.5)
        k = k_ref[:, h * dk:(h + 1) * dk]
        v = v_ref[:, h * dv:(h + 1) * dv]
        q_i = q * jnp.exp(b)
        k_i = k * jnp.exp(-b)
        k_d = k * jnp.exp(b_last - b)
        o = jnp.sum(q_i * k_i, axis=-1, keepdims=True) * v
        for d in range(1, SAMPLE_T):
            a_d = jnp.sum(q_i * pltpu.roll(k_i, d, 0), axis=-1, keepdims=True)
            o = o + jnp.where(step_v >= d, a_d * pltpu.roll(v, d, 0), 0.0)
        q_b = q_i.astype(BF16)
        vb = jnp.concatenate([v.astype(BF16), vpad], axis=0)
        o_par = []
        for e in range(2):
            s_old = s0_ref[e, h]
            o_par.append(_dot(q_b, s_old.astype(BF16)))
            sel = (r_k >= SAMPLE_T) if e == 1 else (r_k < SAMPLE_T)
            kd_e = jnp.concatenate([jnp.where(sel, k_d, 0.0), kpad], axis=0).astype(BF16)
            la_e = jnp.concatenate([jnp.where(sel, la, 0.0), kpad], axis=0)
            hi, lo = _split_bf16(la_e)
            decay = jnp.exp(_dot_tn(hi, ones) + _dot_tn(lo, ones))
            decay_full = jnp.concatenate([decay] * (dv // LANES), axis=1)
            s1_ref[e, h] = decay_full * s_old + _dot_tn(kd_e, vb)
        o = o + jnp.where(odd_v, o_par[1], o_par[0])
        o_ref[:, h * dv:(h + 1) * dv] = o


def _gla_sample(zs, zag, wa2, ba, state):
    n = zs.shape[0]
    bd, h, dk, dv = state.shape
    state_spec = pl.BlockSpec((2, h, dk, dv), lambda i: (i, 0, 0, 0))
    return pl.pallas_call(
        _gla_sample_kernel,
        grid=(n // PAIR_ROWS,),
        in_specs=[
            pl.BlockSpec((PAIR_ROWS, GLA_DK_TOTAL), lambda i: (i, OFF_QG // GLA_DK_TOTAL)),
            pl.BlockSpec((PAIR_ROWS, GLA_DK_TOTAL), lambda i: (i, OFF_KG // GLA_DK_TOTAL)),
            pl.BlockSpec((PAIR_ROWS, GLA_DV_TOTAL), lambda i: (i, OFF_VG // GLA_DV_TOTAL)),
            pl.BlockSpec((PAIR_ROWS, LANES), lambda i: (i, 0)),
            pl.BlockSpec((LANES, GLA_DK_TOTAL), lambda i: (0, 0)),
            pl.BlockSpec((1, GLA_DK_TOTAL), lambda i: (0, 0)),
            state_spec,
        ],
        out_specs=[
            pl.BlockSpec((PAIR_ROWS, GLA_DV_TOTAL), lambda i: (i, 0)),
            state_spec,
        ],
        out_shape=[
            jax.ShapeDtypeStruct((n, GLA_DV_TOTAL), F32),
            jax.ShapeDtypeStruct(state.shape, F32),
        ],
        compiler_params=_cparams(("arbitrary",)),
        name="gla_sample",
    )(zs, zs, zs, zag, wa2, ba, state)


def _merge_kernel(osw_ref, og_ref, rg_ref, gs_ref, gg_ref, x_ref, gn_ref, pswa_ref, pgla_ref, wo_ref, n2_ref,
                  x1_ref, h2_ref):
    a = _dot(osw_ref[...], pswa_ref[...])
    gn = gn_ref[...]
    og = jnp.concatenate(
        [_gla_out(og_ref[:, h * GLA_DV:(h + 1) * GLA_DV].astype(F32), gn,
                  rg_ref[:, h * GLA_DV:(h + 1) * GLA_DV].astype(F32)).astype(BF16)
         for h in range(GLA_HEADS)], axis=1)
    b = _dot(og, pgla_ref[...])
    y = _sigmoid(gs_ref[...].astype(F32)) * a + _sigmoid(gg_ref[...].astype(F32)) * b
    x1 = x_ref[...] + _dot(y.astype(BF16), wo_ref[...])
    x1_ref[...] = x1
    h2_ref[...] = _rms(x1, n2_ref[...]).astype(BF16)


def _merge(o_swa, o_gla, z, x2d, gnorm, p_swa, p_gla, w_o, norm2, tm):
    n = x2d.shape[0]
    resident = functools.partial(pl.BlockSpec, pipeline_mode=pl.Buffered(1))
    return pl.pallas_call(
        _merge_kernel,
        grid=(n // tm,),
        in_specs=[
            pl.BlockSpec((tm, SWA_Q), lambda i: (i, 0)),
            pl.BlockSpec((tm, GLA_DV_TOTAL), lambda i: (i, 0)),
            pl.BlockSpec((tm, GLA_DV_TOTAL), lambda i: (i, OFF_RG // GLA_DV_TOTAL)),
            pl.BlockSpec((tm, D_MODEL), lambda i: (i, OFF_GS // D_MODEL)),
            pl.BlockSpec((tm, D_MODEL), lambda i: (i, OFF_GG // D_MODEL)),
            pl.BlockSpec((tm, D_MODEL), lambda i: (i, 0)),
            pl.BlockSpec((1, GLA_DV), lambda i: (0, 0)),
            resident((SWA_Q, D_MODEL), lambda i: (0, 0)),
            resident((GLA_DV_TOTAL, D_MODEL), lambda i: (0, 0)),
            resident((D_MODEL, D_MODEL), lambda i: (0, 0)),
            pl.BlockSpec((1, D_MODEL), lambda i: (0, 0)),
        ],
        out_specs=[
            pl.BlockSpec((tm, D_MODEL), lambda i: (i, 0)),
            pl.BlockSpec((tm, D_MODEL), lambda i: (i, 0)),
        ],
        out_shape=[
            jax.ShapeDtypeStruct((n, D_MODEL), F32),
            jax.ShapeDtypeStruct((n, D_MODEL), BF16),
        ],
        compiler_params=_cparams(("arbitrary",)),
        name="merge",
    )(o_swa, o_gla, z, z, z, x2d, gnorm, p_swa, p_gla, w_o, norm2)


def _mlp_kernel(h2_ref, x1_ref, wup_ref, wdn_ref, fn_ref, out_ref, acc_ref):
    f = pl.program_id(1)
    nf = pl.num_programs(1)
    @pl.when(f == 0)
    def _():
        acc_ref[...] = jnp.zeros_like(acc_ref)

    u = _dot(h2_ref[...], wup_ref[...])
    u = jnp.square(jnp.maximum(u, 0.0)).astype(BF16)
    acc_ref[...] += _dot(u, wdn_ref[...])

    @pl.when(f == nf - 1)
    def _():
        out_ref[...] = _rms(x1_ref[...] + acc_ref[...], fn_ref[...])


def _mlp(h2, x1, w_up, w_down, final_norm, tm, tf):
    n = h2.shape[0]
    return pl.pallas_call(
        _mlp_kernel,
        grid=(n // tm, D_FF // tf),
        in_specs=[
            pl.BlockSpec((tm, D_MODEL), lambda i, f: (i, 0)),
            pl.BlockSpec((tm, D_MODEL), lambda i, f: (i, 0)),
            pl.BlockSpec((D_MODEL, tf), lambda i, f: (0, f)),
            pl.BlockSpec((tf, D_MODEL), lambda i, f: (f, 0)),
            pl.BlockSpec((1, D_MODEL), lambda i, f: (0, 0)),
        ],
        out_specs=pl.BlockSpec((tm, D_MODEL), lambda i, f: (i, 0)),
        out_shape=jax.ShapeDtypeStruct((n, D_MODEL), F32),
        scratch_shapes=[pltpu.VMEM((tm, D_MODEL), F32)],
        compiler_params=_cparams(("arbitrary", "arbitrary")),
        name="mlp",
    )(h2, x1, w_up, w_down, final_norm)


def _pick_tile(n, pref):
    t = min(n, pref)
    while n % t:
        t //= 2
    return t


def _tile_plan(n_prompt, n_sample, t_prompt, b_sample):
    return {
        "proj_rows_p": _pick_tile(n_prompt, 512), "proj_rows_s": _pick_tile(n_sample, 512),
        "swa_pairs": _pick_tile(b_sample // 2, 4),
        "gla_rows": _pick_tile(t_prompt, GLA_TILE),
        "merge_rows_p": _pick_tile(n_prompt, 256), "merge_rows_s": _pick_tile(n_sample, 256),
        "mlp_rows_p": _pick_tile(n_prompt, 512), "mlp_rows_s": _pick_tile(n_sample, 512),
        "mlp_ff": 1024,
    }


W_IN_SPLITS = (("qs", SWA_Q), ("ks", SWA_KV), ("vs", SWA_KV), ("qg", GLA_DK_TOTAL), ("kg", GLA_DK_TOTAL),
               ("vg", GLA_DV_TOTAL), ("rg", GLA_DV_TOTAL), ("ag", GLA_GATE_RANK), ("gs", D_MODEL), ("gg", D_MODEL))
W_IN_COLS = sum(width for _, width in W_IN_SPLITS)


REORDER_PIECE = HEAD_DIM
REORDER_PIECES = 4


def _w_in_row_table():
    src, o = {}, 0
    for name, width in W_IN_SPLITS:
        src[name] = o
        o += width
    rows = np.zeros(Z_COLS, np.int64)
    for name, dst in (("gs", OFF_GS), ("gg", OFF_GG), ("vg", OFF_VG), ("rg", OFF_RG), ("qg", OFF_QG),
                      ("kg", OFF_KG), ("ks", OFF_KS), ("vs", OFF_VS)):
        width = dict(W_IN_SPLITS)[name]
        rows[dst:dst + width] = src[name] + np.arange(width)
    for h in range(GROUP):
        for g in range(N_KV_HEADS):
            dst = OFF_QS + h * SWA_KV + g * HEAD_DIM
            rows[dst:dst + HEAD_DIM] = src["qs"] + (g * GROUP + h) * HEAD_DIM + np.arange(HEAD_DIM)
    pieces = rows.reshape(-1, REORDER_PIECE)
    assert (pieces == pieces[:, :1] + np.arange(REORDER_PIECE)).all()
    return jnp.asarray(pieces[:, 0], jnp.int32), src["ag"]


def _reorder_w_in_kernel(tbl_ref, *refs):
    piece_refs, ag_ref, wm_ref, wag_ref = refs[:REORDER_PIECES], refs[-3], refs[-2], refs[-1]
    for k, p in enumerate(piece_refs):
        wm_ref[k * REORDER_PIECE:(k + 1) * REORDER_PIECE, :] = p[0].astype(BF16)

    @pl.when(pl.program_id(0) == 0)
    def _():
        pad = jnp.zeros((LANES - GLA_GATE_RANK, wag_ref.shape[1]), BF16)
        wag_ref[...] = jnp.concatenate([ag_ref[0].astype(BF16), pad], axis=0)


def _reorder_w_in(w_in_t):
    d = w_in_t.shape[2]
    table, ag_row = _w_in_row_table()
    rows = REORDER_PIECE * REORDER_PIECES
    piece = lambda k: pl.BlockSpec((pl.Element(1), pl.Element(REORDER_PIECE), pl.Element(d)),
                                   lambda i, tbl: (0, pl.multiple_of(tbl[i * REORDER_PIECES + k], GLA_GATE_RANK), 0))
    grid_spec = pltpu.PrefetchScalarGridSpec(
        num_scalar_prefetch=1,
        grid=(Z_COLS // rows,),
        in_specs=[piece(k) for k in range(REORDER_PIECES)]
        + [pl.BlockSpec((pl.Element(1), pl.Element(GLA_GATE_RANK), pl.Element(d)),
                        lambda i, tbl: (0, ag_row, 0))],
        out_specs=[pl.BlockSpec((rows, d), lambda i, tbl: (i, 0)),
                   pl.BlockSpec((LANES, d), lambda i, tbl: (0, 0))],
    )
    return pl.pallas_call(
        _reorder_w_in_kernel,
        grid_spec=grid_spec,
        out_shape=[jax.ShapeDtypeStruct((Z_COLS, d), BF16), jax.ShapeDtypeStruct((LANES, d), BF16)],
        compiler_params=_cparams(("arbitrary",)),
        name="reorder_w_in",
    )(table, *([w_in_t] * (REORDER_PIECES + 1)))


def kernel(x_prompt, x_sample, cache_swa_k, cache_swa_v, state_gla, norm1, w_in, w_a2, b_a, sink,
           gla_norm, p_swa, p_gla, w_o, norm2, w_up, w_down, final_norm):
    assert norm1.shape[0] == 1, "single-layer stack"
    bp, tp, d = x_prompt.shape
    bs, ts, _ = x_sample.shape
    assert ts == SAMPLE_T and bs % 2 == 0 and tp % WINDOW == 0
    w_buf = cache_swa_k.shape[2]
    assert w_buf == WINDOW

    w_main, w_ag = _reorder_w_in(jnp.swapaxes(w_in, 1, 2))
    wa2 = jnp.pad(w_a2[0], ((0, LANES - GLA_GATE_RANK), (0, 0))).astype(BF16)
    ba = b_a[0][None, :]
    n1, n2, fn = norm1[0][None, :], norm2[0][None, :], final_norm[None, :]
    gn = gla_norm[0][None, :]
    pswa = p_swa[0].reshape(N_KV_HEADS, GROUP, HEAD_DIM, d).transpose(1, 0, 2, 3).reshape(SWA_Q, d).astype(BF16)
    pgla, wo = p_gla[0].astype(BF16), w_o[0].astype(BF16)
    wup, wdn = w_up[0].astype(BF16), w_down[0].astype(BF16)
    sink_smem = sink[0][None, :]
    sink_rows = jnp.broadcast_to(jnp.repeat(sink[0], PAIR_ROWS)[:, None], (N_HEADS * PAIR_ROWS, LANES))

    xp = x_prompt.reshape(bp * tp, d)
    xs = x_sample.reshape(bs * ts, d)
    np_, ns = xp.shape[0], xs.shape[0]

    tiles = _tile_plan(np_, ns, tp, bs)

    zp, zagp = _in_proj(xp, n1, w_main, w_ag, tiles["proj_rows_p"], Z_COLS // 2, BF16, resident_w=True)
    zs, zags = _in_proj(xs, n1, w_main, w_ag, tiles["proj_rows_s"], Z_COLS // 4, F32, resident_w=False)

    zp3 = zp.reshape(bp, tp, Z_COLS)
    o_swa_p, k_last, v_last = _swa_prompt(zp3, sink_smem, _rope_tables(jnp.arange(tp)))
    pos_s = PAST_LEN + jnp.arange(ts)
    tabs_s = _rope_tables(jnp.concatenate([pos_s, pos_s]))
    pos_minor = lambda c: jnp.transpose(c[0], (0, 2, 3, 1)).reshape(bs, SWA_KV, w_buf)
    pos_major = lambda c: jnp.transpose(c.reshape(bs, N_KV_HEADS, HEAD_DIM, w_buf), (0, 3, 1, 2))[None]
    o_swa_s, nk_s, nv_s = _swa_sample(zs, pos_minor(cache_swa_k), pos_minor(cache_swa_v), sink_rows, tabs_s,
                                      tiles["swa_pairs"])

    o_gla_p, s_p = _gla_prompt(zp3, zagp.reshape(bp, tp, LANES), wa2, ba, tiles["gla_rows"])
    o_gla_s, s_s = _gla_sample(zs, zags, wa2, ba, state_gla[0])

    x1p, h2p = _merge(o_swa_p.reshape(np_, SWA_Q), o_gla_p.reshape(np_, GLA_DV_TOTAL), zp, xp, gn,
                      pswa, pgla, wo, n2, tiles["merge_rows_p"])
    x1s, h2s = _merge(o_swa_s, o_gla_s, zs, xs, gn, pswa, pgla, wo, n2, tiles["merge_rows_s"])
    yp = _mlp(h2p, x1p, wup, wdn, fn, tiles["mlp_rows_p"], tiles["mlp_ff"])
    ys = _mlp(h2s, x1s, wup, wdn, fn, tiles["mlp_rows_s"], tiles["mlp_ff"])

    kv5 = lambda a, nb: a.reshape(1, nb, w_buf, N_KV_HEADS, HEAD_DIM)
    return (yp.reshape(bp, tp, d), ys.reshape(bs, ts, d),
            kv5(k_last, bp), kv5(v_last, bp), s_p[None],
            pos_major(nk_s), pos_major(nv_s), s_s[None])
```

```python
import functools

import jax
import jax.numpy as jnp
import numpy as np
from jax import lax
from jax.experimental import pallas as pl
from jax.experimental.pallas import tpu as pltpu

F32 = jnp.float32
BF16 = jnp.bfloat16

D_MODEL = 2048
PAST_LEN = 8192
N_HEADS = 16
N_KV_HEADS = 4
GROUP = N_HEADS // N_KV_HEADS
HEAD_DIM = 64
WINDOW = 128
ROT_DIM = HEAD_DIM // 4
ROPE_THETA = 500000.0
SWA_Q = N_HEADS * HEAD_DIM
SWA_KV = N_KV_HEADS * HEAD_DIM
GLA_HEADS = 4
GLA_DK = 256
GLA_DV = 512
GLA_DK_TOTAL = GLA_HEADS * GLA_DK
GLA_DV_TOTAL = GLA_HEADS * GLA_DV
GLA_GATE_RANK = 16
GLA_GATE_NORM = 16.0
D_FF = 4 * D_MODEL
EPS = 1e-6
LOG2E = 1.4426950408889634

LANES = 128
SUBLANES = 8
VMEM_LIMIT = 56 * 1024 * 1024

OFF_GS = 0
OFF_GG = OFF_GS + D_MODEL
OFF_VG = OFF_GG + D_MODEL
OFF_RG = OFF_VG + GLA_DV_TOTAL
OFF_QG = OFF_RG + GLA_DV_TOTAL
OFF_KG = OFF_QG + GLA_DK_TOTAL
OFF_QS = OFF_KG + GLA_DK_TOTAL
OFF_KS = OFF_QS + SWA_Q
OFF_VS = OFF_KS + SWA_KV
Z_COLS = OFF_VS + SWA_KV

GLA_BLOCK = 16
GLA_TILE = 256


def _cparams(sem):
    return pltpu.CompilerParams(dimension_semantics=sem, vmem_limit_bytes=VMEM_LIMIT)


def _rms(x, w):
    return x * lax.rsqrt(jnp.mean(x * x, axis=-1, keepdims=True) + EPS) * w


def _sigmoid(x):
    return 1.0 / (1.0 + jnp.exp(-x))


def _dot(a, b):
    return jnp.dot(a, b, preferred_element_type=F32)


def _dot_nt(a, b):
    return lax.dot_general(a, b, (((1,), (1,)), ((), ())), preferred_element_type=F32)


def _dot_tn(a, b):
    return lax.dot_general(a, b, (((0,), (0,)), ((), ())), preferred_element_type=F32)


def _split_bf16(x):
    hi = x.astype(BF16)
    lo = (x - hi.astype(F32)).astype(BF16)
    return hi, lo


def _in_proj_kernel(x_ref, nw_ref, w_ref, wag_ref, z_ref, zag_ref):
    h = _rms(x_ref[...], nw_ref[...]).astype(BF16)
    z_ref[...] = _dot_nt(h, w_ref[...]).astype(z_ref.dtype)

    @pl.when(pl.program_id(0) == 0)
    def _():
        zag_ref[...] = _dot_nt(h, wag_ref[...])


def _in_proj(x2d, norm1, w_main, w_ag, tm, tn, z_dtype, resident_w):
    n = x2d.shape[0]
    nrow = n // tm
    w_mode = {"pipeline_mode": pl.Buffered(1)} if resident_w else {}
    return pl.pallas_call(
        _in_proj_kernel,
        grid=(Z_COLS // tn, nrow),
        in_specs=[
            pl.BlockSpec((tm, D_MODEL), lambda j, i: (i, 0)),
            pl.BlockSpec((1, D_MODEL), lambda j, i: (0, 0)),
            pl.BlockSpec((tn, D_MODEL), lambda j, i: (j, 0), **w_mode),
            pl.BlockSpec((LANES, D_MODEL), lambda j, i: (0, 0)),
        ],
        out_specs=[
            pl.BlockSpec((tm, tn), lambda j, i: (i, j)),
            pl.BlockSpec((tm, LANES), lambda j, i: (jnp.where(j == 0, i, nrow - 1), 0)),
        ],
        out_shape=[
            jax.ShapeDtypeStruct((n, Z_COLS), z_dtype),
            jax.ShapeDtypeStruct((n, LANES), F32),
        ],
        compiler_params=_cparams(("arbitrary", "arbitrary")),
        name="in_proj",
    )(x2d, norm1, w_main, w_ag)


def _rope_tables(pos):
    half = ROT_DIM // 2
    inv = ROPE_THETA ** (-jnp.arange(half, dtype=F32) * 2.0 / ROT_DIM)
    ang = pos.astype(F32)[:, None] * inv[None, :]
    cos, sin = jnp.cos(ang), jnp.sin(ang)
    t = pos.shape[0]
    ones = jnp.ones((t, HEAD_DIM - ROT_DIM), F32)
    zeros = jnp.zeros((t, HEAD_DIM - ROT_DIM), F32)
    zh = jnp.zeros((t, half), F32)
    c = jnp.concatenate([cos, cos, ones], axis=1)
    s_lo = jnp.concatenate([zh, sin, zeros], axis=1)
    s_hi = jnp.concatenate([-sin, zh, zeros], axis=1)
    rep = LANES // HEAD_DIM
    return tuple(jnp.tile(a, (1, rep)) for a in (c, s_lo, s_hi))


def _rope(x, c, s_lo, s_hi):
    half = ROT_DIM // 2
    outs = []
    for j in range(x.shape[1] // LANES):
        xc = x[:, j * LANES:(j + 1) * LANES]
        outs.append(xc * c + pltpu.roll(xc, half, 1) * s_lo + pltpu.roll(xc, LANES - half, 1) * s_hi)
    return outs[0] if len(outs) == 1 else jnp.concatenate(outs, axis=1)


def _swa_block(has_prev, q_blk, k_blk, v_blk, tabs, sink_ref, k_prev, v_prev, o_ref, row0):
    w = WINDOW
    kvw = SWA_KV
    c, s_lo, s_hi = tabs
    q = (_rope(q_blk.astype(F32), c, s_lo, s_hi) * (HEAD_DIM ** -0.5 * LOG2E)).astype(BF16)
    k = _rope(k_blk.astype(F32), c, s_lo, s_hi)
    v = v_blk.astype(F32)
    kc = jnp.concatenate([k_prev, k], axis=0).astype(BF16)
    vc = jnp.concatenate([v_prev, v], axis=0).astype(BF16)

    rows = lax.broadcasted_iota(jnp.int32, (w, 2 * w), 0)
    cols = lax.broadcasted_iota(jnp.int32, (w, 2 * w), 1)
    diff = rows + w - cols
    bias = jnp.where((diff >= 0) & (diff < w) & ((cols >= w) | has_prev), 0.0, -jnp.inf)
    bias = jnp.concatenate([bias] * GROUP, axis=0)
    head_shift = HEAD_DIM.bit_length() - 1
    lane_head_q = lax.broadcasted_iota(jnp.int32, (w, kvw), 1) >> head_shift
    lane_head = lax.broadcasted_iota(jnp.int32, (2 * w, kvw), 1) >> head_shift

    for g in range(N_KV_HEADS):
        in_g_q = jnp.where(lane_head_q == g, 1.0, 0.0).astype(BF16)
        in_g = jnp.where(lane_head == g, 1.0, 0.0).astype(BF16)
        qg = jnp.concatenate([q[:, h * kvw:(h + 1) * kvw] * in_g_q for h in range(GROUP)], axis=0)
        sk = jnp.concatenate([jnp.full((w, LANES), sink_ref[0, g * GROUP + h] * LOG2E, F32)
                              for h in range(GROUP)], axis=0)
        s = _dot_nt(qg, kc) + bias
        m = jnp.maximum(jnp.broadcast_to(jnp.max(s, axis=-1, keepdims=True), sk.shape), sk)
        p = jnp.exp2(s - jnp.concatenate([m, m], axis=1)).astype(BF16)
        vg = vc * in_g + (1.0 - in_g)
        oa = _dot(p, vg)
        p_sink = jnp.exp2(sk - m)
        denom = pltpu.roll(oa, 2 * HEAD_DIM, 1) + jnp.concatenate([p_sink, p_sink], axis=1)
        o = (oa / denom).astype(o_ref.dtype)
        for h in range(GROUP):
            lo = g * HEAD_DIM
            o_ref[row0:row0 + w, h * kvw + lo:h * kvw + lo + HEAD_DIM] = o[h * w:(h + 1) * w, lo:lo + HEAD_DIM]
    return k, v


SAMPLE_T = 4
PAIR_ROWS = 2 * SAMPLE_T


def _swa_sample_kernel(q_ref, k_ref, v_ref, ck_ref, cv_ref, c_ref, slo_ref, shi_ref, sink_ref,
                       o_ref, nk_ref, nv_ref, *, pairs):
    w = WINDOW
    kvw = SWA_KV
    c, s_lo, s_hi = c_ref[...], slo_ref[...], shi_ref[...]
    nrow = N_HEADS * PAIR_ROWS
    row = lax.broadcasted_iota(jnp.int32, (nrow, w), 0)
    col = lax.broadcasted_iota(jnp.int32, (nrow, w), 1)
    t_shift = SAMPLE_T.bit_length() - 1
    sq = row & (SAMPLE_T - 1)
    par = (row >> t_shift) & 1
    mask_cache = col > sq
    mask_new = (col < PAIR_ROWS) & ((col >> t_shift) == par) & ((col & (SAMPLE_T - 1)) <= sq)
    lane = lax.broadcasted_iota(jnp.int32, (PAIR_ROWS, kvw), 1) >> (HEAD_DIM.bit_length() - 1)
    pos = lax.broadcasted_iota(jnp.int32, (kvw, w), 1)
    sk = sink_ref[...][:, :1]
    zpad = jnp.zeros((w - PAIR_ROWS, kvw), F32)

    for p in range(pairs):
        rs = slice(p * PAIR_ROWS, (p + 1) * PAIR_ROWS)
        q8 = _rope(q_ref[rs, :], c, s_lo, s_hi) * (HEAD_DIM ** -0.5)
        k8 = _rope(k_ref[rs, :], c, s_lo, s_hi)
        v8 = v_ref[rs, :]
        blocks = [jnp.where(lane == g, q8[:, h * kvw:(h + 1) * kvw], 0.0)
                  for g in range(N_KV_HEADS) for h in range(GROUP)]
        qall = jnp.concatenate(blocks, axis=0).astype(BF16)

        k_new = jnp.concatenate([k8, zpad], axis=0)
        v_new = jnp.concatenate([v8, zpad], axis=0)
        s_cache = [_dot(qall, ck_ref[2 * p + e].astype(BF16)) for e in range(2)]
        s_cache = jnp.where(mask_cache, jnp.where(par == 1, s_cache[1], s_cache[0]), -jnp.inf)
        s_new = jnp.where(mask_new, _dot_nt(qall, k_new.astype(BF16)), -jnp.inf)
        s = jnp.concatenate([s_cache, s_new], axis=1)
        m = jnp.maximum(jnp.max(s, axis=-1, keepdims=True), sk)
        pr = jnp.exp(s - m)
        denom = jnp.sum(pr, axis=-1, keepdims=True) + jnp.exp(sk - m)
        p_cache, p_new = pr[:, :w], pr[:, w:]
        p0 = jnp.where(par == 0, p_cache, 0.0).astype(BF16)
        p1 = jnp.where(par == 1, p_cache, 0.0).astype(BF16)
        oall = (_dot_nt(p0, cv_ref[2 * p].astype(BF16)) + _dot_nt(p1, cv_ref[2 * p + 1].astype(BF16))
                + _dot(p_new.astype(BF16), v_new.astype(BF16))) / denom

        outs = []
        for h in range(GROUP):
            acc = None
            for g in range(N_KV_HEADS):
                hh = g * GROUP + h
                blk = jnp.where(lane == g, oall[hh * PAIR_ROWS:(hh + 1) * PAIR_ROWS], 0.0)
                acc = blk if acc is None else acc + blk
            outs.append(acc)
        o_ref[rs, :] = jnp.concatenate(outs, axis=1).astype(o_ref.dtype)

        for new, cref, nref in ((k_new, ck_ref, nk_ref), (v_new, cv_ref, nv_ref)):
            new_t = new.T
            for e in range(2):
                bd = 2 * p + e
                shifted = pltpu.roll(cref[bd], w - SAMPLE_T, 1)
                tail = pltpu.roll(new_t, w - SAMPLE_T - e * SAMPLE_T, 1)
                nref[bd] = jnp.where(pos >= w - SAMPLE_T, tail, shifted)


def _swa_sample(zs, cache_k, cache_v, sink_rows, tables, pairs):
    n = zs.shape[0]
    bd, kvw, w = cache_k.shape
    rows = pairs * PAIR_ROWS
    tab_spec = pl.BlockSpec((PAIR_ROWS, LANES), lambda i: (0, 0))
    cache_spec = pl.BlockSpec((2 * pairs, kvw, w), lambda i: (i, 0, 0))
    return pl.pallas_call(
        functools.partial(_swa_sample_kernel, pairs=pairs),
        grid=(n // rows,),
        in_specs=[
            pl.BlockSpec((rows, SWA_Q), lambda i: (i, OFF_QS // SWA_Q)),
            pl.BlockSpec((rows, SWA_KV), lambda i: (i, OFF_KS // SWA_KV)),
            pl.BlockSpec((rows, SWA_KV), lambda i: (i, OFF_VS // SWA_KV)),
            cache_spec, cache_spec,
            tab_spec, tab_spec, tab_spec,
            pl.BlockSpec((N_HEADS * PAIR_ROWS, LANES), lambda i: (0, 0)),
        ],
        out_specs=[
            pl.BlockSpec((rows, SWA_Q), lambda i: (i, 0)),
            cache_spec, cache_spec,
        ],
        out_shape=[
            jax.ShapeDtypeStruct((n, SWA_Q), BF16),
            jax.ShapeDtypeStruct((bd, kvw, w), F32),
            jax.ShapeDtypeStruct((bd, kvw, w), F32),
        ],
        compiler_params=_cparams(("arbitrary",)),
        name="swa_sample",
    )(zs, zs, zs, cache_k, cache_v, *tables, sink_rows)


def _log_decay(ag, wa2, ba):
    x = _dot(ag.astype(BF16), wa2) + ba
    log_sig = jnp.minimum(x, 0.0) - jnp.log(1.0 + jnp.exp(-jnp.abs(x)))
    return log_sig * (1.0 / GLA_GATE_NORM)


def _gla_out(o, gnorm, rg):
    dv = o.shape[-1]
    ms = jnp.broadcast_to(jnp.sum(o * o, axis=-1, keepdims=True), (o.shape[0], LANES)) * (1.0 / dv)
    r = lax.rsqrt(ms + EPS)
    half = rg * 0.5
    gate = half + half * jnp.tanh(half)
    return o * jnp.concatenate([r] * (dv // LANES), axis=1) * gnorm * gate


def _gla_constants(tt):
    nb = tt // GLA_BLOCK
    halves = [tt >> (l + 1) for l in range(nb.bit_length() - 1)]
    step = np.arange(tt)
    blk = step // GLA_BLOCK
    t16 = ((blk[:, None] == blk[None, :]) & (step[None, :] <= step[:, None])).astype(np.float32)
    rows = [(blk[None, :] < np.arange(nb)[:, None])]
    for hs in halves:
        t_b = (np.arange(nb) * GLA_BLOCK) // (2 * hs) * (2 * hs) + hs
        rows.append(step[None, :] < t_b[:, None])
    rows.append(np.ones((GLA_BLOCK, tt), bool))
    sel = np.concatenate(rows, axis=0).astype(np.float32)
    group = [(step[:, None] // (2 * hs)) == (step[None, :] // (2 * hs)) for hs in halves[1:]]
    mlev = np.stack(group).astype(np.float32)
    return (jnp.asarray(t16, BF16), jnp.asarray(sel, BF16), jnp.asarray(t16), jnp.asarray(mlev)), halves


def _gla_prompt_kernel(q_ref, k_ref, v_ref, ag_ref, wa2_ref, ba_ref,
                       t16_ref, sel_ref, mdiag_ref, mlev_ref, o_ref, sout_ref, st_ref, *, halves):
    t = pl.program_id(1)
    nt = pl.num_programs(1)
    tt = q_ref.shape[0]
    nb = tt // GLA_BLOCK
    nl = len(halves)
    dk, dv, width = GLA_DK, GLA_DV, GLA_DK_TOTAL

    @pl.when(t == 0)
    def _():
        st_ref[...] = jnp.zeros_like(st_ref)

    la = _log_decay(ag_ref[...], wa2_ref[...], ba_ref[...]) * LOG2E
    hi, lo = _split_bf16(la)
    c = _dot(t16_ref[...], hi) + _dot(t16_ref[...], lo)
    cum = _dot(sel_ref[...], hi) + _dot(sel_ref[...], lo)
    p_start = cum[:nb]
    total = cum[(1 + nl) * nb:(1 + nl) * nb + 1]
    exp_p = jnp.exp2(p_start)
    d_last = total - p_start
    after, before = [], []
    for l in range(nl):
        d = p_start - cum[(1 + l) * nb:(2 + l) * nb]
        after.append(jnp.exp2(jnp.minimum(d, 0.0)))
        before.append(-d)

    zeros = jnp.zeros((GLA_BLOCK, width), BF16)
    names = ["qs", "kinv", "qt", "kd"] + [f"q{l}" for l in range(nl)] + [f"k{l}" for l in range(nl)]
    parts = {nm: [] for nm in names}
    for i in range(nb):
        rs = slice(i * GLA_BLOCK, (i + 1) * GLA_BLOCK)
        row = slice(i, i + 1)
        c_b = c[rs]
        k_b = k_ref[rs, :].astype(F32)
        qs = q_ref[rs, :].astype(F32) * (dk ** -0.5) * jnp.exp2(c_b)
        parts["qs"].append(qs.astype(BF16))
        parts["kinv"].append((k_b * jnp.exp2(-c_b)).astype(BF16))
        parts["qt"].append((qs * exp_p[row]).astype(BF16))
        parts["kd"].append((k_b * jnp.exp2(d_last[row] - c_b)).astype(BF16))
        for l, hs in enumerate(halves):
            if (i * GLA_BLOCK) % (2 * hs) >= hs:
                parts[f"q{l}"].append((qs * after[l][row]).astype(BF16))
                parts[f"k{l}"].append(zeros)
            else:
                parts[f"q{l}"].append(zeros)
                parts[f"k{l}"].append((k_b * jnp.exp2(before[l][row] - c_b)).astype(BF16))
    full = {nm: jnp.concatenate(parts[nm], axis=0) for nm in names}

    in_block = mdiag_ref[...] != 0.0
    for h in range(GLA_HEADS):
        ks = slice(h * dk, (h + 1) * dk)
        vs = slice(h * dv, (h + 1) * dv)
        a = jnp.where(in_block, _dot_nt(full["qs"][:, ks], full["kinv"][:, ks]), 0.0)
        for l in range(nl):
            x = _dot_nt(full[f"q{l}"][:, ks], full[f"k{l}"][:, ks])
            a = a + (x if l == 0 else jnp.where(mlev_ref[l - 1] != 0.0, x, 0.0))
        v_h = v_ref[:, vs]
        st = st_ref[h]
        o = _dot(a.astype(BF16), v_h) + _dot_nt(full["qt"][:, ks], st.astype(BF16))
        st_ref[h] = jnp.exp2(total[:, ks]) * st + _dot_tn(v_h, full["kd"][:, ks])
        o_ref[:, vs] = o.astype(o_ref.dtype)

    @pl.when(t == nt - 1)
    def _():
        for h in range(GLA_HEADS):
            sout_ref[h] = st_ref[h].T


def _gla_prompt(z3, zag3, wa2, ba, tt):
    b, t, _ = z3.shape
    h = GLA_HEADS
    consts, halves = _gla_constants(tt)
    const_specs = [pl.BlockSpec(c.shape, lambda i, s, nd=c.ndim: (0,) * nd) for c in consts]
    return pl.pallas_call(
        functools.partial(_gla_prompt_kernel, halves=tuple(halves)),
        grid=(b, t // tt),
        in_specs=[
            pl.BlockSpec((None, tt, GLA_DK_TOTAL), lambda i, s: (i, s, OFF_QG // GLA_DK_TOTAL)),
            pl.BlockSpec((None, tt, GLA_DK_TOTAL), lambda i, s: (i, s, OFF_KG // GLA_DK_TOTAL)),
            pl.BlockSpec((None, tt, GLA_DV_TOTAL), lambda i, s: (i, s, OFF_VG // GLA_DV_TOTAL)),
            pl.BlockSpec((None, tt, LANES), lambda i, s: (i, s, 0)),
            pl.BlockSpec((LANES, GLA_DK_TOTAL), lambda i, s: (0, 0)),
            pl.BlockSpec((1, GLA_DK_TOTAL), lambda i, s: (0, 0)),
            *const_specs,
        ],
        out_specs=[
            pl.BlockSpec((None, tt, GLA_DV_TOTAL), lambda i, s: (i, s, 0)),
            pl.BlockSpec((None, h, GLA_DK, GLA_DV), lambda i, s: (i, 0, 0, 0)),
        ],
        out_shape=[
            jax.ShapeDtypeStruct((b, t, GLA_DV_TOTAL), BF16),
            jax.ShapeDtypeStruct((b, h, GLA_DK, GLA_DV), F32),
        ],
        scratch_shapes=[pltpu.VMEM((h, GLA_DV, GLA_DK), F32)],
        compiler_params=_cparams(("arbitrary", "arbitrary")),
        name="gla_prompt",
    )(z3, z3, z3, zag3, wa2, ba, *consts)


def _gla_sample_kernel(q_ref, k_ref, v_ref, ag_ref, wa2_ref, ba_ref, s0_ref, o_ref, s1_ref):
    dk, dv = GLA_DK, GLA_DV
    r_k = lax.broadcasted_iota(jnp.int32, (PAIR_ROWS, dk), 0)
    r_v = lax.broadcasted_iota(jnp.int32, (PAIR_ROWS, dv), 0)
    step_k = r_k & (SAMPLE_T - 1)
    step_v = r_v & (SAMPLE_T - 1)
    odd_k = r_k >= SAMPLE_T
    odd_v = r_v >= SAMPLE_T
    la_all = _log_decay(ag_ref[...], wa2_ref[...], ba_ref[...])
    kpad = jnp.zeros((LANES - PAIR_ROWS, dk), F32)
    vpad = jnp.zeros((LANES - PAIR_ROWS, dv), BF16)
    ones = jnp.ones((LANES, LANES), BF16)

    for h in range(GLA_HEADS):
        la = la_all[:, h * dk:(h + 1) * dk]
        b = la + jnp.where(step_k >= 1, pltpu.roll(la, 1, 0), 0.0)
        b = b + jnp.where(step_k >= 2, pltpu.roll(b, 2, 0), 0.0)
        b_last = jnp.where(odd_k, b[PAIR_ROWS - 1:PAIR_ROWS, :], b[SAMPLE_T - 1:SAMPLE_T, :])
        q = q_ref[:, h * dk:(h + 1) * dk] * (dk ** -0.5)
        k = k_ref[:, h * dk:(h + 1) * dk]
        v = v_ref[:, h * dv:(h + 1) * dv]
        q_i = q * jnp.exp(b)
        k_i = k * jnp.exp(-b)
        k_d = k * jnp.exp(b_last - b)
        o = jnp.sum(q_i * k_i, axis=-1, keepdims=True) * v
        for d in range(1, SAMPLE_T):
            a_d = jnp.sum(q_i * pltpu.roll(k_i, d, 0), axis=-1, keepdims=True)
            o = o + jnp.where(step_v >= d, a_d * pltpu.roll(v, d, 0), 0.0)
        q_b = q_i.astype(BF16)
        vb = jnp.concatenate([v.astype(BF16), vpad], axis=0)
        o_par = []
        for e in range(2):
            s_old = s0_ref[e, h]
            o_par.append(_dot(q_b, s_old.astype(BF16)))
            sel = (r_k >= SAMPLE_T) if e == 1 else (r_k < SAMPLE_T)
            kd_e = jnp.concatenate([jnp.where(sel, k_d, 0.0), kpad], axis=0).astype(BF16)
            la_e = jnp.concatenate([jnp.where(sel, la, 0.0), kpad], axis=0)
            hi, lo = _split_bf16(la_e)
            decay = jnp.exp(_dot_tn(hi, ones) + _dot_tn(lo, ones))
            decay_full = jnp.concatenate([decay] * (dv // LANES), axis=1)
            s1_ref[e, h] = decay_full * s_old + _dot_tn(kd_e, vb)
        o = o + jnp.where(odd_v, o_par[1], o_par[0])
        o_ref[:, h * dv:(h + 1) * dv] = o


def _gla_sample(zs, zag, wa2, ba, state):
    n = zs.shape[0]
    bd, h, dk, dv = state.shape
    state_spec = pl.BlockSpec((2, h, dk, dv), lambda i: (i, 0, 0, 0))
    return pl.pallas_call(
        _gla_sample_kernel,
        grid=(n // PAIR_ROWS,),
        in_specs=[
            pl.BlockSpec((PAIR_ROWS, GLA_DK_TOTAL), lambda i: (i, OFF_QG // GLA_DK_TOTAL)),
            pl.BlockSpec((PAIR_ROWS, GLA_DK_TOTAL), lambda i: (i, OFF_KG // GLA_DK_TOTAL)),
            pl.BlockSpec((PAIR_ROWS, GLA_DV_TOTAL), lambda i: (i, OFF_VG // GLA_DV_TOTAL)),
            pl.BlockSpec((PAIR_ROWS, LANES), lambda i: (i, 0)),
            pl.BlockSpec((LANES, GLA_DK_TOTAL), lambda i: (0, 0)),
            pl.BlockSpec((1, GLA_DK_TOTAL), lambda i: (0, 0)),
            state_spec,
        ],
        out_specs=[
            pl.BlockSpec((PAIR_ROWS, GLA_DV_TOTAL), lambda i: (i, 0)),
            state_spec,
        ],
        out_shape=[
            jax.ShapeDtypeStruct((n, GLA_DV_TOTAL), F32),
            jax.ShapeDtypeStruct(state.shape, F32),
        ],
        compiler_params=_cparams(("arbitrary",)),
        name="gla_sample",
    )(zs, zs, zs, zag, wa2, ba, state)


def _merge_gla_branch(og_ref, rg_ref, gg_ref, gn_ref, pgla_ref):
    gn = gn_ref[...]
    og = jnp.concatenate(
        [_gla_out(og_ref[:, h * GLA_DV:(h + 1) * GLA_DV].astype(F32), gn,
                  rg_ref[:, h * GLA_DV:(h + 1) * GLA_DV].astype(F32)).astype(BF16)
         for h in range(GLA_HEADS)], axis=1)
    return _sigmoid(gg_ref[...].astype(F32)) * _dot(og, pgla_ref[...])


def _merge_finish(yb, osw_ref, gs_ref, x_ref, pswa_ref, wo_ref, n2_ref, x1_ref, h2_ref):
    y = _sigmoid(gs_ref[...].astype(F32)) * _dot(osw_ref[...], pswa_ref[...]) + yb
    x1 = x_ref[...] + _dot(y.astype(BF16), wo_ref[...])
    x1_ref[...] = x1
    h2_ref[...] = _rms(x1, n2_ref[...]).astype(BF16)


def _merge_kernel(osw_ref, og_ref, rg_ref, gs_ref, gg_ref, x_ref, gn_ref, pswa_ref, pgla_ref, wo_ref, n2_ref,
                  x1_ref, h2_ref):
    yb = _merge_gla_branch(og_ref, rg_ref, gg_ref, gn_ref, pgla_ref)
    _merge_finish(yb, osw_ref, gs_ref, x_ref, pswa_ref, wo_ref, n2_ref, x1_ref, h2_ref)


def _merge_swa_kernel(sink_ref, q_ref, k_ref, v_ref, c_ref, slo_ref, shi_ref, *refs, tiles_per_batch):
    (og_ref, rg_ref, gs_ref, gg_ref, x_ref, gn_ref, pswa_ref, pgla_ref, wo_ref, n2_ref, x1_ref, h2_ref,
     klast_ref, vlast_ref, kprev_ref, vprev_ref, osw_ref) = refs
    w = WINDOW
    t_local = lax.rem(pl.program_id(0), tiles_per_batch)

    @pl.when(t_local == 0)
    def _():
        kprev_ref[...] = jnp.zeros_like(kprev_ref)
        vprev_ref[...] = jnp.zeros_like(vprev_ref)

    yb = _merge_gla_branch(og_ref, rg_ref, gg_ref, gn_ref, pgla_ref)
    k_prev, v_prev = kprev_ref[...], vprev_ref[...]
    for j in range(q_ref.shape[0] // w):
        rs = slice(j * w, (j + 1) * w)
        tabs = (c_ref[rs, :], slo_ref[rs, :], shi_ref[rs, :])
        has_prev = (t_local > 0) if j == 0 else True
        k_prev, v_prev = _swa_block(has_prev, q_ref[rs, :], k_ref[rs, :], v_ref[rs, :], tabs, sink_ref,
                                    k_prev, v_prev, osw_ref, j * w)
    kprev_ref[...] = k_prev
    vprev_ref[...] = v_prev
    _merge_finish(yb, osw_ref, gs_ref, x_ref, pswa_ref, wo_ref, n2_ref, x1_ref, h2_ref)

    @pl.when(t_local == tiles_per_batch - 1)
    def _():
        klast_ref[...] = kprev_ref[...]
        vlast_ref[...] = vprev_ref[...]


def _merge_specs(tm):
    resident = functools.partial(pl.BlockSpec, pipeline_mode=pl.Buffered(1))
    in_specs = [
        pl.BlockSpec((tm, GLA_DV_TOTAL), lambda i: (i, 0)),
        pl.BlockSpec((tm, GLA_DV_TOTAL), lambda i: (i, OFF_RG // GLA_DV_TOTAL)),
        pl.BlockSpec((tm, D_MODEL), lambda i: (i, OFF_GS // D_MODEL)),
        pl.BlockSpec((tm, D_MODEL), lambda i: (i, OFF_GG // D_MODEL)),
        pl.BlockSpec((tm, D_MODEL), lambda i: (i, 0)),
        pl.BlockSpec((1, GLA_DV), lambda i: (0, 0)),
        resident((SWA_Q, D_MODEL), lambda i: (0, 0)),
        resident((GLA_DV_TOTAL, D_MODEL), lambda i: (0, 0)),
        resident((D_MODEL, D_MODEL), lambda i: (0, 0)),
        pl.BlockSpec((1, D_MODEL), lambda i: (0, 0)),
    ]
    out_specs = [pl.BlockSpec((tm, D_MODEL), lambda i: (i, 0)), pl.BlockSpec((tm, D_MODEL), lambda i: (i, 0))]
    return in_specs, out_specs


def _merge_swa(z, o_gla, x2d, gnorm, p_swa, p_gla, w_o, norm2, sink, tables, batch, tm):
    n = x2d.shape[0]
    w = WINDOW
    tiles_per_batch = n // batch // tm
    in_specs, out_specs = _merge_specs(tm)
    tab_spec = pl.BlockSpec((tm, LANES), lambda i: (lax.rem(i, tiles_per_batch), 0))
    last_spec = pl.BlockSpec((None, w, SWA_KV), lambda i: (i // tiles_per_batch, 0, 0))
    return pl.pallas_call(
        functools.partial(_merge_swa_kernel, tiles_per_batch=tiles_per_batch),
        grid=(n // tm,),
        in_specs=[
            pl.BlockSpec(memory_space=pltpu.SMEM),
            pl.BlockSpec((tm, SWA_Q), lambda i: (i, OFF_QS // SWA_Q)),
            pl.BlockSpec((tm, SWA_KV), lambda i: (i, OFF_KS // SWA_KV)),
            pl.BlockSpec((tm, SWA_KV), lambda i: (i, OFF_VS // SWA_KV)),
            tab_spec, tab_spec, tab_spec,
        ] + in_specs,
        out_specs=out_specs + [last_spec, last_spec],
        out_shape=[
            jax.ShapeDtypeStruct((n, D_MODEL), F32),
            jax.ShapeDtypeStruct((n, D_MODEL), BF16),
            jax.ShapeDtypeStruct((batch, w, SWA_KV), F32),
            jax.ShapeDtypeStruct((batch, w, SWA_KV), F32),
        ],
        scratch_shapes=[pltpu.VMEM((w, SWA_KV), F32), pltpu.VMEM((w, SWA_KV), F32),
                        pltpu.VMEM((tm, SWA_Q), BF16)],
        compiler_params=_cparams(("arbitrary",)),
        name="merge_swa",
    )(sink, z, z, z, *tables, o_gla, z, z, z, x2d, gnorm, p_swa, p_gla, w_o, norm2)


def _merge(o_swa, o_gla, z, x2d, gnorm, p_swa, p_gla, w_o, norm2, tm):
    n = x2d.shape[0]
    in_specs, out_specs = _merge_specs(tm)
    return pl.pallas_call(
        _merge_kernel,
        grid=(n // tm,),
        in_specs=[pl.BlockSpec((tm, SWA_Q), lambda i: (i, 0))] + in_specs,
        out_specs=out_specs,
        out_shape=[
            jax.ShapeDtypeStruct((n, D_MODEL), F32),
            jax.ShapeDtypeStruct((n, D_MODEL), BF16),
        ],
        compiler_params=_cparams(("arbitrary",)),
        name="merge",
    )(o_swa, o_gla, z, z, z, x2d, gnorm, p_swa, p_gla, w_o, norm2)


def _mlp_kernel(h2_ref, x1_ref, wup_ref, wdn_ref, fn_ref, out_ref, acc_ref):
    f = pl.program_id(1)
    nf = pl.num_programs(1)
    @pl.when(f == 0)
    def _():
        acc_ref[...] = jnp.zeros_like(acc_ref)

    u = _dot(h2_ref[...], wup_ref[...])
    u = jnp.square(jnp.maximum(u, 0.0)).astype(BF16)
    acc_ref[...] += _dot(u, wdn_ref[...])

    @pl.when(f == nf - 1)
    def _():
        out_ref[...] = _rms(x1_ref[...] + acc_ref[...], fn_ref[...])


def _mlp(h2, x1, w_up, w_down, final_norm, tm, tf):
    n = h2.shape[0]
    return pl.pallas_call(
        _mlp_kernel,
        grid=(n // tm, D_FF // tf),
        in_specs=[
            pl.BlockSpec((tm, D_MODEL), lambda i, f: (i, 0)),
            pl.BlockSpec((tm, D_MODEL), lambda i, f: (i, 0)),
            pl.BlockSpec((D_MODEL, tf), lambda i, f: (0, f)),
            pl.BlockSpec((tf, D_MODEL), lambda i, f: (f, 0)),
            pl.BlockSpec((1, D_MODEL), lambda i, f: (0, 0)),
        ],
        out_specs=pl.BlockSpec((tm, D_MODEL), lambda i, f: (i, 0)),
        out_shape=jax.ShapeDtypeStruct((n, D_MODEL), F32),
        scratch_shapes=[pltpu.VMEM((tm, D_MODEL), F32)],
        compiler_params=_cparams(("arbitrary", "arbitrary")),
        name="mlp",
    )(h2, x1, w_up, w_down, final_norm)


def _pick_tile(n, pref):
    t = min(n, pref)
    while n % t:
        t //= 2
    return t


def _tile_plan(n_prompt, n_sample, t_prompt, b_sample):
    return {
        "proj_rows_p": _pick_tile(n_prompt, 512), "proj_rows_s": _pick_tile(n_sample, 512),
        "swa_pairs": _pick_tile(b_sample // 2, 4),
        "gla_rows": _pick_tile(t_prompt, GLA_TILE),
        "merge_rows_p": _pick_tile(n_prompt, 256), "merge_rows_s": _pick_tile(n_sample, 256),
        "mlp_rows_p": _pick_tile(n_prompt, 512), "mlp_rows_s": _pick_tile(n_sample, 512),
        "mlp_ff": 1024,
    }


W_IN_SPLITS = (("qs", SWA_Q), ("ks", SWA_KV), ("vs", SWA_KV), ("qg", GLA_DK_TOTAL), ("kg", GLA_DK_TOTAL),
               ("vg", GLA_DV_TOTAL), ("rg", GLA_DV_TOTAL), ("ag", GLA_GATE_RANK), ("gs", D_MODEL), ("gg", D_MODEL))
W_IN_COLS = sum(width for _, width in W_IN_SPLITS)


REORDER_PIECE = HEAD_DIM
REORDER_PIECES = 4


def _w_in_row_table():
    src, o = {}, 0
    for name, width in W_IN_SPLITS:
        src[name] = o
        o += width
    rows = np.zeros(Z_COLS, np.int64)
    for name, dst in (("gs", OFF_GS), ("gg", OFF_GG), ("vg", OFF_VG), ("rg", OFF_RG), ("qg", OFF_QG),
                      ("kg", OFF_KG), ("ks", OFF_KS), ("vs", OFF_VS)):
        width = dict(W_IN_SPLITS)[name]
        rows[dst:dst + width] = src[name] + np.arange(width)
    for h in range(GROUP):
        for g in range(N_KV_HEADS):
            dst = OFF_QS + h * SWA_KV + g * HEAD_DIM
            rows[dst:dst + HEAD_DIM] = src["qs"] + (g * GROUP + h) * HEAD_DIM + np.arange(HEAD_DIM)
    pieces = rows.reshape(-1, REORDER_PIECE)
    assert (pieces == pieces[:, :1] + np.arange(REORDER_PIECE)).all()
    return jnp.asarray(pieces[:, 0], jnp.int32), src["ag"]


def _reorder_w_in_kernel(tbl_ref, *refs):
    piece_refs, ag_ref, wm_ref, wag_ref = refs[:REORDER_PIECES], refs[-3], refs[-2], refs[-1]
    for k, p in enumerate(piece_refs):
        wm_ref[k * REORDER_PIECE:(k + 1) * REORDER_PIECE, :] = p[0].astype(BF16)

    @pl.when(pl.program_id(0) == 0)
    def _():
        pad = jnp.zeros((LANES - GLA_GATE_RANK, wag_ref.shape[1]), BF16)
        wag_ref[...] = jnp.concatenate([ag_ref[0].astype(BF16), pad], axis=0)


def _reorder_w_in(w_in_t):
    d = w_in_t.shape[2]
    table, ag_row = _w_in_row_table()
    rows = REORDER_PIECE * REORDER_PIECES
    piece = lambda k: pl.BlockSpec((pl.Element(1), pl.Element(REORDER_PIECE), pl.Element(d)),
                                   lambda i, tbl: (0, pl.multiple_of(tbl[i * REORDER_PIECES + k], GLA_GATE_RANK), 0))
    grid_spec = pltpu.PrefetchScalarGridSpec(
        num_scalar_prefetch=1,
        grid=(Z_COLS // rows,),
        in_specs=[piece(k) for k in range(REORDER_PIECES)]
        + [pl.BlockSpec((pl.Element(1), pl.Element(GLA_GATE_RANK), pl.Element(d)),
                        lambda i, tbl: (0, ag_row, 0))],
        out_specs=[pl.BlockSpec((rows, d), lambda i, tbl: (i, 0)),
                   pl.BlockSpec((LANES, d), lambda i, tbl: (0, 0))],
    )
    return pl.pallas_call(
        _reorder_w_in_kernel,
        grid_spec=grid_spec,
        out_shape=[jax.ShapeDtypeStruct((Z_COLS, d), BF16), jax.ShapeDtypeStruct((LANES, d), BF16)],
        compiler_params=_cparams(("arbitrary",)),
        name="reorder_w_in",
    )(table, *([w_in_t] * (REORDER_PIECES + 1)))


def kernel(x_prompt, x_sample, cache_swa_k, cache_swa_v, state_gla, norm1, w_in, w_a2, b_a, sink,
           gla_norm, p_swa, p_gla, w_o, norm2, w_up, w_down, final_norm):
    assert norm1.shape[0] == 1, "single-layer stack"
    bp, tp, d = x_prompt.shape
    bs, ts, _ = x_sample.shape
    assert ts == SAMPLE_T and bs % 2 == 0 and tp % WINDOW == 0
    w_buf = cache_swa_k.shape[2]
    assert w_buf == WINDOW

    w_main, w_ag = _reorder_w_in(jnp.swapaxes(w_in, 1, 2))
    wa2 = jnp.pad(w_a2[0], ((0, LANES - GLA_GATE_RANK), (0, 0))).astype(BF16)
    ba = b_a[0][None, :]
    n1, n2, fn = norm1[0][None, :], norm2[0][None, :], final_norm[None, :]
    gn = gla_norm[0][None, :]
    pswa = p_swa[0].reshape(N_KV_HEADS, GROUP, HEAD_DIM, d).transpose(1, 0, 2, 3).reshape(SWA_Q, d).astype(BF16)
    pgla, wo = p_gla[0].astype(BF16), w_o[0].astype(BF16)
    wup, wdn = w_up[0].astype(BF16), w_down[0].astype(BF16)
    sink_smem = sink[0][None, :]
    sink_rows = jnp.broadcast_to(jnp.repeat(sink[0], PAIR_ROWS)[:, None], (N_HEADS * PAIR_ROWS, LANES))

    xp = x_prompt.reshape(bp * tp, d)
    xs = x_sample.reshape(bs * ts, d)
    np_, ns = xp.shape[0], xs.shape[0]

    tiles = _tile_plan(np_, ns, tp, bs)

    zp, zagp = _in_proj(xp, n1, w_main, w_ag, tiles["proj_rows_p"], Z_COLS // 2, BF16, resident_w=True)
    zs, zags = _in_proj(xs, n1, w_main, w_ag, tiles["proj_rows_s"], Z_COLS // 4, F32, resident_w=False)

    zp3 = zp.reshape(bp, tp, Z_COLS)
    pos_s = PAST_LEN + jnp.arange(ts)
    tabs_s = _rope_tables(jnp.concatenate([pos_s, pos_s]))
    pos_minor = lambda c: jnp.transpose(c[0], (0, 2, 3, 1)).reshape(bs, SWA_KV, w_buf)
    pos_major = lambda c: jnp.transpose(c.reshape(bs, N_KV_HEADS, HEAD_DIM, w_buf), (0, 3, 1, 2))[None]
    o_swa_s, nk_s, nv_s = _swa_sample(zs, pos_minor(cache_swa_k), pos_minor(cache_swa_v), sink_rows, tabs_s,
                                      tiles["swa_pairs"])

    o_gla_p, s_p = _gla_prompt(zp3, zagp.reshape(bp, tp, LANES), wa2, ba, tiles["gla_rows"])
    o_gla_s, s_s = _gla_sample(zs, zags, wa2, ba, state_gla[0])

    x1p, h2p, k_last, v_last = _merge_swa(zp, o_gla_p.reshape(np_, GLA_DV_TOTAL), xp, gn, pswa, pgla, wo, n2,
                                          sink_smem, _rope_tables(jnp.arange(tp)), bp, tiles["merge_rows_p"])
    x1s, h2s = _merge(o_swa_s, o_gla_s, zs, xs, gn, pswa, pgla, wo, n2, tiles["merge_rows_s"])
    yp = _mlp(h2p, x1p, wup, wdn, fn, tiles["mlp_rows_p"], tiles["mlp_ff"])
    ys = _mlp(h2s, x1s, wup, wdn, fn, tiles["mlp_rows_s"], tiles["mlp_ff"])

    kv5 = lambda a, nb: a.reshape(1, nb, w_buf, N_KV_HEADS, HEAD_DIM)
    return (yp.reshape(bp, tp, d), ys.reshape(bs, ts, d),
            kv5(k_last, bp), kv5(v_last, bp), s_p[None],
            pos_major(nk_s), pos_major(nv_s), s_s[None])
```

```python
import functools

import jax
import jax.numpy as jnp
import numpy as np
from jax import lax
from jax.experimental import pallas as pl
from jax.experimental.pallas import tpu as pltpu

F32 = jnp.float32
BF16 = jnp.bfloat16

D_MODEL = 2048
PAST_LEN = 8192
N_HEADS = 16
N_KV_HEADS = 4
GROUP = N_HEADS // N_KV_HEADS
HEAD_DIM = 64
WINDOW = 128
ROT_DIM = HEAD_DIM // 4
ROPE_THETA = 500000.0
SWA_Q = N_HEADS * HEAD_DIM
SWA_KV = N_KV_HEADS * HEAD_DIM
GLA_HEADS = 4
GLA_DK = 256
GLA_DV = 512
GLA_DK_TOTAL = GLA_HEADS * GLA_DK
GLA_DV_TOTAL = GLA_HEADS * GLA_DV
GLA_GATE_RANK = 16
GLA_GATE_NORM = 16.0
D_FF = 4 * D_MODEL
EPS = 1e-6
LOG2E = 1.4426950408889634

LANES = 128
SUBLANES = 8
VMEM_LIMIT = 56 * 1024 * 1024

OFF_GS = 0
OFF_GG = OFF_GS + D_MODEL
OFF_VG = OFF_GG + D_MODEL
OFF_RG = OFF_VG + GLA_DV_TOTAL
OFF_QG = OFF_RG + GLA_DV_TOTAL
OFF_KG = OFF_QG + GLA_DK_TOTAL
OFF_QS = OFF_KG + GLA_DK_TOTAL
OFF_KS = OFF_QS + SWA_Q
OFF_VS = OFF_KS + SWA_KV
Z_COLS = OFF_VS + SWA_KV

GLA_BLOCK = 16
GLA_TILE = 256


def _cparams(sem):
    return pltpu.CompilerParams(dimension_semantics=sem, vmem_limit_bytes=VMEM_LIMIT)


def _rms(x, w):
    return x * lax.rsqrt(jnp.mean(x * x, axis=-1, keepdims=True) + EPS) * w


def _sigmoid(x):
    return 1.0 / (1.0 + jnp.exp(-x))


def _dot(a, b):
    return jnp.dot(a, b, preferred_element_type=F32)


def _dot_nt(a, b):
    return lax.dot_general(a, b, (((1,), (1,)), ((), ())), preferred_element_type=F32)


def _dot_tn(a, b):
    return lax.dot_general(a, b, (((0,), (0,)), ((), ())), preferred_element_type=F32)


def _split_bf16(x):
    hi = x.astype(BF16)
    lo = (x - hi.astype(F32)).astype(BF16)
    return hi, lo


def _in_proj_kernel(x_ref, nw_ref, w_ref, wag_ref, z_ref, zag_ref):
    h = _rms(x_ref[...], nw_ref[...]).astype(BF16)
    z_ref[...] = _dot_nt(h, w_ref[...]).astype(z_ref.dtype)

    @pl.when(pl.program_id(0) == 0)
    def _():
        zag_ref[...] = _dot_nt(h, wag_ref[...])


def _in_proj(x2d, norm1, w_main, w_ag, tm, tn, z_dtype):
    n = x2d.shape[0]
    nrow = n // tm
    w_mode = {"pipeline_mode": pl.Buffered(1)}
    return pl.pallas_call(
        _in_proj_kernel,
        grid=(Z_COLS // tn, nrow),
        in_specs=[
            pl.BlockSpec((tm, D_MODEL), lambda j, i: (i, 0)),
            pl.BlockSpec((1, D_MODEL), lambda j, i: (0, 0)),
            pl.BlockSpec((tn, D_MODEL), lambda j, i: (j, 0), **w_mode),
            pl.BlockSpec((LANES, D_MODEL), lambda j, i: (0, 0)),
        ],
        out_specs=[
            pl.BlockSpec((tm, tn), lambda j, i: (i, j)),
            pl.BlockSpec((tm, LANES), lambda j, i: (jnp.where(j == 0, i, nrow - 1), 0)),
        ],
        out_shape=[
            jax.ShapeDtypeStruct((n, Z_COLS), z_dtype),
            jax.ShapeDtypeStruct((n, LANES), F32),
        ],
        compiler_params=_cparams(("arbitrary", "arbitrary")),
        name="in_proj",
    )(x2d, norm1, w_main, w_ag)


def _rope_tables(pos):
    half = ROT_DIM // 2
    inv = ROPE_THETA ** (-jnp.arange(half, dtype=F32) * 2.0 / ROT_DIM)
    ang = pos.astype(F32)[:, None] * inv[None, :]
    cos, sin = jnp.cos(ang), jnp.sin(ang)
    t = pos.shape[0]
    ones = jnp.ones((t, HEAD_DIM - ROT_DIM), F32)
    zeros = jnp.zeros((t, HEAD_DIM - ROT_DIM), F32)
    zh = jnp.zeros((t, half), F32)
    c = jnp.concatenate([cos, cos, ones], axis=1)
    s_lo = jnp.concatenate([zh, sin, zeros], axis=1)
    s_hi = jnp.concatenate([-sin, zh, zeros], axis=1)
    rep = LANES // HEAD_DIM
    return tuple(jnp.tile(a, (1, rep)) for a in (c, s_lo, s_hi))


def _rope(x, c, s_lo, s_hi):
    half = ROT_DIM // 2
    outs = []
    for j in range(x.shape[1] // LANES):
        xc = x[:, j * LANES:(j + 1) * LANES]
        outs.append(xc * c + pltpu.roll(xc, half, 1) * s_lo + pltpu.roll(xc, LANES - half, 1) * s_hi)
    return outs[0] if len(outs) == 1 else jnp.concatenate(outs, axis=1)


def _swa_block(has_prev, q_blk, k_blk, v_blk, tabs, sink_ref, k_prev, v_prev, o_ref, row0):
    w = WINDOW
    kvw = SWA_KV
    c, s_lo, s_hi = tabs
    q = (_rope(q_blk.astype(F32), c, s_lo, s_hi) * (HEAD_DIM ** -0.5 * LOG2E)).astype(BF16)
    k = _rope(k_blk.astype(F32), c, s_lo, s_hi)
    v = v_blk.astype(F32)
    kc = jnp.concatenate([k_prev, k], axis=0).astype(BF16)
    vc = jnp.concatenate([v_prev, v], axis=0).astype(BF16)

    rows = lax.broadcasted_iota(jnp.int32, (w, 2 * w), 0)
    cols = lax.broadcasted_iota(jnp.int32, (w, 2 * w), 1)
    diff = rows + w - cols
    bias = jnp.where((diff >= 0) & (diff < w) & ((cols >= w) | has_prev), 0.0, -jnp.inf)
    bias = jnp.concatenate([bias] * GROUP, axis=0)
    head_shift = HEAD_DIM.bit_length() - 1
    lane_head_q = lax.broadcasted_iota(jnp.int32, (w, kvw), 1) >> head_shift
    lane_head = lax.broadcasted_iota(jnp.int32, (2 * w, kvw), 1) >> head_shift

    for g in range(N_KV_HEADS):
        in_g_q = jnp.where(lane_head_q == g, 1.0, 0.0).astype(BF16)
        in_g = jnp.where(lane_head == g, 1.0, 0.0).astype(BF16)
        qg = jnp.concatenate([q[:, h * kvw:(h + 1) * kvw] * in_g_q for h in range(GROUP)], axis=0)
        sk = jnp.concatenate([jnp.full((w, LANES), sink_ref[0, g * GROUP + h] * LOG2E, F32)
                              for h in range(GROUP)], axis=0)
        s = _dot_nt(qg, kc) + bias
        m = jnp.maximum(jnp.broadcast_to(jnp.max(s, axis=-1, keepdims=True), sk.shape), sk)
        p = jnp.exp2(s - jnp.concatenate([m, m], axis=1)).astype(BF16)
        vg = vc * in_g + (1.0 - in_g)
        oa = _dot(p, vg)
        p_sink = jnp.exp2(sk - m)
        denom = pltpu.roll(oa, 2 * HEAD_DIM, 1) + jnp.concatenate([p_sink, p_sink], axis=1)
        o = (oa / denom).astype(o_ref.dtype)
        for h in range(GROUP):
            lo = g * HEAD_DIM
            o_ref[row0:row0 + w, h * kvw + lo:h * kvw + lo + HEAD_DIM] = o[h * w:(h + 1) * w, lo:lo + HEAD_DIM]
    return k, v


SAMPLE_T = 4
PAIR_ROWS = 2 * SAMPLE_T


def _swa_sample_kernel(q_ref, k_ref, v_ref, ck_ref, cv_ref, c_ref, slo_ref, shi_ref, sink_ref,
                       o_ref, nk_ref, nv_ref, *, pairs):
    w = WINDOW
    kvw = SWA_KV
    c, s_lo, s_hi = c_ref[...], slo_ref[...], shi_ref[...]
    nrow = N_HEADS * PAIR_ROWS
    row = lax.broadcasted_iota(jnp.int32, (nrow, w), 0)
    col = lax.broadcasted_iota(jnp.int32, (nrow, w), 1)
    t_shift = SAMPLE_T.bit_length() - 1
    sq = row & (SAMPLE_T - 1)
    par = (row >> t_shift) & 1
    mask_cache = col > sq
    mask_new = (col < PAIR_ROWS) & ((col >> t_shift) == par) & ((col & (SAMPLE_T - 1)) <= sq)
    lane = lax.broadcasted_iota(jnp.int32, (PAIR_ROWS, kvw), 1) >> (HEAD_DIM.bit_length() - 1)
    pos = lax.broadcasted_iota(jnp.int32, (kvw, w), 1)
    sk = sink_ref[...][:, :1]
    zpad = jnp.zeros((w - PAIR_ROWS, kvw), F32)

    for p in range(pairs):
        rs = slice(p * PAIR_ROWS, (p + 1) * PAIR_ROWS)
        q8 = _rope(q_ref[rs, :], c, s_lo, s_hi) * (HEAD_DIM ** -0.5)
        k8 = _rope(k_ref[rs, :], c, s_lo, s_hi)
        v8 = v_ref[rs, :]
        blocks = [jnp.where(lane == g, q8[:, h * kvw:(h + 1) * kvw], 0.0)
                  for g in range(N_KV_HEADS) for h in range(GROUP)]
        qall = jnp.concatenate(blocks, axis=0).astype(BF16)

        k_new = jnp.concatenate([k8, zpad], axis=0)
        v_new = jnp.concatenate([v8, zpad], axis=0)
        s_cache = [_dot(qall, ck_ref[2 * p + e].astype(BF16)) for e in range(2)]
        s_cache = jnp.where(mask_cache, jnp.where(par == 1, s_cache[1], s_cache[0]), -jnp.inf)
        s_new = jnp.where(mask_new, _dot_nt(qall, k_new.astype(BF16)), -jnp.inf)
        s = jnp.concatenate([s_cache, s_new], axis=1)
        m = jnp.maximum(jnp.max(s, axis=-1, keepdims=True), sk)
        pr = jnp.exp(s - m)
        denom = jnp.sum(pr, axis=-1, keepdims=True) + jnp.exp(sk - m)
        p_cache, p_new = pr[:, :w], pr[:, w:]
        p0 = jnp.where(par == 0, p_cache, 0.0).astype(BF16)
        p1 = jnp.where(par == 1, p_cache, 0.0).astype(BF16)
        oall = (_dot_nt(p0, cv_ref[2 * p].astype(BF16)) + _dot_nt(p1, cv_ref[2 * p + 1].astype(BF16))
                + _dot(p_new.astype(BF16), v_new.astype(BF16))) / denom

        outs = []
        for h in range(GROUP):
            acc = None
            for g in range(N_KV_HEADS):
                hh = g * GROUP + h
                blk = jnp.where(lane == g, oall[hh * PAIR_ROWS:(hh + 1) * PAIR_ROWS], 0.0)
                acc = blk if acc is None else acc + blk
            outs.append(acc)
        o_ref[rs, :] = jnp.concatenate(outs, axis=1).astype(o_ref.dtype)

        for new, cref, nref in ((k_new, ck_ref, nk_ref), (v_new, cv_ref, nv_ref)):
            new_t = new.T
            for e in range(2):
                bd = 2 * p + e
                shifted = pltpu.roll(cref[bd], w - SAMPLE_T, 1)
                tail = pltpu.roll(new_t, w - SAMPLE_T - e * SAMPLE_T, 1)
                nref[bd] = jnp.where(pos >= w - SAMPLE_T, tail, shifted)


def _swa_sample(zs, cache_k, cache_v, sink_rows, tables, pairs):
    n = zs.shape[0]
    bd, kvw, w = cache_k.shape
    rows = pairs * PAIR_ROWS
    tab_spec = pl.BlockSpec((PAIR_ROWS, LANES), lambda i: (0, 0))
    cache_spec = pl.BlockSpec((2 * pairs, kvw, w), lambda i: (i, 0, 0))
    return pl.pallas_call(
        functools.partial(_swa_sample_kernel, pairs=pairs),
        grid=(n // rows,),
        in_specs=[
            pl.BlockSpec((rows, SWA_Q), lambda i: (i, OFF_QS // SWA_Q)),
            pl.BlockSpec((rows, SWA_KV), lambda i: (i, OFF_KS // SWA_KV)),
            pl.BlockSpec((rows, SWA_KV), lambda i: (i, OFF_VS // SWA_KV)),
            cache_spec, cache_spec,
            tab_spec, tab_spec, tab_spec,
            pl.BlockSpec((N_HEADS * PAIR_ROWS, LANES), lambda i: (0, 0)),
        ],
        out_specs=[
            pl.BlockSpec((rows, SWA_Q), lambda i: (i, 0)),
            cache_spec, cache_spec,
        ],
        out_shape=[
            jax.ShapeDtypeStruct((n, SWA_Q), BF16),
            jax.ShapeDtypeStruct((bd, kvw, w), F32),
            jax.ShapeDtypeStruct((bd, kvw, w), F32),
        ],
        compiler_params=_cparams(("arbitrary",)),
        name="swa_sample",
    )(zs, zs, zs, cache_k, cache_v, *tables, sink_rows)


def _log_decay(ag, wa2, ba):
    x = _dot(ag.astype(BF16), wa2) + ba
    log_sig = jnp.minimum(x, 0.0) - jnp.log(1.0 + jnp.exp(-jnp.abs(x)))
    return log_sig * (1.0 / GLA_GATE_NORM)


def _gla_out(o, gnorm, rg):
    dv = o.shape[-1]
    ms = jnp.broadcast_to(jnp.sum(o * o, axis=-1, keepdims=True), (o.shape[0], LANES)) * (1.0 / dv)
    r = lax.rsqrt(ms + EPS)
    half = rg * 0.5
    gate = half + half * jnp.tanh(half)
    return o * jnp.concatenate([r] * (dv // LANES), axis=1) * gnorm * gate


def _gla_constants(tt):
    nb = tt // GLA_BLOCK
    halves = [tt >> (l + 1) for l in range(nb.bit_length() - 1)]
    step = np.arange(tt)
    blk = step // GLA_BLOCK
    t16 = ((blk[:, None] == blk[None, :]) & (step[None, :] <= step[:, None])).astype(np.float32)
    rows = [(blk[None, :] < np.arange(nb)[:, None])]
    for hs in halves:
        t_b = (np.arange(nb) * GLA_BLOCK) // (2 * hs) * (2 * hs) + hs
        rows.append(step[None, :] < t_b[:, None])
    rows.append(np.ones((GLA_BLOCK, tt), bool))
    sel = np.concatenate(rows, axis=0).astype(np.float32)
    group = [(step[:, None] // (2 * hs)) == (step[None, :] // (2 * hs)) for hs in halves[1:]]
    mlev = np.stack(group).astype(np.float32)
    return (jnp.asarray(t16, BF16), jnp.asarray(sel, BF16), jnp.asarray(t16), jnp.asarray(mlev)), halves


def _gla_prompt_kernel(q_ref, k_ref, v_ref, ag_ref, wa2_ref, ba_ref,
                       t16_ref, sel_ref, mdiag_ref, mlev_ref, o_ref, sout_ref, st_ref, *, halves):
    t = pl.program_id(1)
    nt = pl.num_programs(1)
    tt = q_ref.shape[0]
    nb = tt // GLA_BLOCK
    nl = len(halves)
    dk, dv, width = GLA_DK, GLA_DV, GLA_DK_TOTAL

    @pl.when(t == 0)
    def _():
        st_ref[...] = jnp.zeros_like(st_ref)

    la = _log_decay(ag_ref[...], wa2_ref[...], ba_ref[...]) * LOG2E
    hi, lo = _split_bf16(la)
    c = _dot(t16_ref[...], hi) + _dot(t16_ref[...], lo)
    cum = _dot(sel_ref[...], hi) + _dot(sel_ref[...], lo)
    p_start = cum[:nb]
    total = cum[(1 + nl) * nb:(1 + nl) * nb + 1]
    exp_p = jnp.exp2(p_start)
    d_last = total - p_start
    after, before = [], []
    for l in range(nl):
        d = p_start - cum[(1 + l) * nb:(2 + l) * nb]
        after.append(jnp.exp2(jnp.minimum(d, 0.0)))
        before.append(-d)

    zeros = jnp.zeros((GLA_BLOCK, dk), BF16)
    names = ["qs", "kinv", "qt", "kd"] + [f"q{l}" for l in range(nl)] + [f"k{l}" for l in range(nl)]

    def scaled_operands(h):
        ks = slice(h * dk, (h + 1) * dk)
        parts = {nm: [] for nm in names}
        for i in range(nb):
            rs = slice(i * GLA_BLOCK, (i + 1) * GLA_BLOCK)
            row = slice(i, i + 1)
            c_b = c[rs, ks]
            k_b = k_ref[rs, ks].astype(F32)
            qs = q_ref[rs, ks].astype(F32) * (dk ** -0.5) * jnp.exp2(c_b)
            parts["qs"].append(qs.astype(BF16))
            parts["kinv"].append((k_b * jnp.exp2(-c_b)).astype(BF16))
            parts["qt"].append((qs * exp_p[row, ks]).astype(BF16))
            parts["kd"].append((k_b * jnp.exp2(d_last[row, ks] - c_b)).astype(BF16))
            for l, hs in enumerate(halves):
                if (i * GLA_BLOCK) % (2 * hs) >= hs:
                    parts[f"q{l}"].append((qs * after[l][row, ks]).astype(BF16))
                    parts[f"k{l}"].append(zeros)
                else:
                    parts[f"q{l}"].append(zeros)
                    parts[f"k{l}"].append((k_b * jnp.exp2(before[l][row, ks] - c_b)).astype(BF16))
        return {nm: jnp.concatenate(parts[nm], axis=0) for nm in names}

    in_block = mdiag_ref[...] != 0.0
    ops = scaled_operands(0)
    for h in range(GLA_HEADS):
        ks = slice(h * dk, (h + 1) * dk)
        vs = slice(h * dv, (h + 1) * dv)
        scores = [_dot_nt(ops["qs"], ops["kinv"])] + [_dot_nt(ops[f"q{l}"], ops[f"k{l}"]) for l in range(nl)]
        v_h = v_ref[:, vs]
        st = st_ref[h]
        o_state = _dot_nt(ops["qt"], st.astype(BF16))
        st_new = jnp.exp2(total[:, ks]) * st + _dot_tn(v_h, ops["kd"])
        if h + 1 < GLA_HEADS:
            ops = scaled_operands(h + 1)
        a = jnp.where(in_block, scores[0], 0.0) + scores[1]
        for l in range(1, nl):
            a = a + jnp.where(mlev_ref[l - 1] != 0.0, scores[l + 1], 0.0)
        st_ref[h] = st_new
        o_ref[:, vs] = (_dot(a.astype(BF16), v_h) + o_state).astype(o_ref.dtype)

    @pl.when(t == nt - 1)
    def _():
        for h in range(GLA_HEADS):
            sout_ref[h] = st_ref[h].T


def _gla_prompt(z3, zag3, wa2, ba, tt):
    b, t, _ = z3.shape
    h = GLA_HEADS
    consts, halves = _gla_constants(tt)
    const_specs = [pl.BlockSpec(c.shape, lambda i, s, nd=c.ndim: (0,) * nd) for c in consts]
    return pl.pallas_call(
        functools.partial(_gla_prompt_kernel, halves=tuple(halves)),
        grid=(b, t // tt),
        in_specs=[
            pl.BlockSpec((None, tt, GLA_DK_TOTAL), lambda i, s: (i, s, OFF_QG // GLA_DK_TOTAL)),
            pl.BlockSpec((None, tt, GLA_DK_TOTAL), lambda i, s: (i, s, OFF_KG // GLA_DK_TOTAL)),
            pl.BlockSpec((None, tt, GLA_DV_TOTAL), lambda i, s: (i, s, OFF_VG // GLA_DV_TOTAL)),
            pl.BlockSpec((None, tt, LANES), lambda i, s: (i, s, 0)),
            pl.BlockSpec((LANES, GLA_DK_TOTAL), lambda i, s: (0, 0)),
            pl.BlockSpec((1, GLA_DK_TOTAL), lambda i, s: (0, 0)),
            *const_specs,
        ],
        out_specs=[
            pl.BlockSpec((None, tt, GLA_DV_TOTAL), lambda i, s: (i, s, 0)),
            pl.BlockSpec((None, h, GLA_DK, GLA_DV), lambda i, s: (i, 0, 0, 0)),
        ],
        out_shape=[
            jax.ShapeDtypeStruct((b, t, GLA_DV_TOTAL), BF16),
            jax.ShapeDtypeStruct((b, h, GLA_DK, GLA_DV), F32),
        ],
        scratch_shapes=[pltpu.VMEM((h, GLA_DV, GLA_DK), F32)],
        compiler_params=_cparams(("arbitrary", "arbitrary")),
        name="gla_prompt",
    )(z3, z3, z3, zag3, wa2, ba, *consts)


def _gla_sample_kernel(q_ref, k_ref, v_ref, ag_ref, wa2_ref, ba_ref, s0_ref, o_ref, s1_ref):
    dk, dv = GLA_DK, GLA_DV
    r_k = lax.broadcasted_iota(jnp.int32, (PAIR_ROWS, dk), 0)
    r_v = lax.broadcasted_iota(jnp.int32, (PAIR_ROWS, dv), 0)
    step_k = r_k & (SAMPLE_T - 1)
    step_v = r_v & (SAMPLE_T - 1)
    odd_k = r_k >= SAMPLE_T
    odd_v = r_v >= SAMPLE_T
    la_all = _log_decay(ag_ref[...], wa2_ref[...], ba_ref[...])
    kpad = jnp.zeros((LANES - PAIR_ROWS, dk), F32)
    vpad = jnp.zeros((LANES - PAIR_ROWS, dv), BF16)
    ones = jnp.ones((LANES, LANES), BF16)

    for h in range(GLA_HEADS):
        la = la_all[:, h * dk:(h + 1) * dk]
        b = la + jnp.where(step_k >= 1, pltpu.roll(la, 1, 0), 0.0)
        b = b + jnp.where(step_k >= 2, pltpu.roll(b, 2, 0), 0.0)
        b_last = jnp.where(odd_k, b[PAIR_ROWS - 1:PAIR_ROWS, :], b[SAMPLE_T - 1:SAMPLE_T, :])
        q = q_ref[:, h * dk:(h + 1) * dk] * (dk ** -0.5)
        k = k_ref[:, h * dk:(h + 1) * dk]
        v = v_ref[:, h * dv:(h + 1) * dv]
        q_i = q * jnp.exp(b)
        k_i = k * jnp.exp(-b)
        k_d = k * jnp.exp(b_last - b)
        o = jnp.sum(q_i * k_i, axis=-1, keepdims=True) * v
        for d in range(1, SAMPLE_T):
            a_d = jnp.sum(q_i * pltpu.roll(k_i, d, 0), axis=-1, keepdims=True)
            o = o + jnp.where(step_v >= d, a_d * pltpu.roll(v, d, 0), 0.0)
        q_b = q_i.astype(BF16)
        vb = jnp.concatenate([v.astype(BF16), vpad], axis=0)
        o_par = []
        for e in range(2):
            s_old = s0_ref[e, h]
            o_par.append(_dot(q_b, s_old.astype(BF16)))
            sel = (r_k >= SAMPLE_T) if e == 1 else (r_k < SAMPLE_T)
            kd_e = jnp.concatenate([jnp.where(sel, k_d, 0.0), kpad], axis=0).astype(BF16)
            la_e = jnp.concatenate([jnp.where(sel, la, 0.0), kpad], axis=0)
            hi, lo = _split_bf16(la_e)
            decay = jnp.exp(_dot_tn(hi, ones) + _dot_tn(lo, ones))
            decay_full = jnp.concatenate([decay] * (dv // LANES), axis=1)
            s1_ref[e, h] = decay_full * s_old + _dot_tn(kd_e, vb)
        o = o + jnp.where(odd_v, o_par[1], o_par[0])
        o_ref[:, h * dv:(h + 1) * dv] = o


def _gla_sample(zs, zag, wa2, ba, state):
    n = zs.shape[0]
    bd, h, dk, dv = state.shape
    state_spec = pl.BlockSpec((2, h, dk, dv), lambda i: (i, 0, 0, 0))
    return pl.pallas_call(
        _gla_sample_kernel,
        grid=(n // PAIR_ROWS,),
        in_specs=[
            pl.BlockSpec((PAIR_ROWS, GLA_DK_TOTAL), lambda i: (i, OFF_QG // GLA_DK_TOTAL)),
            pl.BlockSpec((PAIR_ROWS, GLA_DK_TOTAL), lambda i: (i, OFF_KG // GLA_DK_TOTAL)),
            pl.BlockSpec((PAIR_ROWS, GLA_DV_TOTAL), lambda i: (i, OFF_VG // GLA_DV_TOTAL)),
            pl.BlockSpec((PAIR_ROWS, LANES), lambda i: (i, 0)),
            pl.BlockSpec((LANES, GLA_DK_TOTAL), lambda i: (0, 0)),
            pl.BlockSpec((1, GLA_DK_TOTAL), lambda i: (0, 0)),
            state_spec,
        ],
        out_specs=[
            pl.BlockSpec((PAIR_ROWS, GLA_DV_TOTAL), lambda i: (i, 0)),
            state_spec,
        ],
        out_shape=[
            jax.ShapeDtypeStruct((n, GLA_DV_TOTAL), F32),
            jax.ShapeDtypeStruct(state.shape, F32),
        ],
        compiler_params=_cparams(("arbitrary",)),
        name="gla_sample",
    )(zs, zs, zs, zag, wa2, ba, state)


def _merge_gla_branch(og_ref, rg_ref, gg_ref, gn_ref, pgla_ref):
    gn = gn_ref[...]
    og = jnp.concatenate(
        [_gla_out(og_ref[:, h * GLA_DV:(h + 1) * GLA_DV].astype(F32), gn,
                  rg_ref[:, h * GLA_DV:(h + 1) * GLA_DV].astype(F32)).astype(BF16)
         for h in range(GLA_HEADS)], axis=1)
    return _sigmoid(gg_ref[...].astype(F32)) * _dot(og, pgla_ref[...])


def _merge_finish(yb, osw_ref, gs_ref, x_ref, pswa_ref, wo_ref, n2_ref, x1_ref, h2_ref):
    y = _sigmoid(gs_ref[...].astype(F32)) * _dot(osw_ref[...], pswa_ref[...]) + yb
    x1 = x_ref[...] + _dot(y.astype(BF16), wo_ref[...])
    x1_ref[...] = x1
    h2_ref[...] = _rms(x1, n2_ref[...]).astype(BF16)


def _merge_kernel(osw_ref, og_ref, rg_ref, gs_ref, gg_ref, x_ref, gn_ref, pswa_ref, pgla_ref, wo_ref, n2_ref,
                  x1_ref, h2_ref):
    yb = _merge_gla_branch(og_ref, rg_ref, gg_ref, gn_ref, pgla_ref)
    _merge_finish(yb, osw_ref, gs_ref, x_ref, pswa_ref, wo_ref, n2_ref, x1_ref, h2_ref)


def _merge_swa_kernel(sink_ref, q_ref, k_ref, v_ref, c_ref, slo_ref, shi_ref, *refs, tiles_per_batch):
    (og_ref, rg_ref, gs_ref, gg_ref, x_ref, gn_ref, pswa_ref, pgla_ref, wo_ref, n2_ref, x1_ref, h2_ref,
     klast_ref, vlast_ref, kprev_ref, vprev_ref, osw_ref) = refs
    w = WINDOW
    t_local = lax.rem(pl.program_id(0), tiles_per_batch)

    @pl.when(t_local == 0)
    def _():
        kprev_ref[...] = jnp.zeros_like(kprev_ref)
        vprev_ref[...] = jnp.zeros_like(vprev_ref)

    yb = _merge_gla_branch(og_ref, rg_ref, gg_ref, gn_ref, pgla_ref)
    k_prev, v_prev = kprev_ref[...], vprev_ref[...]
    for j in range(q_ref.shape[0] // w):
        rs = slice(j * w, (j + 1) * w)
        tabs = (c_ref[rs, :], slo_ref[rs, :], shi_ref[rs, :])
        has_prev = (t_local > 0) if j == 0 else True
        k_prev, v_prev = _swa_block(has_prev, q_ref[rs, :], k_ref[rs, :], v_ref[rs, :], tabs, sink_ref,
                                    k_prev, v_prev, osw_ref, j * w)
    kprev_ref[...] = k_prev
    vprev_ref[...] = v_prev
    _merge_finish(yb, osw_ref, gs_ref, x_ref, pswa_ref, wo_ref, n2_ref, x1_ref, h2_ref)

    @pl.when(t_local == tiles_per_batch - 1)
    def _():
        klast_ref[...] = kprev_ref[...]
        vlast_ref[...] = vprev_ref[...]


def _merge_specs(tm):
    resident = functools.partial(pl.BlockSpec, pipeline_mode=pl.Buffered(1))
    in_specs = [
        pl.BlockSpec((tm, GLA_DV_TOTAL), lambda i: (i, 0)),
        pl.BlockSpec((tm, GLA_DV_TOTAL), lambda i: (i, OFF_RG // GLA_DV_TOTAL)),
        pl.BlockSpec((tm, D_MODEL), lambda i: (i, OFF_GS // D_MODEL)),
        pl.BlockSpec((tm, D_MODEL), lambda i: (i, OFF_GG // D_MODEL)),
        pl.BlockSpec((tm, D_MODEL), lambda i: (i, 0)),
        pl.BlockSpec((1, GLA_DV), lambda i: (0, 0)),
        resident((SWA_Q, D_MODEL), lambda i: (0, 0)),
        resident((GLA_DV_TOTAL, D_MODEL), lambda i: (0, 0)),
        resident((D_MODEL, D_MODEL), lambda i: (0, 0)),
        pl.BlockSpec((1, D_MODEL), lambda i: (0, 0)),
    ]
    out_specs = [pl.BlockSpec((tm, D_MODEL), lambda i: (i, 0)), pl.BlockSpec((tm, D_MODEL), lambda i: (i, 0))]
    return in_specs, out_specs


def _merge_swa(z, o_gla, x2d, gnorm, p_swa, p_gla, w_o, norm2, sink, tables, batch, tm):
    n = x2d.shape[0]
    w = WINDOW
    tiles_per_batch = n // batch // tm
    in_specs, out_specs = _merge_specs(tm)
    tab_spec = pl.BlockSpec((tm, LANES), lambda i: (lax.rem(i, tiles_per_batch), 0))
    last_spec = pl.BlockSpec((None, w, SWA_KV), lambda i: (i // tiles_per_batch, 0, 0))
    return pl.pallas_call(
        functools.partial(_merge_swa_kernel, tiles_per_batch=tiles_per_batch),
        grid=(n // tm,),
        in_specs=[
            pl.BlockSpec(memory_space=pltpu.SMEM),
            pl.BlockSpec((tm, SWA_Q), lambda i: (i, OFF_QS // SWA_Q)),
            pl.BlockSpec((tm, SWA_KV), lambda i: (i, OFF_KS // SWA_KV)),
            pl.BlockSpec((tm, SWA_KV), lambda i: (i, OFF_VS // SWA_KV)),
            tab_spec, tab_spec, tab_spec,
        ] + in_specs,
        out_specs=out_specs + [last_spec, last_spec],
        out_shape=[
            jax.ShapeDtypeStruct((n, D_MODEL), F32),
            jax.ShapeDtypeStruct((n, D_MODEL), BF16),
            jax.ShapeDtypeStruct((batch, w, SWA_KV), F32),
            jax.ShapeDtypeStruct((batch, w, SWA_KV), F32),
        ],
        scratch_shapes=[pltpu.VMEM((w, SWA_KV), F32), pltpu.VMEM((w, SWA_KV), F32),
                        pltpu.VMEM((tm, SWA_Q), BF16)],
        compiler_params=_cparams(("arbitrary",)),
        name="merge_swa",
    )(sink, z, z, z, *tables, o_gla, z, z, z, x2d, gnorm, p_swa, p_gla, w_o, norm2)


def _merge(o_swa, o_gla, z, x2d, gnorm, p_swa, p_gla, w_o, norm2, tm):
    n = x2d.shape[0]
    in_specs, out_specs = _merge_specs(tm)
    return pl.pallas_call(
        _merge_kernel,
        grid=(n // tm,),
        in_specs=[pl.BlockSpec((tm, SWA_Q), lambda i: (i, 0))] + in_specs,
        out_specs=out_specs,
        out_shape=[
            jax.ShapeDtypeStruct((n, D_MODEL), F32),
            jax.ShapeDtypeStruct((n, D_MODEL), BF16),
        ],
        compiler_params=_cparams(("arbitrary",)),
        name="merge",
    )(o_swa, o_gla, z, z, z, x2d, gnorm, p_swa, p_gla, w_o, norm2)


def _mlp_step(h2_ref, x1_ref, wup, wdn, fn_ref, out_ref, acc_ref):
    f = pl.program_id(1)
    nf = pl.num_programs(1)

    @pl.when(f == 0)
    def _():
        acc_ref[...] = jnp.zeros_like(acc_ref)

    u = _dot(h2_ref[...], wup)
    u = jnp.square(jnp.maximum(u, 0.0)).astype(BF16)
    acc_ref[...] += _dot(u, wdn)

    @pl.when(f == nf - 1)
    def _():
        out_ref[...] = _rms(x1_ref[...] + acc_ref[...], fn_ref[...])


def _mlp_kernel(h2_ref, x1_ref, wup_ref, wdn_ref, fn_ref, out_ref, acc_ref):
    _mlp_step(h2_ref, x1_ref, wup_ref[...], wdn_ref[...], fn_ref, out_ref, acc_ref)


def _mlp_cast_kernel(h2_ref, x1_ref, wup_ref, wdn_ref, fn_ref, out_ref, wup_bf_ref, wdn_bf_ref, acc_ref):
    wup = wup_ref[...].astype(BF16)
    wdn = wdn_ref[...].astype(BF16)
    wup_bf_ref[...] = wup
    wdn_bf_ref[...] = wdn
    _mlp_step(h2_ref, x1_ref, wup, wdn, fn_ref, out_ref, acc_ref)


def _mlp_cast(h2, x1, w_up, w_down, final_norm, tf):
    n = h2.shape[0]
    return pl.pallas_call(
        _mlp_cast_kernel,
        grid=(1, D_FF // tf),
        in_specs=[
            pl.BlockSpec((n, D_MODEL), lambda i, f: (0, 0)),
            pl.BlockSpec((n, D_MODEL), lambda i, f: (0, 0)),
            pl.BlockSpec((None, D_MODEL, tf), lambda i, f: (0, 0, f)),
            pl.BlockSpec((None, tf, D_MODEL), lambda i, f: (0, f, 0)),
            pl.BlockSpec((1, D_MODEL), lambda i, f: (0, 0)),
        ],
        out_specs=[
            pl.BlockSpec((n, D_MODEL), lambda i, f: (0, 0)),
            pl.BlockSpec((D_MODEL, tf), lambda i, f: (0, f)),
            pl.BlockSpec((tf, D_MODEL), lambda i, f: (f, 0)),
        ],
        out_shape=[
            jax.ShapeDtypeStruct((n, D_MODEL), F32),
            jax.ShapeDtypeStruct((D_MODEL, D_FF), BF16),
            jax.ShapeDtypeStruct((D_FF, D_MODEL), BF16),
        ],
        scratch_shapes=[pltpu.VMEM((n, D_MODEL), F32)],
        compiler_params=_cparams(("arbitrary", "arbitrary")),
        name="mlp_cast",
    )(h2, x1, w_up, w_down, final_norm)


def _mlp(h2, x1, w_up, w_down, final_norm, tm, tf):
    n = h2.shape[0]
    return pl.pallas_call(
        _mlp_kernel,
        grid=(n // tm, D_FF // tf),
        in_specs=[
            pl.BlockSpec((tm, D_MODEL), lambda i, f: (i, 0)),
            pl.BlockSpec((tm, D_MODEL), lambda i, f: (i, 0)),
            pl.BlockSpec((D_MODEL, tf), lambda i, f: (0, f)),
            pl.BlockSpec((tf, D_MODEL), lambda i, f: (f, 0)),
            pl.BlockSpec((1, D_MODEL), lambda i, f: (0, 0)),
        ],
        out_specs=pl.BlockSpec((tm, D_MODEL), lambda i, f: (i, 0)),
        out_shape=jax.ShapeDtypeStruct((n, D_MODEL), F32),
        scratch_shapes=[pltpu.VMEM((tm, D_MODEL), F32)],
        compiler_params=_cparams(("arbitrary", "arbitrary")),
        name="mlp",
    )(h2, x1, w_up, w_down, final_norm)


def _pick_tile(n, pref):
    t = min(n, pref)
    while n % t:
        t //= 2
    return t


def _tile_plan(n_prompt, n_sample, t_prompt, b_sample):
    return {
        "proj_rows_p": _pick_tile(n_prompt, 512),
        "swa_pairs": _pick_tile(b_sample // 2, 4),
        "gla_rows": _pick_tile(t_prompt, GLA_TILE),
        "merge_rows_p": _pick_tile(n_prompt, 256), "merge_rows_s": _pick_tile(n_sample, 256),
        "mlp_rows_p": _pick_tile(n_prompt, 512),
        "mlp_ff": 1024, "mlp_cast_ff": 512,
    }


W_IN_SPLITS = (("qs", SWA_Q), ("ks", SWA_KV), ("vs", SWA_KV), ("qg", GLA_DK_TOTAL), ("kg", GLA_DK_TOTAL),
               ("vg", GLA_DV_TOTAL), ("rg", GLA_DV_TOTAL), ("ag", GLA_GATE_RANK), ("gs", D_MODEL), ("gg", D_MODEL))
W_IN_COLS = sum(width for _, width in W_IN_SPLITS)


REORDER_PIECE = HEAD_DIM
REORDER_PIECES = 4


def _w_in_row_table():
    src, o = {}, 0
    for name, width in W_IN_SPLITS:
        src[name] = o
        o += width
    rows = np.zeros(Z_COLS, np.int64)
    for name, dst in (("gs", OFF_GS), ("gg", OFF_GG), ("vg", OFF_VG), ("rg", OFF_RG), ("qg", OFF_QG),
                      ("kg", OFF_KG), ("ks", OFF_KS), ("vs", OFF_VS)):
        width = dict(W_IN_SPLITS)[name]
        rows[dst:dst + width] = src[name] + np.arange(width)
    for h in range(GROUP):
        for g in range(N_KV_HEADS):
            dst = OFF_QS + h * SWA_KV + g * HEAD_DIM
            rows[dst:dst + HEAD_DIM] = src["qs"] + (g * GROUP + h) * HEAD_DIM + np.arange(HEAD_DIM)
    pieces = rows.reshape(-1, REORDER_PIECE)
    assert (pieces == pieces[:, :1] + np.arange(REORDER_PIECE)).all()
    return jnp.asarray(pieces[:, 0], jnp.int32), src["ag"]


def _reorder_w_in_kernel(tbl_ref, *refs):
    piece_refs = refs[:REORDER_PIECES]
    ag_ref, x_ref, nw_ref, wm_ref, wag_ref, z_ref, zag_ref, h_ref = refs[REORDER_PIECES:]

    @pl.when(pl.program_id(0) == 0)
    def _():
        pad = jnp.zeros((LANES - GLA_GATE_RANK, wag_ref.shape[1]), BF16)
        wag = jnp.concatenate([ag_ref[0].astype(BF16), pad], axis=0)
        wag_ref[...] = wag
        h = _rms(x_ref[...], nw_ref[...]).astype(BF16)
        h_ref[...] = h
        zag_ref[...] = _dot_nt(h, wag)

    tile = jnp.concatenate([p[0].astype(BF16) for p in piece_refs], axis=0)
    wm_ref[...] = tile
    z_ref[...] = _dot_nt(h_ref[...], tile)


def _reorder_w_in(w_in_t, x_rows, norm1):
    d = w_in_t.shape[2]
    n = x_rows.shape[0]
    table, ag_row = _w_in_row_table()
    rows = REORDER_PIECE * REORDER_PIECES
    piece = lambda k: pl.BlockSpec((pl.Element(1), pl.Element(REORDER_PIECE), pl.Element(d)),
                                   lambda i, tbl: (0, pl.multiple_of(tbl[i * REORDER_PIECES + k], GLA_GATE_RANK), 0))
    grid_spec = pltpu.PrefetchScalarGridSpec(
        num_scalar_prefetch=1,
        grid=(Z_COLS // rows,),
        in_specs=[piece(k) for k in range(REORDER_PIECES)]
        + [pl.BlockSpec((pl.Element(1), pl.Element(GLA_GATE_RANK), pl.Element(d)),
                        lambda i, tbl: (0, ag_row, 0)),
           pl.BlockSpec((n, d), lambda i, tbl: (0, 0)),
           pl.BlockSpec((1, d), lambda i, tbl: (0, 0))],
        out_specs=[pl.BlockSpec((rows, d), lambda i, tbl: (i, 0)),
                   pl.BlockSpec((LANES, d), lambda i, tbl: (0, 0)),
                   pl.BlockSpec((n, rows), lambda i, tbl: (0, i)),
                   pl.BlockSpec((n, LANES), lambda i, tbl: (0, 0))],
        scratch_shapes=[pltpu.VMEM((n, d), BF16)],
    )
    return pl.pallas_call(
        _reorder_w_in_kernel,
        grid_spec=grid_spec,
        out_shape=[jax.ShapeDtypeStruct((Z_COLS, d), BF16), jax.ShapeDtypeStruct((LANES, d), BF16),
                   jax.ShapeDtypeStruct((n, Z_COLS), F32), jax.ShapeDtypeStruct((n, LANES), F32)],
        compiler_params=_cparams(("arbitrary",)),
        name="reorder_w_in",
    )(table, *([w_in_t] * (REORDER_PIECES + 1)), x_rows, norm1)


def kernel(x_prompt, x_sample, cache_swa_k, cache_swa_v, state_gla, norm1, w_in, w_a2, b_a, sink,
           gla_norm, p_swa, p_gla, w_o, norm2, w_up, w_down, final_norm):
    assert norm1.shape[0] == 1, "single-layer stack"
    bp, tp, d = x_prompt.shape
    bs, ts, _ = x_sample.shape
    assert ts == SAMPLE_T and bs % 2 == 0 and tp % WINDOW == 0
    w_buf = cache_swa_k.shape[2]
    assert w_buf == WINDOW

    wa2 = jnp.pad(w_a2[0], ((0, LANES - GLA_GATE_RANK), (0, 0))).astype(BF16)
    ba = b_a[0][None, :]
    n1, n2, fn = norm1[0][None, :], norm2[0][None, :], final_norm[None, :]
    gn = gla_norm[0][None, :]
    pswa = p_swa[0].reshape(N_KV_HEADS, GROUP, HEAD_DIM, d).transpose(1, 0, 2, 3).reshape(SWA_Q, d).astype(BF16)
    pgla, wo = p_gla[0].astype(BF16), w_o[0].astype(BF16)
    sink_smem = sink[0][None, :]
    sink_rows = jnp.broadcast_to(jnp.repeat(sink[0], PAIR_ROWS)[:, None], (N_HEADS * PAIR_ROWS, LANES))

    xp = x_prompt.reshape(bp * tp, d)
    xs = x_sample.reshape(bs * ts, d)
    np_, ns = xp.shape[0], xs.shape[0]

    tiles = _tile_plan(np_, ns, tp, bs)

    w_main, w_ag, zs, zags = _reorder_w_in(jnp.swapaxes(w_in, 1, 2), xs, n1)
    zp, zagp = _in_proj(xp, n1, w_main, w_ag, tiles["proj_rows_p"], Z_COLS // 2, BF16)

    zp3 = zp.reshape(bp, tp, Z_COLS)
    pos_s = PAST_LEN + jnp.arange(ts)
    tabs_s = _rope_tables(jnp.concatenate([pos_s, pos_s]))
    pos_minor = lambda c: jnp.transpose(c[0], (0, 2, 3, 1)).reshape(bs, SWA_KV, w_buf)
    pos_major = lambda c: jnp.transpose(c.reshape(bs, N_KV_HEADS, HEAD_DIM, w_buf), (0, 3, 1, 2))[None]
    o_swa_s, nk_s, nv_s = _swa_sample(zs, pos_minor(cache_swa_k), pos_minor(cache_swa_v), sink_rows, tabs_s,
                                      tiles["swa_pairs"])

    o_gla_p, s_p = _gla_prompt(zp3, zagp.reshape(bp, tp, LANES), wa2, ba, tiles["gla_rows"])
    o_gla_s, s_s = _gla_sample(zs, zags, wa2, ba, state_gla[0])

    x1p, h2p, k_last, v_last = _merge_swa(zp, o_gla_p.reshape(np_, GLA_DV_TOTAL), xp, gn, pswa, pgla, wo, n2,
                                          sink_smem, _rope_tables(jnp.arange(tp)), bp, tiles["merge_rows_p"])
    x1s, h2s = _merge(o_swa_s, o_gla_s, zs, xs, gn, pswa, pgla, wo, n2, tiles["merge_rows_s"])
    ys, wup, wdn = _mlp_cast(h2s, x1s, w_up, w_down, fn, tiles["mlp_cast_ff"])
    yp = _mlp(h2p, x1p, wup, wdn, fn, tiles["mlp_rows_p"], tiles["mlp_ff"])

    kv5 = lambda a, nb: a.reshape(1, nb, w_buf, N_KV_HEADS, HEAD_DIM)
    return (yp.reshape(bp, tp, d), ys.reshape(bs, ts, d),
            kv5(k_last, bp), kv5(v_last, bp), s_p[None],
            pos_major(nk_s), pos_major(nv_s), s_s[None])
```

```python
import functools

import jax
import jax.numpy as jnp
import numpy as np
from jax import lax
from jax.experimental import pallas as pl
from jax.experimental.pallas import tpu as pltpu

F32 = jnp.float32
BF16 = jnp.bfloat16

D_MODEL = 2048
PAST_LEN = 8192
N_HEADS = 16
N_KV_HEADS = 4
GROUP = N_HEADS // N_KV_HEADS
HEAD_DIM = 64
WINDOW = 128
ROT_DIM = HEAD_DIM // 4
ROPE_THETA = 500000.0
SWA_Q = N_HEADS * HEAD_DIM
SWA_KV = N_KV_HEADS * HEAD_DIM
GLA_HEADS = 4
GLA_DK = 256
GLA_DV = 512
GLA_DK_TOTAL = GLA_HEADS * GLA_DK
GLA_DV_TOTAL = GLA_HEADS * GLA_DV
GLA_GATE_RANK = 16
GLA_GATE_NORM = 16.0
D_FF = 4 * D_MODEL
EPS = 1e-6
LOG2E = 1.4426950408889634

LANES = 128
SUBLANES = 8
VMEM_LIMIT = 56 * 1024 * 1024

OFF_GS = 0
OFF_GG = OFF_GS + D_MODEL
OFF_VG = OFF_GG + D_MODEL
OFF_RG = OFF_VG + GLA_DV_TOTAL
OFF_QG = OFF_RG + GLA_DV_TOTAL
OFF_KG = OFF_QG + GLA_DK_TOTAL
OFF_QS = OFF_KG + GLA_DK_TOTAL
OFF_KS = OFF_QS + SWA_Q
OFF_VS = OFF_KS + SWA_KV
Z_COLS = OFF_VS + SWA_KV

GLA_BLOCK = 16
GLA_TILE = 256


def _cparams(sem):
    return pltpu.CompilerParams(dimension_semantics=sem, vmem_limit_bytes=VMEM_LIMIT)


def _rms(x, w):
    return x * lax.rsqrt(jnp.mean(x * x, axis=-1, keepdims=True) + EPS) * w


def _sigmoid(x):
    return 1.0 / (1.0 + jnp.exp(-x))


def _dot(a, b):
    return jnp.dot(a, b, preferred_element_type=F32)


def _dot_nt(a, b):
    return lax.dot_general(a, b, (((1,), (1,)), ((), ())), preferred_element_type=F32)


def _dot_tn(a, b):
    return lax.dot_general(a, b, (((0,), (0,)), ((), ())), preferred_element_type=F32)


def _split_bf16(x):
    hi = x.astype(BF16)
    lo = (x - hi.astype(F32)).astype(BF16)
    return hi, lo


def _in_proj_kernel(x_ref, nw_ref, w_ref, wag_ref, z_ref, zag_ref):
    h = _rms(x_ref[...], nw_ref[...]).astype(BF16)
    z_ref[...] = _dot_nt(h, w_ref[...]).astype(z_ref.dtype)

    @pl.when(pl.program_id(0) == 0)
    def _():
        zag_ref[...] = _dot_nt(h, wag_ref[...])


def _in_proj(x2d, norm1, w_main, w_ag, tm, tn, z_dtype):
    n = x2d.shape[0]
    nrow = n // tm
    w_mode = {"pipeline_mode": pl.Buffered(1)}
    return pl.pallas_call(
        _in_proj_kernel,
        grid=(Z_COLS // tn, nrow),
        in_specs=[
            pl.BlockSpec((tm, D_MODEL), lambda j, i: (i, 0)),
            pl.BlockSpec((1, D_MODEL), lambda j, i: (0, 0)),
            pl.BlockSpec((tn, D_MODEL), lambda j, i: (j, 0), **w_mode),
            pl.BlockSpec((LANES, D_MODEL), lambda j, i: (0, 0)),
        ],
        out_specs=[
            pl.BlockSpec((tm, tn), lambda j, i: (i, j)),
            pl.BlockSpec((tm, LANES), lambda j, i: (jnp.where(j == 0, i, nrow - 1), 0)),
        ],
        out_shape=[
            jax.ShapeDtypeStruct((n, Z_COLS), z_dtype),
            jax.ShapeDtypeStruct((n, LANES), F32),
        ],
        compiler_params=_cparams(("arbitrary", "arbitrary")),
        name="in_proj",
    )(x2d, norm1, w_main, w_ag)


def _rope_tables(pos):
    half = ROT_DIM // 2
    inv = ROPE_THETA ** (-jnp.arange(half, dtype=F32) * 2.0 / ROT_DIM)
    ang = pos.astype(F32)[:, None] * inv[None, :]
    cos, sin = jnp.cos(ang), jnp.sin(ang)
    t = pos.shape[0]
    ones = jnp.ones((t, HEAD_DIM - ROT_DIM), F32)
    zeros = jnp.zeros((t, HEAD_DIM - ROT_DIM), F32)
    zh = jnp.zeros((t, half), F32)
    c = jnp.concatenate([cos, cos, ones], axis=1)
    s_lo = jnp.concatenate([zh, sin, zeros], axis=1)
    s_hi = jnp.concatenate([-sin, zh, zeros], axis=1)
    rep = LANES // HEAD_DIM
    return tuple(jnp.tile(a, (1, rep)) for a in (c, s_lo, s_hi))


def _rope(x, c, s_lo, s_hi):
    half = ROT_DIM // 2
    outs = []
    for j in range(x.shape[1] // LANES):
        xc = x[:, j * LANES:(j + 1) * LANES]
        outs.append(xc * c + pltpu.roll(xc, half, 1) * s_lo + pltpu.roll(xc, LANES - half, 1) * s_hi)
    return outs[0] if len(outs) == 1 else jnp.concatenate(outs, axis=1)


def _swa_block(has_prev, q_blk, k_blk, v_blk, tabs, sink_ref, k_prev, v_prev, o_ref, row0):
    w = WINDOW
    kvw = SWA_KV
    c, s_lo, s_hi = tabs
    q = (_rope(q_blk.astype(F32), c, s_lo, s_hi) * (HEAD_DIM ** -0.5 * LOG2E)).astype(BF16)
    k = _rope(k_blk.astype(F32), c, s_lo, s_hi)
    v = v_blk.astype(F32)
    kc = jnp.concatenate([k_prev, k], axis=0).astype(BF16)
    vc = jnp.concatenate([v_prev, v], axis=0).astype(BF16)

    rows = lax.broadcasted_iota(jnp.int32, (w, 2 * w), 0)
    cols = lax.broadcasted_iota(jnp.int32, (w, 2 * w), 1)
    diff = rows + w - cols
    bias = jnp.where((diff >= 0) & (diff < w) & ((cols >= w) | has_prev), 0.0, -jnp.inf)
    bias = jnp.concatenate([bias] * GROUP, axis=0)
    head_shift = HEAD_DIM.bit_length() - 1
    lane_head_q = lax.broadcasted_iota(jnp.int32, (w, kvw), 1) >> head_shift
    lane_head = lax.broadcasted_iota(jnp.int32, (2 * w, kvw), 1) >> head_shift

    for g in range(N_KV_HEADS):
        in_g_q = jnp.where(lane_head_q == g, 1.0, 0.0).astype(BF16)
        in_g = jnp.where(lane_head == g, 1.0, 0.0).astype(BF16)
        qg = jnp.concatenate([q[:, h * kvw:(h + 1) * kvw] * in_g_q for h in range(GROUP)], axis=0)
        sk = jnp.concatenate([jnp.full((w, LANES), sink_ref[0, g * GROUP + h] * LOG2E, F32)
                              for h in range(GROUP)], axis=0)
        s = _dot_nt(qg, kc) + bias
        m = jnp.maximum(jnp.broadcast_to(jnp.max(s, axis=-1, keepdims=True), sk.shape), sk)
        p = jnp.exp2(s - jnp.concatenate([m, m], axis=1)).astype(BF16)
        vg = vc * in_g + (1.0 - in_g)
        oa = _dot(p, vg)
        p_sink = jnp.exp2(sk - m)
        denom = pltpu.roll(oa, 2 * HEAD_DIM, 1) + jnp.concatenate([p_sink, p_sink], axis=1)
        o = (oa / denom).astype(o_ref.dtype)
        for h in range(GROUP):
            lo = g * HEAD_DIM
            o_ref[row0:row0 + w, h * kvw + lo:h * kvw + lo + HEAD_DIM] = o[h * w:(h + 1) * w, lo:lo + HEAD_DIM]
    return k, v


SAMPLE_T = 4
PAIR_ROWS = 2 * SAMPLE_T


def _swa_sample_kernel(q_ref, k_ref, v_ref, ck_ref, cv_ref, c_ref, slo_ref, shi_ref, sink_ref,
                       o_ref, nk_ref, nv_ref, *, pairs):
    w = WINDOW
    kvw = SWA_KV
    c, s_lo, s_hi = c_ref[...], slo_ref[...], shi_ref[...]
    nrow = N_HEADS * PAIR_ROWS
    row = lax.broadcasted_iota(jnp.int32, (nrow, w), 0)
    col = lax.broadcasted_iota(jnp.int32, (nrow, w), 1)
    t_shift = SAMPLE_T.bit_length() - 1
    sq = row & (SAMPLE_T - 1)
    par = (row >> t_shift) & 1
    mask_cache = col > sq
    mask_new = (col < PAIR_ROWS) & ((col >> t_shift) == par) & ((col & (SAMPLE_T - 1)) <= sq)
    lane = lax.broadcasted_iota(jnp.int32, (PAIR_ROWS, kvw), 1) >> (HEAD_DIM.bit_length() - 1)
    pos = lax.broadcasted_iota(jnp.int32, (kvw, w), 1)
    sk = sink_ref[...][:, :1]
    zpad = jnp.zeros((w - PAIR_ROWS, kvw), F32)

    for p in range(pairs):
        rs = slice(p * PAIR_ROWS, (p + 1) * PAIR_ROWS)
        q8 = _rope(q_ref[rs, :], c, s_lo, s_hi) * (HEAD_DIM ** -0.5)
        k8 = _rope(k_ref[rs, :], c, s_lo, s_hi)
        v8 = v_ref[rs, :]
        blocks = [jnp.where(lane == g, q8[:, h * kvw:(h + 1) * kvw], 0.0)
                  for g in range(N_KV_HEADS) for h in range(GROUP)]
        qall = jnp.concatenate(blocks, axis=0).astype(BF16)

        k_new = jnp.concatenate([k8, zpad], axis=0)
        v_new = jnp.concatenate([v8, zpad], axis=0)
        s_cache = [_dot(qall, ck_ref[2 * p + e].astype(BF16)) for e in range(2)]
        s_cache = jnp.where(mask_cache, jnp.where(par == 1, s_cache[1], s_cache[0]), -jnp.inf)
        s_new = jnp.where(mask_new, _dot_nt(qall, k_new.astype(BF16)), -jnp.inf)
        s = jnp.concatenate([s_cache, s_new], axis=1)
        m = jnp.maximum(jnp.max(s, axis=-1, keepdims=True), sk)
        pr = jnp.exp(s - m)
        denom = jnp.sum(pr, axis=-1, keepdims=True) + jnp.exp(sk - m)
        p_cache, p_new = pr[:, :w], pr[:, w:]
        p0 = jnp.where(par == 0, p_cache, 0.0).astype(BF16)
        p1 = jnp.where(par == 1, p_cache, 0.0).astype(BF16)
        oall = (_dot_nt(p0, cv_ref[2 * p].astype(BF16)) + _dot_nt(p1, cv_ref[2 * p + 1].astype(BF16))
                + _dot(p_new.astype(BF16), v_new.astype(BF16))) / denom

        outs = []
        for h in range(GROUP):
            acc = None
            for g in range(N_KV_HEADS):
                hh = g * GROUP + h
                blk = jnp.where(lane == g, oall[hh * PAIR_ROWS:(hh + 1) * PAIR_ROWS], 0.0)
                acc = blk if acc is None else acc + blk
            outs.append(acc)
        o_ref[rs, :] = jnp.concatenate(outs, axis=1).astype(o_ref.dtype)

        for new, cref, nref in ((k_new, ck_ref, nk_ref), (v_new, cv_ref, nv_ref)):
            new_t = new.T
            for e in range(2):
                bd = 2 * p + e
                shifted = pltpu.roll(cref[bd], w - SAMPLE_T, 1)
                tail = pltpu.roll(new_t, w - SAMPLE_T - e * SAMPLE_T, 1)
                nref[bd] = jnp.where(pos >= w - SAMPLE_T, tail, shifted)


def _swa_sample(zs, cache_k, cache_v, sink_rows, tables, pairs):
    n = zs.shape[0]
    bd, kvw, w = cache_k.shape
    rows = pairs * PAIR_ROWS
    tab_spec = pl.BlockSpec((PAIR_ROWS, LANES), lambda i: (0, 0))
    cache_spec = pl.BlockSpec((2 * pairs, kvw, w), lambda i: (i, 0, 0))
    return pl.pallas_call(
        functools.partial(_swa_sample_kernel, pairs=pairs),
        grid=(n // rows,),
        in_specs=[
            pl.BlockSpec((rows, SWA_Q), lambda i: (i, OFF_QS // SWA_Q)),
            pl.BlockSpec((rows, SWA_KV), lambda i: (i, OFF_KS // SWA_KV)),
            pl.BlockSpec((rows, SWA_KV), lambda i: (i, OFF_VS // SWA_KV)),
            cache_spec, cache_spec,
            tab_spec, tab_spec, tab_spec,
            pl.BlockSpec((N_HEADS * PAIR_ROWS, LANES), lambda i: (0, 0)),
        ],
        out_specs=[
            pl.BlockSpec((rows, SWA_Q), lambda i: (i, 0)),
            cache_spec, cache_spec,
        ],
        out_shape=[
            jax.ShapeDtypeStruct((n, SWA_Q), BF16),
            jax.ShapeDtypeStruct((bd, kvw, w), F32),
            jax.ShapeDtypeStruct((bd, kvw, w), F32),
        ],
        compiler_params=_cparams(("arbitrary",)),
        name="swa_sample",
    )(zs, zs, zs, cache_k, cache_v, *tables, sink_rows)


def _log_decay(ag, wa2, ba):
    x = _dot(ag.astype(BF16), wa2) + ba
    log_sig = jnp.minimum(x, 0.0) - jnp.log(1.0 + jnp.exp(-jnp.abs(x)))
    return log_sig * (1.0 / GLA_GATE_NORM)


def _gla_out(o, gnorm, rg):
    dv = o.shape[-1]
    ms = jnp.broadcast_to(jnp.sum(o * o, axis=-1, keepdims=True), (o.shape[0], LANES)) * (1.0 / dv)
    r = lax.rsqrt(ms + EPS)
    half = rg * 0.5
    gate = half + half * jnp.tanh(half)
    return o * jnp.concatenate([r] * (dv // LANES), axis=1) * gnorm * gate


def _gla_constants(tt):
    nb = tt // GLA_BLOCK
    halves = [tt >> (l + 1) for l in range(nb.bit_length() - 1)]
    step = np.arange(tt)
    blk = step // GLA_BLOCK
    t16 = ((blk[:, None] == blk[None, :]) & (step[None, :] <= step[:, None])).astype(np.float32)
    rows = [(blk[None, :] < np.arange(nb)[:, None])]
    for hs in halves:
        t_b = (np.arange(nb) * GLA_BLOCK) // (2 * hs) * (2 * hs) + hs
        rows.append(step[None, :] < t_b[:, None])
    rows.append(np.ones((GLA_BLOCK, tt), bool))
    sel = np.concatenate(rows, axis=0).astype(np.float32)
    group = [(step[:, None] // (2 * hs)) == (step[None, :] // (2 * hs)) for hs in halves[1:]]
    mlev = np.stack(group).astype(np.float32)
    return (jnp.asarray(t16, BF16), jnp.asarray(sel, BF16), jnp.asarray(t16), jnp.asarray(mlev)), halves


def _gla_prompt_kernel(q_ref, k_ref, v_ref, ag_ref, wa2_ref, ba_ref,
                       t16_ref, sel_ref, mdiag_ref, mlev_ref, o_ref, sout_ref, st_ref, *, halves):
    t = pl.program_id(1)
    nt = pl.num_programs(1)
    tt = q_ref.shape[0]
    nb = tt // GLA_BLOCK
    nl = len(halves)
    dk, dv, width = GLA_DK, GLA_DV, GLA_DK_TOTAL

    @pl.when(t == 0)
    def _():
        st_ref[...] = jnp.zeros_like(st_ref)

    la = _log_decay(ag_ref[...], wa2_ref[...], ba_ref[...]) * LOG2E
    hi, lo = _split_bf16(la)
    c = _dot(t16_ref[...], hi) + _dot(t16_ref[...], lo)
    cum = _dot(sel_ref[...], hi) + _dot(sel_ref[...], lo)
    p_start = cum[:nb]
    total = cum[(1 + nl) * nb:(1 + nl) * nb + 1]
    exp_p = jnp.exp2(p_start)
    d_last = total - p_start
    after, before = [], []
    for l in range(nl):
        d = p_start - cum[(1 + l) * nb:(2 + l) * nb]
        after.append(jnp.exp2(jnp.minimum(d, 0.0)))
        before.append(-d)

    zeros = jnp.zeros((GLA_BLOCK, dk), BF16)
    names = ["qs", "kinv", "qt", "kd"] + [f"q{l}" for l in range(nl)] + [f"k{l}" for l in range(nl)]

    def scaled_operands(h):
        ks = slice(h * dk, (h + 1) * dk)
        parts = {nm: [] for nm in names}
        for i in range(nb):
            rs = slice(i * GLA_BLOCK, (i + 1) * GLA_BLOCK)
            row = slice(i, i + 1)
            c_b = c[rs, ks]
            k_b = k_ref[rs, ks].astype(F32)
            qs = q_ref[rs, ks].astype(F32) * (dk ** -0.5) * jnp.exp2(c_b)
            parts["qs"].append(qs.astype(BF16))
            parts["kinv"].append((k_b * jnp.exp2(-c_b)).astype(BF16))
            parts["qt"].append((qs * exp_p[row, ks]).astype(BF16))
            parts["kd"].append((k_b * jnp.exp2(d_last[row, ks] - c_b)).astype(BF16))
            for l, hs in enumerate(halves):
                if (i * GLA_BLOCK) % (2 * hs) >= hs:
                    parts[f"q{l}"].append((qs * after[l][row, ks]).astype(BF16))
                    parts[f"k{l}"].append(zeros)
                else:
                    parts[f"q{l}"].append(zeros)
                    parts[f"k{l}"].append((k_b * jnp.exp2(before[l][row, ks] - c_b)).astype(BF16))
        return {nm: jnp.concatenate(parts[nm], axis=0) for nm in names}

    in_block = mdiag_ref[...] != 0.0
    ops = scaled_operands(0)
    for h in range(GLA_HEADS):
        ks = slice(h * dk, (h + 1) * dk)
        vs = slice(h * dv, (h + 1) * dv)
        scores = [_dot_nt(ops["qs"], ops["kinv"])] + [_dot_nt(ops[f"q{l}"], ops[f"k{l}"]) for l in range(nl)]
        v_h = v_ref[:, vs]
        st = st_ref[h]
        o_state = _dot_nt(ops["qt"], st.astype(BF16))
        st_new = jnp.exp2(total[:, ks]) * st + _dot_tn(v_h, ops["kd"])
        if h + 1 < GLA_HEADS:
            ops = scaled_operands(h + 1)
        a = jnp.where(in_block, scores[0], 0.0) + scores[1]
        for l in range(1, nl):
            a = a + jnp.where(mlev_ref[l - 1] != 0.0, scores[l + 1], 0.0)
        st_ref[h] = st_new
        o_ref[:, vs] = (_dot(a.astype(BF16), v_h) + o_state).astype(o_ref.dtype)

    @pl.when(t == nt - 1)
    def _():
        for h in range(GLA_HEADS):
            sout_ref[h] = st_ref[h].T


def _gla_prompt(z3, zag3, wa2, ba, tt):
    b, t, _ = z3.shape
    h = GLA_HEADS
    consts, halves = _gla_constants(tt)
    const_specs = [pl.BlockSpec(c.shape, lambda i, s, nd=c.ndim: (0,) * nd) for c in consts]
    return pl.pallas_call(
        functools.partial(_gla_prompt_kernel, halves=tuple(halves)),
        grid=(b, t // tt),
        in_specs=[
            pl.BlockSpec((None, tt, GLA_DK_TOTAL), lambda i, s: (i, s, OFF_QG // GLA_DK_TOTAL)),
            pl.BlockSpec((None, tt, GLA_DK_TOTAL), lambda i, s: (i, s, OFF_KG // GLA_DK_TOTAL)),
            pl.BlockSpec((None, tt, GLA_DV_TOTAL), lambda i, s: (i, s, OFF_VG // GLA_DV_TOTAL)),
            pl.BlockSpec((None, tt, LANES), lambda i, s: (i, s, 0)),
            pl.BlockSpec((LANES, GLA_DK_TOTAL), lambda i, s: (0, 0)),
            pl.BlockSpec((1, GLA_DK_TOTAL), lambda i, s: (0, 0)),
            *const_specs,
        ],
        out_specs=[
            pl.BlockSpec((None, tt, GLA_DV_TOTAL), lambda i, s: (i, s, 0)),
            pl.BlockSpec((None, h, GLA_DK, GLA_DV), lambda i, s: (i, 0, 0, 0)),
        ],
        out_shape=[
            jax.ShapeDtypeStruct((b, t, GLA_DV_TOTAL), BF16),
            jax.ShapeDtypeStruct((b, h, GLA_DK, GLA_DV), F32),
        ],
        scratch_shapes=[pltpu.VMEM((h, GLA_DV, GLA_DK), F32)],
        compiler_params=_cparams(("arbitrary", "arbitrary")),
        name="gla_prompt",
    )(z3, z3, z3, zag3, wa2, ba, *consts)


def _gla_sample_kernel(q_ref, k_ref, v_ref, ag_ref, wa2_ref, ba_ref, s0_ref, o_ref, s1_ref):
    dk, dv = GLA_DK, GLA_DV
    r_k = lax.broadcasted_iota(jnp.int32, (PAIR_ROWS, dk), 0)
    r_v = lax.broadcasted_iota(jnp.int32, (PAIR_ROWS, dv), 0)
    step_k = r_k & (SAMPLE_T - 1)
    step_v = r_v & (SAMPLE_T - 1)
    odd_k = r_k >= SAMPLE_T
    odd_v = r_v >= SAMPLE_T
    la_all = _log_decay(ag_ref[...], wa2_ref[...], ba_ref[...])
    kpad = jnp.zeros((LANES - PAIR_ROWS, dk), F32)
    vpad = jnp.zeros((LANES - PAIR_ROWS, dv), BF16)
    ones = jnp.ones((LANES, LANES), BF16)

    for h in range(GLA_HEADS):
        la = la_all[:, h * dk:(h + 1) * dk]
        b = la + jnp.where(step_k >= 1, pltpu.roll(la, 1, 0), 0.0)
        b = b + jnp.where(step_k >= 2, pltpu.roll(b, 2, 0), 0.0)
        b_last = jnp.where(odd_k, b[PAIR_ROWS - 1:PAIR_ROWS, :], b[SAMPLE_T - 1:SAMPLE_T, :])
        q = q_ref[:, h * dk:(h + 1) * dk] * (dk ** -0.5)
        k = k_ref[:, h * dk:(h + 1) * dk]
        v = v_ref[:, h * dv:(h + 1) * dv]
        q_i = q * jnp.exp(b)
        k_i = k * jnp.exp(-b)
        k_d = k * jnp.exp(b_last - b)
        o = jnp.sum(q_i * k_i, axis=-1, keepdims=True) * v
        for d in range(1, SAMPLE_T):
            a_d = jnp.sum(q_i * pltpu.roll(k_i, d, 0), axis=-1, keepdims=True)
            o = o + jnp.where(step_v >= d, a_d * pltpu.roll(v, d, 0), 0.0)
        q_b = q_i.astype(BF16)
        vb = jnp.concatenate([v.astype(BF16), vpad], axis=0)
        o_par = []
        for e in range(2):
            s_old = s0_ref[e, h]
            o_par.append(_dot(q_b, s_old.astype(BF16)))
            sel = (r_k >= SAMPLE_T) if e == 1 else (r_k < SAMPLE_T)
            kd_e = jnp.concatenate([jnp.where(sel, k_d, 0.0), kpad], axis=0).astype(BF16)
            la_e = jnp.concatenate([jnp.where(sel, la, 0.0), kpad], axis=0)
            hi, lo = _split_bf16(la_e)
            decay = jnp.exp(_dot_tn(hi, ones) + _dot_tn(lo, ones))
            decay_full = jnp.concatenate([decay] * (dv // LANES), axis=1)
            s1_ref[e, h] = decay_full * s_old + _dot_tn(kd_e, vb)
        o = o + jnp.where(odd_v, o_par[1], o_par[0])
        o_ref[:, h * dv:(h + 1) * dv] = o


def _gla_sample(zs, zag, wa2, ba, state):
    n = zs.shape[0]
    bd, h, dk, dv = state.shape
    state_spec = pl.BlockSpec((2, h, dk, dv), lambda i: (i, 0, 0, 0))
    return pl.pallas_call(
        _gla_sample_kernel,
        grid=(n // PAIR_ROWS,),
        in_specs=[
            pl.BlockSpec((PAIR_ROWS, GLA_DK_TOTAL), lambda i: (i, OFF_QG // GLA_DK_TOTAL)),
            pl.BlockSpec((PAIR_ROWS, GLA_DK_TOTAL), lambda i: (i, OFF_KG // GLA_DK_TOTAL)),
            pl.BlockSpec((PAIR_ROWS, GLA_DV_TOTAL), lambda i: (i, OFF_VG // GLA_DV_TOTAL)),
            pl.BlockSpec((PAIR_ROWS, LANES), lambda i: (i, 0)),
            pl.BlockSpec((LANES, GLA_DK_TOTAL), lambda i: (0, 0)),
            pl.BlockSpec((1, GLA_DK_TOTAL), lambda i: (0, 0)),
            state_spec,
        ],
        out_specs=[
            pl.BlockSpec((PAIR_ROWS, GLA_DV_TOTAL), lambda i: (i, 0)),
            state_spec,
        ],
        out_shape=[
            jax.ShapeDtypeStruct((n, GLA_DV_TOTAL), F32),
            jax.ShapeDtypeStruct(state.shape, F32),
        ],
        compiler_params=_cparams(("arbitrary",)),
        name="gla_sample",
    )(zs, zs, zs, zag, wa2, ba, state)


def _merge_gla_branch(og_ref, rg_ref, gg_ref, gn_ref, pgla_ref):
    gn = gn_ref[...]
    og = jnp.concatenate(
        [_gla_out(og_ref[:, h * GLA_DV:(h + 1) * GLA_DV].astype(F32), gn,
                  rg_ref[:, h * GLA_DV:(h + 1) * GLA_DV].astype(F32)).astype(BF16)
         for h in range(GLA_HEADS)], axis=1)
    return _sigmoid(gg_ref[...].astype(F32)) * _dot(og, pgla_ref[...])


def _merge_finish(yb, osw_ref, gs_ref, x_ref, pswa_ref, wo_ref, n2_ref, x1_ref, h2_ref):
    y = _sigmoid(gs_ref[...].astype(F32)) * _dot(osw_ref[...], pswa_ref[...]) + yb
    x1 = x_ref[...] + _dot(y.astype(BF16), wo_ref[...])
    x1_ref[...] = x1
    h2_ref[...] = _rms(x1, n2_ref[...]).astype(BF16)


def _merge_kernel(osw_ref, og_ref, rg_ref, gs_ref, gg_ref, x_ref, gn_ref, pswa_ref, pgla_ref, wo_ref, n2_ref,
                  x1_ref, h2_ref):
    yb = _merge_gla_branch(og_ref, rg_ref, gg_ref, gn_ref, pgla_ref)
    _merge_finish(yb, osw_ref, gs_ref, x_ref, pswa_ref, wo_ref, n2_ref, x1_ref, h2_ref)


def _merge_swa_kernel(sink_ref, q_ref, k_ref, v_ref, c_ref, slo_ref, shi_ref, *refs, tiles_per_batch):
    (og_ref, rg_ref, gs_ref, gg_ref, x_ref, gn_ref, pswa_ref, pgla_ref, wo_ref, n2_ref, x1_ref, h2_ref,
     klast_ref, vlast_ref, kprev_ref, vprev_ref, osw_ref) = refs
    w = WINDOW
    t_local = lax.rem(pl.program_id(0), tiles_per_batch)

    @pl.when(t_local == 0)
    def _():
        kprev_ref[...] = jnp.zeros_like(kprev_ref)
        vprev_ref[...] = jnp.zeros_like(vprev_ref)

    yb = _merge_gla_branch(og_ref, rg_ref, gg_ref, gn_ref, pgla_ref)
    k_prev, v_prev = kprev_ref[...], vprev_ref[...]
    for j in range(q_ref.shape[0] // w):
        rs = slice(j * w, (j + 1) * w)
        tabs = (c_ref[rs, :], slo_ref[rs, :], shi_ref[rs, :])
        has_prev = (t_local > 0) if j == 0 else True
        k_prev, v_prev = _swa_block(has_prev, q_ref[rs, :], k_ref[rs, :], v_ref[rs, :], tabs, sink_ref,
                                    k_prev, v_prev, osw_ref, j * w)
    kprev_ref[...] = k_prev
    vprev_ref[...] = v_prev
    _merge_finish(yb, osw_ref, gs_ref, x_ref, pswa_ref, wo_ref, n2_ref, x1_ref, h2_ref)

    @pl.when(t_local == tiles_per_batch - 1)
    def _():
        klast_ref[...] = kprev_ref[...]
        vlast_ref[...] = vprev_ref[...]


def _merge_specs(tm):
    resident = functools.partial(pl.BlockSpec, pipeline_mode=pl.Buffered(1))
    in_specs = [
        pl.BlockSpec((tm, GLA_DV_TOTAL), lambda i: (i, 0)),
        pl.BlockSpec((tm, GLA_DV_TOTAL), lambda i: (i, OFF_RG // GLA_DV_TOTAL)),
        pl.BlockSpec((tm, D_MODEL), lambda i: (i, OFF_GS // D_MODEL)),
        pl.BlockSpec((tm, D_MODEL), lambda i: (i, OFF_GG // D_MODEL)),
        pl.BlockSpec((tm, D_MODEL), lambda i: (i, 0)),
        pl.BlockSpec((1, GLA_DV), lambda i: (0, 0)),
        resident((SWA_Q, D_MODEL), lambda i: (0, 0)),
        resident((GLA_DV_TOTAL, D_MODEL), lambda i: (0, 0)),
        resident((D_MODEL, D_MODEL), lambda i: (0, 0)),
        pl.BlockSpec((1, D_MODEL), lambda i: (0, 0)),
    ]
    out_specs = [pl.BlockSpec((tm, D_MODEL), lambda i: (i, 0)), pl.BlockSpec((tm, D_MODEL), lambda i: (i, 0))]
    return in_specs, out_specs


def _merge_swa(z, o_gla, x2d, gnorm, p_swa, p_gla, w_o, norm2, sink, tables, batch, tm):
    n = x2d.shape[0]
    w = WINDOW
    tiles_per_batch = n // batch // tm
    in_specs, out_specs = _merge_specs(tm)
    tab_spec = pl.BlockSpec((tm, LANES), lambda i: (lax.rem(i, tiles_per_batch), 0))
    last_spec = pl.BlockSpec((None, w, SWA_KV), lambda i: (i // tiles_per_batch, 0, 0))
    return pl.pallas_call(
        functools.partial(_merge_swa_kernel, tiles_per_batch=tiles_per_batch),
        grid=(n // tm,),
        in_specs=[
            pl.BlockSpec(memory_space=pltpu.SMEM),
            pl.BlockSpec((tm, SWA_Q), lambda i: (i, OFF_QS // SWA_Q)),
            pl.BlockSpec((tm, SWA_KV), lambda i: (i, OFF_KS // SWA_KV)),
            pl.BlockSpec((tm, SWA_KV), lambda i: (i, OFF_VS // SWA_KV)),
            tab_spec, tab_spec, tab_spec,
        ] + in_specs,
        out_specs=out_specs + [last_spec, last_spec],
        out_shape=[
            jax.ShapeDtypeStruct((n, D_MODEL), F32),
            jax.ShapeDtypeStruct((n, D_MODEL), BF16),
            jax.ShapeDtypeStruct((batch, w, SWA_KV), F32),
            jax.ShapeDtypeStruct((batch, w, SWA_KV), F32),
        ],
        scratch_shapes=[pltpu.VMEM((w, SWA_KV), F32), pltpu.VMEM((w, SWA_KV), F32),
                        pltpu.VMEM((tm, SWA_Q), BF16)],
        compiler_params=_cparams(("arbitrary",)),
        name="merge_swa",
    )(sink, z, z, z, *tables, o_gla, z, z, z, x2d, gnorm, p_swa, p_gla, w_o, norm2)


def _merge(o_swa, o_gla, z, x2d, gnorm, p_swa, p_gla, w_o, norm2, tm):
    n = x2d.shape[0]
    in_specs, out_specs = _merge_specs(tm)
    return pl.pallas_call(
        _merge_kernel,
        grid=(n // tm,),
        in_specs=[pl.BlockSpec((tm, SWA_Q), lambda i: (i, 0))] + in_specs,
        out_specs=out_specs,
        out_shape=[
            jax.ShapeDtypeStruct((n, D_MODEL), F32),
            jax.ShapeDtypeStruct((n, D_MODEL), BF16),
        ],
        compiler_params=_cparams(("arbitrary",)),
        name="merge",
    )(o_swa, o_gla, z, z, z, x2d, gnorm, p_swa, p_gla, w_o, norm2)


def _mlp_step(h2_ref, x1_ref, load_wup, load_wdn, fn_ref, out_ref, acc_ref):
    f = pl.program_id(1)
    nf = pl.num_programs(1)

    @pl.when(f == 0)
    def _():
        acc_ref[...] = jnp.zeros_like(acc_ref)

    u = _dot(h2_ref[...], load_wup())
    u = jnp.square(jnp.maximum(u, 0.0)).astype(BF16)
    acc_ref[...] += _dot(u, load_wdn())

    @pl.when(f == nf - 1)
    def _():
        out_ref[...] = _rms(x1_ref[...] + acc_ref[...], fn_ref[...])


def _mlp_kernel(h2_ref, x1_ref, wup_ref, wdn_ref, fn_ref, out_ref, acc_ref):
    _mlp_step(h2_ref, x1_ref, lambda: wup_ref[...], lambda: wdn_ref[...], fn_ref, out_ref, acc_ref)


def _mlp_cast_kernel(h2_ref, x1_ref, wup_ref, wdn_ref, fn_ref, out_ref, wup_bf_ref, wdn_bf_ref, acc_ref):
    def cast_and_keep(src_ref, dst_ref):
        def load():
            w = src_ref[...].astype(BF16)
            dst_ref[...] = w
            return w
        return load

    _mlp_step(h2_ref, x1_ref, cast_and_keep(wup_ref, wup_bf_ref), cast_and_keep(wdn_ref, wdn_bf_ref),
              fn_ref, out_ref, acc_ref)


def _mlp_cast(h2, x1, w_up, w_down, final_norm, tf):
    n = h2.shape[0]
    return pl.pallas_call(
        _mlp_cast_kernel,
        grid=(1, D_FF // tf),
        in_specs=[
            pl.BlockSpec((n, D_MODEL), lambda i, f: (0, 0)),
            pl.BlockSpec((n, D_MODEL), lambda i, f: (0, 0)),
            pl.BlockSpec((None, D_MODEL, tf), lambda i, f: (0, 0, f)),
            pl.BlockSpec((None, tf, D_MODEL), lambda i, f: (0, f, 0)),
            pl.BlockSpec((1, D_MODEL), lambda i, f: (0, 0)),
        ],
        out_specs=[
            pl.BlockSpec((n, D_MODEL), lambda i, f: (0, 0)),
            pl.BlockSpec((D_MODEL, tf), lambda i, f: (0, f)),
            pl.BlockSpec((tf, D_MODEL), lambda i, f: (f, 0)),
        ],
        out_shape=[
            jax.ShapeDtypeStruct((n, D_MODEL), F32),
            jax.ShapeDtypeStruct((D_MODEL, D_FF), BF16),
            jax.ShapeDtypeStruct((D_FF, D_MODEL), BF16),
        ],
        scratch_shapes=[pltpu.VMEM((n, D_MODEL), F32)],
        compiler_params=_cparams(("arbitrary", "arbitrary")),
        name="mlp_cast",
    )(h2, x1, w_up, w_down, final_norm)


def _mlp(h2, x1, w_up, w_down, final_norm, tm, tf):
    n = h2.shape[0]
    return pl.pallas_call(
        _mlp_kernel,
        grid=(n // tm, D_FF // tf),
        in_specs=[
            pl.BlockSpec((tm, D_MODEL), lambda i, f: (i, 0)),
            pl.BlockSpec((tm, D_MODEL), lambda i, f: (i, 0)),
            pl.BlockSpec((D_MODEL, tf), lambda i, f: (0, f)),
            pl.BlockSpec((tf, D_MODEL), lambda i, f: (f, 0)),
            pl.BlockSpec((1, D_MODEL), lambda i, f: (0, 0)),
        ],
        out_specs=pl.BlockSpec((tm, D_MODEL), lambda i, f: (i, 0)),
        out_shape=jax.ShapeDtypeStruct((n, D_MODEL), F32),
        scratch_shapes=[pltpu.VMEM((tm, D_MODEL), F32)],
        compiler_params=_cparams(("arbitrary", "arbitrary")),
        name="mlp",
    )(h2, x1, w_up, w_down, final_norm)


def _pick_tile(n, pref):
    t = min(n, pref)
    while n % t:
        t //= 2
    return t


def _tile_plan(n_prompt, n_sample, t_prompt, b_sample):
    return {
        "proj_rows_p": _pick_tile(n_prompt, 512),
        "swa_pairs": _pick_tile(b_sample // 2, 4),
        "gla_rows": _pick_tile(t_prompt, GLA_TILE),
        "merge_rows_p": _pick_tile(n_prompt, 256), "merge_rows_s": _pick_tile(n_sample, 256),
        "mlp_rows_p": _pick_tile(n_prompt, 512),
        "mlp_ff": 1024, "mlp_cast_ff": 512,
    }


W_IN_SPLITS = (("qs", SWA_Q), ("ks", SWA_KV), ("vs", SWA_KV), ("qg", GLA_DK_TOTAL), ("kg", GLA_DK_TOTAL),
               ("vg", GLA_DV_TOTAL), ("rg", GLA_DV_TOTAL), ("ag", GLA_GATE_RANK), ("gs", D_MODEL), ("gg", D_MODEL))
W_IN_COLS = sum(width for _, width in W_IN_SPLITS)


REORDER_PIECE = HEAD_DIM
REORDER_PIECES = 4


def _w_in_row_table():
    src, o = {}, 0
    for name, width in W_IN_SPLITS:
        src[name] = o
        o += width
    rows = np.zeros(Z_COLS, np.int64)
    for name, dst in (("gs", OFF_GS), ("gg", OFF_GG), ("vg", OFF_VG), ("rg", OFF_RG), ("qg", OFF_QG),
                      ("kg", OFF_KG), ("ks", OFF_KS), ("vs", OFF_VS)):
        width = dict(W_IN_SPLITS)[name]
        rows[dst:dst + width] = src[name] + np.arange(width)
    for h in range(GROUP):
        for g in range(N_KV_HEADS):
            dst = OFF_QS + h * SWA_KV + g * HEAD_DIM
            rows[dst:dst + HEAD_DIM] = src["qs"] + (g * GROUP + h) * HEAD_DIM + np.arange(HEAD_DIM)
    pieces = rows.reshape(-1, REORDER_PIECE)
    assert (pieces == pieces[:, :1] + np.arange(REORDER_PIECE)).all()
    return jnp.asarray(pieces[:, 0], jnp.int32), src["ag"]


def _reorder_w_in_kernel(tbl_ref, *refs):
    piece_refs = refs[:REORDER_PIECES]
    ag_ref, x_ref, nw_ref, wm_ref, wag_ref, z_ref, zag_ref, h_ref = refs[REORDER_PIECES:]

    @pl.when(pl.program_id(0) == 0)
    def _():
        pad = jnp.zeros((LANES - GLA_GATE_RANK, wag_ref.shape[1]), BF16)
        wag = jnp.concatenate([ag_ref[0].astype(BF16), pad], axis=0)
        wag_ref[...] = wag
        h = _rms(x_ref[...], nw_ref[...]).astype(BF16)
        h_ref[...] = h
        zag_ref[...] = _dot_nt(h, wag)

    tile = jnp.concatenate([p[0].astype(BF16) for p in piece_refs], axis=0)
    wm_ref[...] = tile
    z_ref[...] = _dot_nt(h_ref[...], tile)


def _reorder_w_in(w_in_t, x_rows, norm1):
    d = w_in_t.shape[2]
    n = x_rows.shape[0]
    table, ag_row = _w_in_row_table()
    rows = REORDER_PIECE * REORDER_PIECES
    piece = lambda k: pl.BlockSpec((pl.Element(1), pl.Element(REORDER_PIECE), pl.Element(d)),
                                   lambda i, tbl: (0, pl.multiple_of(tbl[i * REORDER_PIECES + k], GLA_GATE_RANK), 0))
    grid_spec = pltpu.PrefetchScalarGridSpec(
        num_scalar_prefetch=1,
        grid=(Z_COLS // rows,),
        in_specs=[piece(k) for k in range(REORDER_PIECES)]
        + [pl.BlockSpec((pl.Element(1), pl.Element(GLA_GATE_RANK), pl.Element(d)),
                        lambda i, tbl: (0, ag_row, 0)),
           pl.BlockSpec((n, d), lambda i, tbl: (0, 0)),
           pl.BlockSpec((1, d), lambda i, tbl: (0, 0))],
        out_specs=[pl.BlockSpec((rows, d), lambda i, tbl: (i, 0)),
                   pl.BlockSpec((LANES, d), lambda i, tbl: (0, 0)),
                   pl.BlockSpec((n, rows), lambda i, tbl: (0, i)),
                   pl.BlockSpec((n, LANES), lambda i, tbl: (0, 0))],
        scratch_shapes=[pltpu.VMEM((n, d), BF16)],
    )
    return pl.pallas_call(
        _reorder_w_in_kernel,
        grid_spec=grid_spec,
        out_shape=[jax.ShapeDtypeStruct((Z_COLS, d), BF16), jax.ShapeDtypeStruct((LANES, d), BF16),
                   jax.ShapeDtypeStruct((n, Z_COLS), F32), jax.ShapeDtypeStruct((n, LANES), F32)],
        compiler_params=_cparams(("arbitrary",)),
        name="reorder_w_in",
    )(table, *([w_in_t] * (REORDER_PIECES + 1)), x_rows, norm1)


def kernel(x_prompt, x_sample, cache_swa_k, cache_swa_v, state_gla, norm1, w_in, w_a2, b_a, sink,
           gla_norm, p_swa, p_gla, w_o, norm2, w_up, w_down, final_norm):
    assert norm1.shape[0] == 1, "single-layer stack"
    bp, tp, d = x_prompt.shape
    bs, ts, _ = x_sample.shape
    assert ts == SAMPLE_T and bs % 2 == 0 and tp % WINDOW == 0
    w_buf = cache_swa_k.shape[2]
    assert w_buf == WINDOW

    wa2 = jnp.pad(w_a2[0], ((0, LANES - GLA_GATE_RANK), (0, 0))).astype(BF16)
    ba = b_a[0][None, :]
    n1, n2, fn = norm1[0][None, :], norm2[0][None, :], final_norm[None, :]
    gn = gla_norm[0][None, :]
    pswa = p_swa[0].reshape(N_KV_HEADS, GROUP, HEAD_DIM, d).transpose(1, 0, 2, 3).reshape(SWA_Q, d).astype(BF16)
    pgla, wo = p_gla[0].astype(BF16), w_o[0].astype(BF16)
    sink_smem = sink[0][None, :]
    sink_rows = jnp.broadcast_to(jnp.repeat(sink[0], PAIR_ROWS)[:, None], (N_HEADS * PAIR_ROWS, LANES))

    xp = x_prompt.reshape(bp * tp, d)
    xs = x_sample.reshape(bs * ts, d)
    np_, ns = xp.shape[0], xs.shape[0]

    tiles = _tile_plan(np_, ns, tp, bs)

    w_main, w_ag, zs, zags = _reorder_w_in(jnp.swapaxes(w_in, 1, 2), xs, n1)
    zp, zagp = _in_proj(xp, n1, w_main, w_ag, tiles["proj_rows_p"], Z_COLS // 2, BF16)

    zp3 = zp.reshape(bp, tp, Z_COLS)
    pos_s = PAST_LEN + jnp.arange(ts)
    tabs_s = _rope_tables(jnp.concatenate([pos_s, pos_s]))
    pos_minor = lambda c: jnp.transpose(c[0], (0, 2, 3, 1)).reshape(bs, SWA_KV, w_buf)
    pos_major = lambda c: jnp.transpose(c.reshape(bs, N_KV_HEADS, HEAD_DIM, w_buf), (0, 3, 1, 2))[None]
    o_swa_s, nk_s, nv_s = _swa_sample(zs, pos_minor(cache_swa_k), pos_minor(cache_swa_v), sink_rows, tabs_s,
                                      tiles["swa_pairs"])

    o_gla_p, s_p = _gla_prompt(zp3, zagp.reshape(bp, tp, LANES), wa2, ba, tiles["gla_rows"])
    o_gla_s, s_s = _gla_sample(zs, zags, wa2, ba, state_gla[0])

    x1p, h2p, k_last, v_last = _merge_swa(zp, o_gla_p.reshape(np_, GLA_DV_TOTAL), xp, gn, pswa, pgla, wo, n2,
                                          sink_smem, _rope_tables(jnp.arange(tp)), bp, tiles["merge_rows_p"])
    x1s, h2s = _merge(o_swa_s, o_gla_s, zs, xs, gn, pswa, pgla, wo, n2, tiles["merge_rows_s"])
    ys, wup, wdn = _mlp_cast(h2s, x1s, w_up, w_down, fn, tiles["mlp_cast_ff"])
    yp = _mlp(h2p, x1p, wup, wdn, fn, tiles["mlp_rows_p"], tiles["mlp_ff"])

    kv5 = lambda a, nb: a.reshape(1, nb, w_buf, N_KV_HEADS, HEAD_DIM)
    return (yp.reshape(bp, tp, d), ys.reshape(bs, ts, d),
            kv5(k_last, bp), kv5(v_last, bp), s_p[None],
            pos_major(nk_s), pos_major(nv_s), s_s[None])
```

```python
import functools

import jax
import jax.numpy as jnp
import numpy as np
from jax import lax
from jax.experimental import pallas as pl
from jax.experimental.pallas import tpu as pltpu

F32 = jnp.float32
BF16 = jnp.bfloat16

D_MODEL = 2048
PAST_LEN = 8192
N_HEADS = 16
N_KV_HEADS = 4
GROUP = N_HEADS // N_KV_HEADS
HEAD_DIM = 64
WINDOW = 128
ROT_DIM = HEAD_DIM // 4
ROPE_THETA = 500000.0
SWA_Q = N_HEADS * HEAD_DIM
SWA_KV = N_KV_HEADS * HEAD_DIM
GLA_HEADS = 4
GLA_DK = 256
GLA_DV = 512
GLA_DK_TOTAL = GLA_HEADS * GLA_DK
GLA_DV_TOTAL = GLA_HEADS * GLA_DV
GLA_GATE_RANK = 16
GLA_GATE_NORM = 16.0
D_FF = 4 * D_MODEL
EPS = 1e-6
LOG2E = 1.4426950408889634

LANES = 128
VMEM_LIMIT = 56 * 1024 * 1024

OFF_GS = 0
OFF_GG = OFF_GS + D_MODEL
OFF_VG = OFF_GG + D_MODEL
OFF_RG = OFF_VG + GLA_DV_TOTAL
OFF_QG = OFF_RG + GLA_DV_TOTAL
OFF_KG = OFF_QG + GLA_DK_TOTAL
OFF_QS = OFF_KG + GLA_DK_TOTAL
OFF_KS = OFF_QS + SWA_Q
OFF_VS = OFF_KS + SWA_KV
Z_COLS = OFF_VS + SWA_KV

GLA_BLOCK = 16
GLA_TILE = 256


def _cparams(sem):
    return pltpu.CompilerParams(dimension_semantics=sem, vmem_limit_bytes=VMEM_LIMIT)


def _rms(x, w):
    return x * lax.rsqrt(jnp.mean(x * x, axis=-1, keepdims=True) + EPS) * w


def _sigmoid(x):
    return 1.0 / (1.0 + jnp.exp(-x))


def _dot(a, b):
    return jnp.dot(a, b, preferred_element_type=F32)


def _dot_nt(a, b):
    return lax.dot_general(a, b, (((1,), (1,)), ((), ())), preferred_element_type=F32)


def _dot_tn(a, b):
    return lax.dot_general(a, b, (((0,), (0,)), ((), ())), preferred_element_type=F32)


def _split_bf16(x):
    hi = x.astype(BF16)
    lo = (x - hi.astype(F32)).astype(BF16)
    return hi, lo


def _in_proj_kernel(x_ref, nw_ref, w_ref, wag_ref, z_ref, zag_ref):
    h = _rms(x_ref[...], nw_ref[...]).astype(BF16)
    z_ref[...] = _dot_nt(h, w_ref[...]).astype(z_ref.dtype)

    @pl.when(pl.program_id(0) == 0)
    def _():
        zag_ref[...] = _dot_nt(h, wag_ref[...])


def _in_proj(x2d, norm1, w_main, w_ag, tm, tn, z_dtype):
    n = x2d.shape[0]
    nrow = n // tm
    w_mode = {"pipeline_mode": pl.Buffered(1)}
    return pl.pallas_call(
        _in_proj_kernel,
        grid=(Z_COLS // tn, nrow),
        in_specs=[
            pl.BlockSpec((tm, D_MODEL), lambda j, i: (i, 0)),
            pl.BlockSpec((1, D_MODEL), lambda j, i: (0, 0)),
            pl.BlockSpec((tn, D_MODEL), lambda j, i: (j, 0), **w_mode),
            pl.BlockSpec((LANES, D_MODEL), lambda j, i: (0, 0)),
        ],
        out_specs=[
            pl.BlockSpec((tm, tn), lambda j, i: (i, j)),
            pl.BlockSpec((tm, LANES), lambda j, i: (jnp.where(j == 0, i, nrow - 1), 0)),
        ],
        out_shape=[
            jax.ShapeDtypeStruct((n, Z_COLS), z_dtype),
            jax.ShapeDtypeStruct((n, LANES), F32),
        ],
        compiler_params=_cparams(("arbitrary", "arbitrary")),
        name="in_proj",
    )(x2d, norm1, w_main, w_ag)


def _rope_tables(pos):
    half = ROT_DIM // 2
    inv = ROPE_THETA ** (-jnp.arange(half, dtype=F32) * 2.0 / ROT_DIM)
    ang = pos.astype(F32)[:, None] * inv[None, :]
    cos, sin = jnp.cos(ang), jnp.sin(ang)
    t = pos.shape[0]
    ones = jnp.ones((t, HEAD_DIM - ROT_DIM), F32)
    zeros = jnp.zeros((t, HEAD_DIM - ROT_DIM), F32)
    zh = jnp.zeros((t, half), F32)
    c = jnp.concatenate([cos, cos, ones], axis=1)
    s_lo = jnp.concatenate([zh, sin, zeros], axis=1)
    s_hi = jnp.concatenate([-sin, zh, zeros], axis=1)
    rep = LANES // HEAD_DIM
    return tuple(jnp.tile(a, (1, rep)) for a in (c, s_lo, s_hi))


def _rope(x, c, s_lo, s_hi):
    half = ROT_DIM // 2
    outs = []
    for j in range(x.shape[1] // LANES):
        xc = x[:, j * LANES:(j + 1) * LANES]
        outs.append(xc * c + pltpu.roll(xc, half, 1) * s_lo + pltpu.roll(xc, LANES - half, 1) * s_hi)
    return outs[0] if len(outs) == 1 else jnp.concatenate(outs, axis=1)


def _swa_block(has_prev, q_blk, k_blk, v_blk, tabs, sink_ref, k_prev, v_prev, o_ref, row0):
    w = WINDOW
    kvw = SWA_KV
    c, s_lo, s_hi = tabs
    q = (_rope(q_blk.astype(F32), c, s_lo, s_hi) * (HEAD_DIM ** -0.5 * LOG2E)).astype(BF16)
    k = _rope(k_blk.astype(F32), c, s_lo, s_hi)
    v = v_blk.astype(F32)
    kc = jnp.concatenate([k_prev, k], axis=0).astype(BF16)
    vc = jnp.concatenate([v_prev, v], axis=0).astype(BF16)

    rows = lax.broadcasted_iota(jnp.int32, (w, 2 * w), 0)
    cols = lax.broadcasted_iota(jnp.int32, (w, 2 * w), 1)
    diff = rows + w - cols
    bias = jnp.where((diff >= 0) & (diff < w) & ((cols >= w) | has_prev), 0.0, -jnp.inf)
    bias = jnp.concatenate([bias] * GROUP, axis=0)
    head_shift = HEAD_DIM.bit_length() - 1
    lane_head_q = lax.broadcasted_iota(jnp.int32, (w, kvw), 1) >> head_shift
    lane_head = lax.broadcasted_iota(jnp.int32, (2 * w, kvw), 1) >> head_shift

    for g in range(N_KV_HEADS):
        in_g_q = jnp.where(lane_head_q == g, 1.0, 0.0).astype(BF16)
        in_g = jnp.where(lane_head == g, 1.0, 0.0).astype(BF16)
        qg = jnp.concatenate([q[:, h * kvw:(h + 1) * kvw] * in_g_q for h in range(GROUP)], axis=0)
        sk = jnp.concatenate([jnp.full((w, LANES), sink_ref[0, g * GROUP + h] * LOG2E, F32)
                              for h in range(GROUP)], axis=0)
        s = _dot_nt(qg, kc) + bias
        m = jnp.maximum(jnp.broadcast_to(jnp.max(s, axis=-1, keepdims=True), sk.shape), sk)
        p = jnp.exp2(s - jnp.concatenate([m, m], axis=1)).astype(BF16)
        vg = vc * in_g + (1.0 - in_g)
        oa = _dot(p, vg)
        p_sink = jnp.exp2(sk - m)
        denom = pltpu.roll(oa, 2 * HEAD_DIM, 1) + jnp.concatenate([p_sink, p_sink], axis=1)
        o = (oa / denom).astype(o_ref.dtype)
        for h in range(GROUP):
            lo = g * HEAD_DIM
            o_ref[row0:row0 + w, h * kvw + lo:h * kvw + lo + HEAD_DIM] = o[h * w:(h + 1) * w, lo:lo + HEAD_DIM]
    return k, v


SAMPLE_T = 4
PAIR_ROWS = 2 * SAMPLE_T


def _swa_sample_kernel(q_ref, k_ref, v_ref, ck_ref, cv_ref, c_ref, slo_ref, shi_ref, sink_ref,
                       o_ref, nk_ref, nv_ref, *, pairs):
    w = WINDOW
    kvw = SWA_KV
    c, s_lo, s_hi = c_ref[...], slo_ref[...], shi_ref[...]
    nrow = N_HEADS * PAIR_ROWS
    row = lax.broadcasted_iota(jnp.int32, (nrow, w), 0)
    col = lax.broadcasted_iota(jnp.int32, (nrow, w), 1)
    t_shift = SAMPLE_T.bit_length() - 1
    sq = row & (SAMPLE_T - 1)
    par = (row >> t_shift) & 1
    mask_cache = col > sq
    mask_new = (col < PAIR_ROWS) & ((col >> t_shift) == par) & ((col & (SAMPLE_T - 1)) <= sq)
    lane = lax.broadcasted_iota(jnp.int32, (PAIR_ROWS, kvw), 1) >> (HEAD_DIM.bit_length() - 1)
    pos = lax.broadcasted_iota(jnp.int32, (kvw, w), 1)
    sk = sink_ref[...][:, :1]
    zpad = jnp.zeros((w - PAIR_ROWS, kvw), F32)

    for p in range(pairs):
        rs = slice(p * PAIR_ROWS, (p + 1) * PAIR_ROWS)
        q8 = _rope(q_ref[rs, :], c, s_lo, s_hi) * (HEAD_DIM ** -0.5)
        k8 = _rope(k_ref[rs, :], c, s_lo, s_hi)
        v8 = v_ref[rs, :]
        blocks = [jnp.where(lane == g, q8[:, h * kvw:(h + 1) * kvw], 0.0)
                  for g in range(N_KV_HEADS) for h in range(GROUP)]
        qall = jnp.concatenate(blocks, axis=0).astype(BF16)

        k_new = jnp.concatenate([k8, zpad], axis=0)
        v_new = jnp.concatenate([v8, zpad], axis=0)
        s_cache = [_dot(qall, ck_ref[2 * p + e].astype(BF16)) for e in range(2)]
        s_cache = jnp.where(mask_cache, jnp.where(par == 1, s_cache[1], s_cache[0]), -jnp.inf)
        s_new = jnp.where(mask_new, _dot_nt(qall, k_new.astype(BF16)), -jnp.inf)
        s = jnp.concatenate([s_cache, s_new], axis=1)
        m = jnp.maximum(jnp.max(s, axis=-1, keepdims=True), sk)
        pr = jnp.exp(s - m)
        denom = jnp.sum(pr, axis=-1, keepdims=True) + jnp.exp(sk - m)
        p_cache, p_new = pr[:, :w], pr[:, w:]
        p0 = jnp.where(par == 0, p_cache, 0.0).astype(BF16)
        p1 = jnp.where(par == 1, p_cache, 0.0).astype(BF16)
        oall = (_dot_nt(p0, cv_ref[2 * p].astype(BF16)) + _dot_nt(p1, cv_ref[2 * p + 1].astype(BF16))
                + _dot(p_new.astype(BF16), v_new.astype(BF16))) / denom

        outs = []
        for h in range(GROUP):
            acc = None
            for g in range(N_KV_HEADS):
                hh = g * GROUP + h
                blk = jnp.where(lane == g, oall[hh * PAIR_ROWS:(hh + 1) * PAIR_ROWS], 0.0)
                acc = blk if acc is None else acc + blk
            outs.append(acc)
        o_ref[rs, :] = jnp.concatenate(outs, axis=1).astype(o_ref.dtype)

        for new, cref, nref in ((k_new, ck_ref, nk_ref), (v_new, cv_ref, nv_ref)):
            new_t = new.T
            for e in range(2):
                bd = 2 * p + e
                shifted = pltpu.roll(cref[bd], w - SAMPLE_T, 1)
                tail = pltpu.roll(new_t, w - SAMPLE_T - e * SAMPLE_T, 1)
                nref[bd] = jnp.where(pos >= w - SAMPLE_T, tail, shifted)


def _swa_sample(zs, cache_k, cache_v, sink_rows, tables, pairs):
    n = zs.shape[0]
    bd, kvw, w = cache_k.shape
    rows = pairs * PAIR_ROWS
    tab_spec = pl.BlockSpec((PAIR_ROWS, LANES), lambda i: (0, 0))
    cache_spec = pl.BlockSpec((2 * pairs, kvw, w), lambda i: (i, 0, 0))
    return pl.pallas_call(
        functools.partial(_swa_sample_kernel, pairs=pairs),
        grid=(n // rows,),
        in_specs=[
            pl.BlockSpec((rows, SWA_Q), lambda i: (i, OFF_QS // SWA_Q)),
            pl.BlockSpec((rows, SWA_KV), lambda i: (i, OFF_KS // SWA_KV)),
            pl.BlockSpec((rows, SWA_KV), lambda i: (i, OFF_VS // SWA_KV)),
            cache_spec, cache_spec,
            tab_spec, tab_spec, tab_spec,
            pl.BlockSpec((N_HEADS * PAIR_ROWS, LANES), lambda i: (0, 0)),
        ],
        out_specs=[
            pl.BlockSpec((rows, SWA_Q), lambda i: (i, 0)),
            cache_spec, cache_spec,
        ],
        out_shape=[
            jax.ShapeDtypeStruct((n, SWA_Q), BF16),
            jax.ShapeDtypeStruct((bd, kvw, w), F32),
            jax.ShapeDtypeStruct((bd, kvw, w), F32),
        ],
        compiler_params=_cparams(("arbitrary",)),
        name="swa_sample",
    )(zs, zs, zs, cache_k, cache_v, *tables, sink_rows)


def _log_decay(ag, wa2, ba):
    x = _dot(ag.astype(BF16), wa2) + ba
    log_sig = jnp.minimum(x, 0.0) - jnp.log(1.0 + jnp.exp(-jnp.abs(x)))
    return log_sig * (1.0 / GLA_GATE_NORM)


def _gla_out(o, gnorm, rg):
    dv = o.shape[-1]
    ms = jnp.broadcast_to(jnp.sum(o * o, axis=-1, keepdims=True), (o.shape[0], LANES)) * (1.0 / dv)
    r = lax.rsqrt(ms + EPS)
    half = rg * 0.5
    gate = half + half * jnp.tanh(half)
    return o * jnp.concatenate([r] * (dv // LANES), axis=1) * gnorm * gate


def _gla_constants(tt):
    nb = tt // GLA_BLOCK
    halves = [tt >> (l + 1) for l in range(nb.bit_length() - 1)]
    step = np.arange(tt)
    blk = step // GLA_BLOCK
    t16 = ((blk[:, None] == blk[None, :]) & (step[None, :] <= step[:, None])).astype(np.float32)
    rows = [(blk[None, :] < np.arange(nb)[:, None])]
    for hs in halves:
        t_b = (np.arange(nb) * GLA_BLOCK) // (2 * hs) * (2 * hs) + hs
        rows.append(step[None, :] < t_b[:, None])
    rows.append(np.ones((GLA_BLOCK, tt), bool))
    sel = np.concatenate(rows, axis=0).astype(np.float32)
    group = [(step[:, None] // (2 * hs)) == (step[None, :] // (2 * hs)) for hs in halves[1:]]
    mlev = np.stack(group).astype(np.float32)
    return (jnp.asarray(t16, BF16), jnp.asarray(sel, BF16), jnp.asarray(t16), jnp.asarray(mlev)), halves


def _gla_prompt_kernel(q_ref, k_ref, v_ref, ag_ref, wa2_ref, ba_ref,
                       t16_ref, sel_ref, mdiag_ref, mlev_ref, o_ref, sout_ref, st_ref, *, halves):
    t = pl.program_id(1)
    nt = pl.num_programs(1)
    tt = q_ref.shape[0]
    nb = tt // GLA_BLOCK
    nl = len(halves)
    dk, dv = GLA_DK, GLA_DV

    @pl.when(t == 0)
    def _():
        st_ref[...] = jnp.zeros_like(st_ref)

    la = _log_decay(ag_ref[...], wa2_ref[...], ba_ref[...]) * LOG2E
    hi, lo = _split_bf16(la)
    c = _dot(t16_ref[...], hi) + _dot(t16_ref[...], lo)
    cum = _dot(sel_ref[...], hi) + _dot(sel_ref[...], lo)
    p_start = cum[:nb]
    total = cum[(1 + nl) * nb:(1 + nl) * nb + 1]
    exp_p = jnp.exp2(p_start)
    d_last = total - p_start
    after, before = [], []
    for l in range(nl):
        d = p_start - cum[(1 + l) * nb:(2 + l) * nb]
        after.append(jnp.exp2(jnp.minimum(d, 0.0)))
        before.append(-d)

    zeros = jnp.zeros((GLA_BLOCK, dk), BF16)
    names = ["qs", "kinv", "qt", "kd"] + [f"q{l}" for l in range(nl)] + [f"k{l}" for l in range(nl)]

    def scaled_operands(h):
        ks = slice(h * dk, (h + 1) * dk)
        parts = {nm: [] for nm in names}
        for i in range(nb):
            rs = slice(i * GLA_BLOCK, (i + 1) * GLA_BLOCK)
            row = slice(i, i + 1)
            c_b = c[rs, ks]
            k_b = k_ref[rs, ks].astype(F32)
            qs = q_ref[rs, ks].astype(F32) * (dk ** -0.5) * jnp.exp2(c_b)
            parts["qs"].append(qs.astype(BF16))
            parts["kinv"].append((k_b * jnp.exp2(-c_b)).astype(BF16))
            parts["qt"].append((qs * exp_p[row, ks]).astype(BF16))
            parts["kd"].append((k_b * jnp.exp2(d_last[row, ks] - c_b)).astype(BF16))
            for l, hs in enumerate(halves):
                if (i * GLA_BLOCK) % (2 * hs) >= hs:
                    parts[f"q{l}"].append((qs * after[l][row, ks]).astype(BF16))
                    parts[f"k{l}"].append(zeros)
                else:
                    parts[f"q{l}"].append(zeros)
                    parts[f"k{l}"].append((k_b * jnp.exp2(before[l][row, ks] - c_b)).astype(BF16))
        return {nm: jnp.concatenate(parts[nm], axis=0) for nm in names}

    in_block = mdiag_ref[...] != 0.0
    ops = scaled_operands(0)
    for h in range(GLA_HEADS):
        ks = slice(h * dk, (h + 1) * dk)
        vs = slice(h * dv, (h + 1) * dv)
        scores = [_dot_nt(ops["qs"], ops["kinv"])] + [_dot_nt(ops[f"q{l}"], ops[f"k{l}"]) for l in range(nl)]
        v_h = v_ref[:, vs]
        st = st_ref[h]
        o_state = _dot_nt(ops["qt"], st.astype(BF16))
        st_new = jnp.exp2(total[:, ks]) * st + _dot_tn(v_h, ops["kd"])
        if h + 1 < GLA_HEADS:
            ops = scaled_operands(h + 1)
        a = jnp.where(in_block, scores[0], 0.0) + scores[1]
        for l in range(1, nl):
            a = a + jnp.where(mlev_ref[l - 1] != 0.0, scores[l + 1], 0.0)
        st_ref[h] = st_new
        o_ref[:, vs] = (_dot(a.astype(BF16), v_h) + o_state).astype(o_ref.dtype)

    @pl.when(t == nt - 1)
    def _():
        for h in range(GLA_HEADS):
            sout_ref[h] = st_ref[h].T


def _gla_prompt(z3, zag3, wa2, ba, tt):
    b, t, _ = z3.shape
    h = GLA_HEADS
    consts, halves = _gla_constants(tt)
    const_specs = [pl.BlockSpec(c.shape, lambda i, s, nd=c.ndim: (0,) * nd) for c in consts]
    return pl.pallas_call(
        functools.partial(_gla_prompt_kernel, halves=tuple(halves)),
        grid=(b, t // tt),
        in_specs=[
            pl.BlockSpec((None, tt, GLA_DK_TOTAL), lambda i, s: (i, s, OFF_QG // GLA_DK_TOTAL)),
            pl.BlockSpec((None, tt, GLA_DK_TOTAL), lambda i, s: (i, s, OFF_KG // GLA_DK_TOTAL)),
            pl.BlockSpec((None, tt, GLA_DV_TOTAL), lambda i, s: (i, s, OFF_VG // GLA_DV_TOTAL)),
            pl.BlockSpec((None, tt, LANES), lambda i, s: (i, s, 0)),
            pl.BlockSpec((LANES, GLA_DK_TOTAL), lambda i, s: (0, 0)),
            pl.BlockSpec((1, GLA_DK_TOTAL), lambda i, s: (0, 0)),
            *const_specs,
        ],
        out_specs=[
            pl.BlockSpec((None, tt, GLA_DV_TOTAL), lambda i, s: (i, s, 0)),
            pl.BlockSpec((None, h, GLA_DK, GLA_DV), lambda i, s: (i, 0, 0, 0)),
        ],
        out_shape=[
            jax.ShapeDtypeStruct((b, t, GLA_DV_TOTAL), BF16),
            jax.ShapeDtypeStruct((b, h, GLA_DK, GLA_DV), F32),
        ],
        scratch_shapes=[pltpu.VMEM((h, GLA_DV, GLA_DK), F32)],
        compiler_params=_cparams(("arbitrary", "arbitrary")),
        name="gla_prompt",
    )(z3, z3, z3, zag3, wa2, ba, *consts)


def _gla_sample_kernel(q_ref, k_ref, v_ref, ag_ref, wa2_ref, ba_ref, s0_ref, o_ref, s1_ref):
    dk, dv = GLA_DK, GLA_DV
    r_k = lax.broadcasted_iota(jnp.int32, (PAIR_ROWS, dk), 0)
    r_v = lax.broadcasted_iota(jnp.int32, (PAIR_ROWS, dv), 0)
    step_k = r_k & (SAMPLE_T - 1)
    step_v = r_v & (SAMPLE_T - 1)
    odd_k = r_k >= SAMPLE_T
    odd_v = r_v >= SAMPLE_T
    la_all = _log_decay(ag_ref[...], wa2_ref[...], ba_ref[...])
    kpad = jnp.zeros((LANES - PAIR_ROWS, dk), F32)
    vpad = jnp.zeros((LANES - PAIR_ROWS, dv), BF16)
    ones = jnp.ones((LANES, LANES), BF16)

    for h in range(GLA_HEADS):
        la = la_all[:, h * dk:(h + 1) * dk]
        b = la + jnp.where(step_k >= 1, pltpu.roll(la, 1, 0), 0.0)
        b = b + jnp.where(step_k >= 2, pltpu.roll(b, 2, 0), 0.0)
        b_last = jnp.where(odd_k, b[PAIR_ROWS - 1:PAIR_ROWS, :], b[SAMPLE_T - 1:SAMPLE_T, :])
        q = q_ref[:, h * dk:(h + 1) * dk] * (dk ** -0.5)
        k = k_ref[:, h * dk:(h + 1) * dk]
        v = v_ref[:, h * dv:(h + 1) * dv]
        q_i = q * jnp.exp(b)
        k_i = k * jnp.exp(-b)
        k_d = k * jnp.exp(b_last - b)
        o = jnp.sum(q_i * k_i, axis=-1, keepdims=True) * v
        for d in range(1, SAMPLE_T):
            a_d = jnp.sum(q_i * pltpu.roll(k_i, d, 0), axis=-1, keepdims=True)
            o = o + jnp.where(step_v >= d, a_d * pltpu.roll(v, d, 0), 0.0)
        q_b = q_i.astype(BF16)
        vb = jnp.concatenate([v.astype(BF16), vpad], axis=0)
        o_par = []
        for e in range(2):
            s_old = s0_ref[e, h]
            o_par.append(_dot(q_b, s_old.astype(BF16)))
            sel = (r_k >= SAMPLE_T) if e == 1 else (r_k < SAMPLE_T)
            kd_e = jnp.concatenate([jnp.where(sel, k_d, 0.0), kpad], axis=0).astype(BF16)
            la_e = jnp.concatenate([jnp.where(sel, la, 0.0), kpad], axis=0)
            hi, lo = _split_bf16(la_e)
            decay = jnp.exp(_dot_tn(hi, ones) + _dot_tn(lo, ones))
            decay_full = jnp.concatenate([decay] * (dv // LANES), axis=1)
            s1_ref[e, h] = decay_full * s_old + _dot_tn(kd_e, vb)
        o = o + jnp.where(odd_v, o_par[1], o_par[0])
        o_ref[:, h * dv:(h + 1) * dv] = o


def _gla_sample(zs, zag, wa2, ba, state):
    n = zs.shape[0]
    bd, h, dk, dv = state.shape
    state_spec = pl.BlockSpec((2, h, dk, dv), lambda i: (i, 0, 0, 0))
    return pl.pallas_call(
        _gla_sample_kernel,
        grid=(n // PAIR_ROWS,),
        in_specs=[
            pl.BlockSpec((PAIR_ROWS, GLA_DK_TOTAL), lambda i: (i, OFF_QG // GLA_DK_TOTAL)),
            pl.BlockSpec((PAIR_ROWS, GLA_DK_TOTAL), lambda i: (i, OFF_KG // GLA_DK_TOTAL)),
            pl.BlockSpec((PAIR_ROWS, GLA_DV_TOTAL), lambda i: (i, OFF_VG // GLA_DV_TOTAL)),
            pl.BlockSpec((PAIR_ROWS, LANES), lambda i: (i, 0)),
            pl.BlockSpec((LANES, GLA_DK_TOTAL), lambda i: (0, 0)),
            pl.BlockSpec((1, GLA_DK_TOTAL), lambda i: (0, 0)),
            state_spec,
        ],
        out_specs=[
            pl.BlockSpec((PAIR_ROWS, GLA_DV_TOTAL), lambda i: (i, 0)),
            state_spec,
        ],
        out_shape=[
            jax.ShapeDtypeStruct((n, GLA_DV_TOTAL), F32),
            jax.ShapeDtypeStruct(state.shape, F32),
        ],
        compiler_params=_cparams(("arbitrary",)),
        name="gla_sample",
    )(zs, zs, zs, zag, wa2, ba, state)


def _merge_gla_branch(og_ref, rg_ref, gg_ref, gn_ref, pgla_ref):
    gn = gn_ref[...]
    b = None
    for h in range(GLA_HEADS):
        vs = slice(h * GLA_DV, (h + 1) * GLA_DV)
        og = _gla_out(og_ref[:, vs].astype(F32), gn, rg_ref[:, vs].astype(F32)).astype(BF16)
        part = _dot(og, pgla_ref[vs, :])
        b = part if b is None else b + part
    return _sigmoid(gg_ref[...].astype(F32)) * b


def _merge_finish(yb, osw_ref, gs_ref, x_ref, pswa_ref, wo_ref, n2_ref, x1_ref, h2_ref):
    y = _sigmoid(gs_ref[...].astype(F32)) * _dot(osw_ref[...], pswa_ref[...]) + yb
    x1 = x_ref[...] + _dot(y.astype(BF16), wo_ref[...])
    x1_ref[...] = x1
    h2_ref[...] = _rms(x1, n2_ref[...]).astype(BF16)


def _merge_kernel(osw_ref, og_ref, rg_ref, gs_ref, gg_ref, x_ref, gn_ref, pswa_ref, pgla_ref, wo_ref, n2_ref,
                  x1_ref, h2_ref):
    yb = _merge_gla_branch(og_ref, rg_ref, gg_ref, gn_ref, pgla_ref)
    _merge_finish(yb, osw_ref, gs_ref, x_ref, pswa_ref, wo_ref, n2_ref, x1_ref, h2_ref)


def _merge_swa_kernel(sink_ref, q_ref, k_ref, v_ref, c_ref, slo_ref, shi_ref, *refs, tiles_per_batch):
    (og_ref, rg_ref, gs_ref, gg_ref, x_ref, gn_ref, pswa_ref, pgla_ref, wo_ref, n2_ref, x1_ref, h2_ref,
     klast_ref, vlast_ref, kprev_ref, vprev_ref, osw_ref) = refs
    w = WINDOW
    t_local = lax.rem(pl.program_id(0), tiles_per_batch)

    @pl.when(t_local == 0)
    def _():
        kprev_ref[...] = jnp.zeros_like(kprev_ref)
        vprev_ref[...] = jnp.zeros_like(vprev_ref)

    yb = _merge_gla_branch(og_ref, rg_ref, gg_ref, gn_ref, pgla_ref)
    k_prev, v_prev = kprev_ref[...], vprev_ref[...]
    for j in range(q_ref.shape[0] // w):
        rs = slice(j * w, (j + 1) * w)
        tabs = (c_ref[rs, :], slo_ref[rs, :], shi_ref[rs, :])
        has_prev = (t_local > 0) if j == 0 else True
        k_prev, v_prev = _swa_block(has_prev, q_ref[rs, :], k_ref[rs, :], v_ref[rs, :], tabs, sink_ref,
                                    k_prev, v_prev, osw_ref, j * w)
    kprev_ref[...] = k_prev
    vprev_ref[...] = v_prev
    _merge_finish(yb, osw_ref, gs_ref, x_ref, pswa_ref, wo_ref, n2_ref, x1_ref, h2_ref)

    @pl.when(t_local == tiles_per_batch - 1)
    def _():
        klast_ref[...] = kprev_ref[...]
        vlast_ref[...] = vprev_ref[...]


def _merge_specs(tm):
    resident = functools.partial(pl.BlockSpec, pipeline_mode=pl.Buffered(1))
    in_specs = [
        pl.BlockSpec((tm, GLA_DV_TOTAL), lambda i: (i, 0)),
        pl.BlockSpec((tm, GLA_DV_TOTAL), lambda i: (i, OFF_RG // GLA_DV_TOTAL)),
        pl.BlockSpec((tm, D_MODEL), lambda i: (i, OFF_GS // D_MODEL)),
        pl.BlockSpec((tm, D_MODEL), lambda i: (i, OFF_GG // D_MODEL)),
        pl.BlockSpec((tm, D_MODEL), lambda i: (i, 0)),
        pl.BlockSpec((1, GLA_DV), lambda i: (0, 0)),
        resident((SWA_Q, D_MODEL), lambda i: (0, 0)),
        resident((GLA_DV_TOTAL, D_MODEL), lambda i: (0, 0)),
        resident((D_MODEL, D_MODEL), lambda i: (0, 0)),
        pl.BlockSpec((1, D_MODEL), lambda i: (0, 0)),
    ]
    out_specs = [pl.BlockSpec((tm, D_MODEL), lambda i: (i, 0)), pl.BlockSpec((tm, D_MODEL), lambda i: (i, 0))]
    return in_specs, out_specs


def _merge_swa(z, o_gla, x2d, gnorm, p_swa, p_gla, w_o, norm2, sink, tables, batch, tm):
    n = x2d.shape[0]
    w = WINDOW
    tiles_per_batch = n // batch // tm
    in_specs, out_specs = _merge_specs(tm)
    tab_spec = pl.BlockSpec((tm, LANES), lambda i: (lax.rem(i, tiles_per_batch), 0))
    last_spec = pl.BlockSpec((None, w, SWA_KV), lambda i: (i // tiles_per_batch, 0, 0))
    return pl.pallas_call(
        functools.partial(_merge_swa_kernel, tiles_per_batch=tiles_per_batch),
        grid=(n // tm,),
        in_specs=[
            pl.BlockSpec(memory_space=pltpu.SMEM),
            pl.BlockSpec((tm, SWA_Q), lambda i: (i, OFF_QS // SWA_Q)),
            pl.BlockSpec((tm, SWA_KV), lambda i: (i, OFF_KS // SWA_KV)),
            pl.BlockSpec((tm, SWA_KV), lambda i: (i, OFF_VS // SWA_KV)),
            tab_spec, tab_spec, tab_spec,
        ] + in_specs,
        out_specs=out_specs + [last_spec, last_spec],
        out_shape=[
            jax.ShapeDtypeStruct((n, D_MODEL), F32),
            jax.ShapeDtypeStruct((n, D_MODEL), BF16),
            jax.ShapeDtypeStruct((batch, w, SWA_KV), F32),
            jax.ShapeDtypeStruct((batch, w, SWA_KV), F32),
        ],
        scratch_shapes=[pltpu.VMEM((w, SWA_KV), F32), pltpu.VMEM((w, SWA_KV), F32),
                        pltpu.VMEM((tm, SWA_Q), BF16)],
        compiler_params=_cparams(("arbitrary",)),
        name="merge_swa",
    )(sink, z, z, z, *tables, o_gla, z, z, z, x2d, gnorm, p_swa, p_gla, w_o, norm2)


def _merge(o_swa, o_gla, z, x2d, gnorm, p_swa, p_gla, w_o, norm2, tm):
    n = x2d.shape[0]
    in_specs, out_specs = _merge_specs(tm)
    return pl.pallas_call(
        _merge_kernel,
        grid=(n // tm,),
        in_specs=[pl.BlockSpec((tm, SWA_Q), lambda i: (i, 0))] + in_specs,
        out_specs=out_specs,
        out_shape=[
            jax.ShapeDtypeStruct((n, D_MODEL), F32),
            jax.ShapeDtypeStruct((n, D_MODEL), BF16),
        ],
        compiler_params=_cparams(("arbitrary",)),
        name="merge",
    )(o_swa, o_gla, z, z, z, x2d, gnorm, p_swa, p_gla, w_o, norm2)


def _mlp_step(h2_ref, x1_ref, load_wup, load_wdn, fn_ref, out_ref, acc_ref):
    f = pl.program_id(1)
    nf = pl.num_programs(1)

    @pl.when(f == 0)
    def _():
        acc_ref[...] = jnp.zeros_like(acc_ref)

    u = _dot(h2_ref[...], load_wup())
    u = jnp.square(jnp.maximum(u, 0.0)).astype(BF16)
    acc_ref[...] += _dot(u, load_wdn())

    @pl.when(f == nf - 1)
    def _():
        out_ref[...] = _rms(x1_ref[...] + acc_ref[...], fn_ref[...])


def _mlp_kernel(h2_ref, x1_ref, wup_ref, wdn_ref, fn_ref, out_ref, acc_ref):
    _mlp_step(h2_ref, x1_ref, lambda: wup_ref[...], lambda: wdn_ref[...], fn_ref, out_ref, acc_ref)


def _mlp_cast_kernel(h2_ref, x1_ref, wup_ref, wdn_ref, fn_ref, out_ref, wup_bf_ref, wdn_bf_ref, acc_ref):
    def cast_and_keep(src_ref, dst_ref):
        def load():
            w = src_ref[...].astype(BF16)
            dst_ref[...] = w
            return w
        return load

    _mlp_step(h2_ref, x1_ref, cast_and_keep(wup_ref, wup_bf_ref), cast_and_keep(wdn_ref, wdn_bf_ref),
              fn_ref, out_ref, acc_ref)


def _mlp_cast(h2, x1, w_up, w_down, final_norm, tf):
    n = h2.shape[0]
    return pl.pallas_call(
        _mlp_cast_kernel,
        grid=(1, D_FF // tf),
        in_specs=[
            pl.BlockSpec((n, D_MODEL), lambda i, f: (0, 0)),
            pl.BlockSpec((n, D_MODEL), lambda i, f: (0, 0)),
            pl.BlockSpec((None, D_MODEL, tf), lambda i, f: (0, 0, f)),
            pl.BlockSpec((None, tf, D_MODEL), lambda i, f: (0, f, 0)),
            pl.BlockSpec((1, D_MODEL), lambda i, f: (0, 0)),
        ],
        out_specs=[
            pl.BlockSpec((n, D_MODEL), lambda i, f: (0, 0)),
            pl.BlockSpec((D_MODEL, tf), lambda i, f: (0, f)),
            pl.BlockSpec((tf, D_MODEL), lambda i, f: (f, 0)),
        ],
        out_shape=[
            jax.ShapeDtypeStruct((n, D_MODEL), F32),
            jax.ShapeDtypeStruct((D_MODEL, D_FF), BF16),
            jax.ShapeDtypeStruct((D_FF, D_MODEL), BF16),
        ],
        scratch_shapes=[pltpu.VMEM((n, D_MODEL), F32)],
        compiler_params=_cparams(("arbitrary", "arbitrary")),
        name="mlp_cast",
    )(h2, x1, w_up, w_down, final_norm)


def _mlp(h2, x1, w_up, w_down, final_norm, tm, tf):
    n = h2.shape[0]
    return pl.pallas_call(
        _mlp_kernel,
        grid=(n // tm, D_FF // tf),
        in_specs=[
            pl.BlockSpec((tm, D_MODEL), lambda i, f: (i, 0)),
            pl.BlockSpec((tm, D_MODEL), lambda i, f: (i, 0)),
            pl.BlockSpec((D_MODEL, tf), lambda i, f: (0, f)),
            pl.BlockSpec((tf, D_MODEL), lambda i, f: (f, 0)),
            pl.BlockSpec((1, D_MODEL), lambda i, f: (0, 0)),
        ],
        out_specs=pl.BlockSpec((tm, D_MODEL), lambda i, f: (i, 0)),
        out_shape=jax.ShapeDtypeStruct((n, D_MODEL), F32),
        scratch_shapes=[pltpu.VMEM((tm, D_MODEL), F32)],
        compiler_params=_cparams(("arbitrary", "arbitrary")),
        name="mlp",
    )(h2, x1, w_up, w_down, final_norm)


def _pick_tile(n, pref):
    t = min(n, pref)
    while n % t:
        t //= 2
    return t


def _tile_plan(n_prompt, n_sample, t_prompt, b_sample):
    return {
        "proj_rows_p": _pick_tile(n_prompt, 512),
        "swa_pairs": _pick_tile(b_sample // 2, 4),
        "gla_rows": _pick_tile(t_prompt, GLA_TILE),
        "merge_rows_p": _pick_tile(n_prompt, 256), "merge_rows_s": _pick_tile(n_sample, 256),
        "mlp_rows_p": _pick_tile(n_prompt, 512),
        "mlp_ff": 1024, "mlp_cast_ff": 512,
    }


W_IN_SPLITS = (("qs", SWA_Q), ("ks", SWA_KV), ("vs", SWA_KV), ("qg", GLA_DK_TOTAL), ("kg", GLA_DK_TOTAL),
               ("vg", GLA_DV_TOTAL), ("rg", GLA_DV_TOTAL), ("ag", GLA_GATE_RANK), ("gs", D_MODEL), ("gg", D_MODEL))


REORDER_PIECE = HEAD_DIM
REORDER_PIECES = 8


def _w_in_row_table():
    src, o = {}, 0
    for name, width in W_IN_SPLITS:
        src[name] = o
        o += width
    rows = np.zeros(Z_COLS, np.int64)
    for name, dst in (("gs", OFF_GS), ("gg", OFF_GG), ("vg", OFF_VG), ("rg", OFF_RG), ("qg", OFF_QG),
                      ("kg", OFF_KG), ("ks", OFF_KS), ("vs", OFF_VS)):
        width = dict(W_IN_SPLITS)[name]
        rows[dst:dst + width] = src[name] + np.arange(width)
    for h in range(GROUP):
        for g in range(N_KV_HEADS):
            dst = OFF_QS + h * SWA_KV + g * HEAD_DIM
            rows[dst:dst + HEAD_DIM] = src["qs"] + (g * GROUP + h) * HEAD_DIM + np.arange(HEAD_DIM)
    pieces = rows.reshape(-1, REORDER_PIECE)
    assert (pieces == pieces[:, :1] + np.arange(REORDER_PIECE)).all()
    return jnp.asarray(pieces[:, 0], jnp.int32), src["ag"]


def _reorder_w_in_kernel(tbl_ref, *refs):
    piece_refs = refs[:REORDER_PIECES]
    ag_ref, x_ref, nw_ref, wm_ref, wag_ref, z_ref, zag_ref, h_ref = refs[REORDER_PIECES:]

    @pl.when(pl.program_id(0) == 0)
    def _():
        pad = jnp.zeros((LANES - GLA_GATE_RANK, wag_ref.shape[1]), BF16)
        wag = jnp.concatenate([ag_ref[0].astype(BF16), pad], axis=0)
        wag_ref[...] = wag
        h = _rms(x_ref[...], nw_ref[...]).astype(BF16)
        h_ref[...] = h
        zag_ref[...] = _dot_nt(h, wag)

    tile = jnp.concatenate([p[0].astype(BF16) for p in piece_refs], axis=0)
    wm_ref[...] = tile
    z_ref[...] = _dot_nt(h_ref[...], tile)


def _reorder_w_in(w_in_t, x_rows, norm1):
    d = w_in_t.shape[2]
    n = x_rows.shape[0]
    table, ag_row = _w_in_row_table()
    rows = REORDER_PIECE * REORDER_PIECES
    piece = lambda k: pl.BlockSpec((pl.Element(1), pl.Element(REORDER_PIECE), pl.Element(d)),
                                   lambda i, tbl: (0, pl.multiple_of(tbl[i * REORDER_PIECES + k], GLA_GATE_RANK), 0))
    grid_spec = pltpu.PrefetchScalarGridSpec(
        num_scalar_prefetch=1,
        grid=(Z_COLS // rows,),
        in_specs=[piece(k) for k in range(REORDER_PIECES)]
        + [pl.BlockSpec((pl.Element(1), pl.Element(GLA_GATE_RANK), pl.Element(d)),
                        lambda i, tbl: (0, ag_row, 0)),
           pl.BlockSpec((n, d), lambda i, tbl: (0, 0)),
           pl.BlockSpec((1, d), lambda i, tbl: (0, 0))],
        out_specs=[pl.BlockSpec((rows, d), lambda i, tbl: (i, 0)),
                   pl.BlockSpec((LANES, d), lambda i, tbl: (0, 0)),
                   pl.BlockSpec((n, rows), lambda i, tbl: (0, i)),
                   pl.BlockSpec((n, LANES), lambda i, tbl: (0, 0))],
        scratch_shapes=[pltpu.VMEM((n, d), BF16)],
    )
    return pl.pallas_call(
        _reorder_w_in_kernel,
        grid_spec=grid_spec,
        out_shape=[jax.ShapeDtypeStruct((Z_COLS, d), BF16), jax.ShapeDtypeStruct((LANES, d), BF16),
                   jax.ShapeDtypeStruct((n, Z_COLS), F32), jax.ShapeDtypeStruct((n, LANES), F32)],
        compiler_params=_cparams(("arbitrary",)),
        name="reorder_w_in",
    )(table, *([w_in_t] * (REORDER_PIECES + 1)), x_rows, norm1)


def kernel(x_prompt, x_sample, cache_swa_k, cache_swa_v, state_gla, norm1, w_in, w_a2, b_a, sink,
           gla_norm, p_swa, p_gla, w_o, norm2, w_up, w_down, final_norm):
    assert norm1.shape[0] == 1, "single-layer stack"
    bp, tp, d = x_prompt.shape
    bs, ts, _ = x_sample.shape
    assert ts == SAMPLE_T and bs % 2 == 0 and tp % WINDOW == 0
    w_buf = cache_swa_k.shape[2]
    assert w_buf == WINDOW

    wa2 = jnp.pad(w_a2[0], ((0, LANES - GLA_GATE_RANK), (0, 0))).astype(BF16)
    ba = b_a[0][None, :]
    n1, n2, fn = norm1[0][None, :], norm2[0][None, :], final_norm[None, :]
    gn = gla_norm[0][None, :]
    pswa = p_swa[0].reshape(N_KV_HEADS, GROUP, HEAD_DIM, d).transpose(1, 0, 2, 3).reshape(SWA_Q, d).astype(BF16)
    pgla, wo = p_gla[0].astype(BF16), w_o[0].astype(BF16)
    sink_smem = sink[0][None, :]
    sink_rows = jnp.broadcast_to(jnp.repeat(sink[0], PAIR_ROWS)[:, None], (N_HEADS * PAIR_ROWS, LANES))

    xp = x_prompt.reshape(bp * tp, d)
    xs = x_sample.reshape(bs * ts, d)
    np_, ns = xp.shape[0], xs.shape[0]

    tiles = _tile_plan(np_, ns, tp, bs)

    w_main, w_ag, zs, zags = _reorder_w_in(jnp.swapaxes(w_in, 1, 2), xs, n1)
    zp, zagp = _in_proj(xp, n1, w_main, w_ag, tiles["proj_rows_p"], Z_COLS // 2, BF16)

    zp3 = zp.reshape(bp, tp, Z_COLS)
    pos_s = PAST_LEN + jnp.arange(ts)
    tabs_s = _rope_tables(jnp.concatenate([pos_s, pos_s]))
    pos_minor = lambda c: jnp.transpose(c[0], (0, 2, 3, 1)).reshape(bs, SWA_KV, w_buf)
    pos_major = lambda c: jnp.transpose(c.reshape(bs, N_KV_HEADS, HEAD_DIM, w_buf), (0, 3, 1, 2))[None]
    o_swa_s, nk_s, nv_s = _swa_sample(zs, pos_minor(cache_swa_k), pos_minor(cache_swa_v), sink_rows, tabs_s,
                                      tiles["swa_pairs"])

    o_gla_p, s_p = _gla_prompt(zp3, zagp.reshape(bp, tp, LANES), wa2, ba, tiles["gla_rows"])
    o_gla_s, s_s = _gla_sample(zs, zags, wa2, ba, state_gla[0])

    x1p, h2p, k_last, v_last = _merge_swa(zp, o_gla_p.reshape(np_, GLA_DV_TOTAL), xp, gn, pswa, pgla, wo, n2,
                                          sink_smem, _rope_tables(jnp.arange(tp)), bp, tiles["merge_rows_p"])
    x1s, h2s = _merge(o_swa_s, o_gla_s, zs, xs, gn, pswa, pgla, wo, n2, tiles["merge_rows_s"])
    ys, wup, wdn = _mlp_cast(h2s, x1s, w_up, w_down, fn, tiles["mlp_cast_ff"])
    yp = _mlp(h2p, x1p, wup, wdn, fn, tiles["mlp_rows_p"], tiles["mlp_ff"])

    kv5 = lambda a, nb: a.reshape(1, nb, w_buf, N_KV_HEADS, HEAD_DIM)
    return (yp.reshape(bp, tp, d), ys.reshape(bs, ts, d),
            kv5(k_last, bp), kv5(v_last, bp), s_p[None],
            pos_major(nk_s), pos_major(nv_s), s_s[None])
```

```python
import functools

import jax
import jax.numpy as jnp
import numpy as np
from jax import lax
from jax.experimental import pallas as pl
from jax.experimental.pallas import tpu as pltpu

F32 = jnp.float32
BF16 = jnp.bfloat16

D_MODEL = 2048
PAST_LEN = 8192
N_HEADS = 16
N_KV_HEADS = 4
GROUP = N_HEADS // N_KV_HEADS
HEAD_DIM = 64
WINDOW = 128
ROT_DIM = HEAD_DIM // 4
ROPE_THETA = 500000.0
SWA_Q = N_HEADS * HEAD_DIM
SWA_KV = N_KV_HEADS * HEAD_DIM
GLA_HEADS = 4
GLA_DK = 256
GLA_DV = 512
GLA_DK_TOTAL = GLA_HEADS * GLA_DK
GLA_DV_TOTAL = GLA_HEADS * GLA_DV
GLA_GATE_RANK = 16
GLA_GATE_NORM = 16.0
D_FF = 4 * D_MODEL
EPS = 1e-6
LOG2E = 1.4426950408889634

LANES = 128
VMEM_LIMIT = 56 * 1024 * 1024

OFF_GS = 0
OFF_GG = OFF_GS + D_MODEL
OFF_VG = OFF_GG + D_MODEL
OFF_RG = OFF_VG + GLA_DV_TOTAL
OFF_QG = OFF_RG + GLA_DV_TOTAL
OFF_KG = OFF_QG + GLA_DK_TOTAL
OFF_QS = OFF_KG + GLA_DK_TOTAL
OFF_KS = OFF_QS + SWA_Q
OFF_VS = OFF_KS + SWA_KV
Z_COLS = OFF_VS + SWA_KV

GLA_BLOCK = 16
GLA_TILE = 256


def _cparams(sem):
    return pltpu.CompilerParams(dimension_semantics=sem, vmem_limit_bytes=VMEM_LIMIT)


def _rms(x, w):
    return x * lax.rsqrt(jnp.mean(x * x, axis=-1, keepdims=True) + EPS) * w


def _sigmoid(x):
    return 1.0 / (1.0 + jnp.exp(-x))


def _dot(a, b):
    return jnp.dot(a, b, preferred_element_type=F32)


def _dot_nt(a, b):
    return lax.dot_general(a, b, (((1,), (1,)), ((), ())), preferred_element_type=F32)


def _dot_tn(a, b):
    return lax.dot_general(a, b, (((0,), (0,)), ((), ())), preferred_element_type=F32)


def _split_bf16(x):
    hi = x.astype(BF16)
    lo = (x - hi.astype(F32)).astype(BF16)
    return hi, lo


def _in_proj_kernel(x_ref, nw_ref, w_ref, wag_ref, z_ref, zag_ref):
    h = _rms(x_ref[...], nw_ref[...]).astype(BF16)
    z_ref[...] = _dot_nt(h, w_ref[...]).astype(z_ref.dtype)

    @pl.when(pl.program_id(0) == 0)
    def _():
        zag_ref[...] = _dot_nt(h, wag_ref[...])


def _in_proj(x2d, norm1, w_main, w_ag, tm, tn, z_dtype):
    n = x2d.shape[0]
    nrow = n // tm
    w_mode = {"pipeline_mode": pl.Buffered(1)}
    return pl.pallas_call(
        _in_proj_kernel,
        grid=(Z_COLS // tn, nrow),
        in_specs=[
            pl.BlockSpec((tm, D_MODEL), lambda j, i: (i, 0)),
            pl.BlockSpec((1, D_MODEL), lambda j, i: (0, 0)),
            pl.BlockSpec((tn, D_MODEL), lambda j, i: (j, 0), **w_mode),
            pl.BlockSpec((LANES, D_MODEL), lambda j, i: (0, 0)),
        ],
        out_specs=[
            pl.BlockSpec((tm, tn), lambda j, i: (i, j)),
            pl.BlockSpec((tm, LANES), lambda j, i: (jnp.where(j == 0, i, nrow - 1), 0)),
        ],
        out_shape=[
            jax.ShapeDtypeStruct((n, Z_COLS), z_dtype),
            jax.ShapeDtypeStruct((n, LANES), F32),
        ],
        compiler_params=_cparams(("arbitrary", "arbitrary")),
        name="in_proj",
    )(x2d, norm1, w_main, w_ag)


def _rope_tables(pos):
    half = ROT_DIM // 2
    inv = ROPE_THETA ** (-jnp.arange(half, dtype=F32) * 2.0 / ROT_DIM)
    ang = pos.astype(F32)[:, None] * inv[None, :]
    cos, sin = jnp.cos(ang), jnp.sin(ang)
    t = pos.shape[0]
    ones = jnp.ones((t, HEAD_DIM - ROT_DIM), F32)
    zeros = jnp.zeros((t, HEAD_DIM - ROT_DIM), F32)
    zh = jnp.zeros((t, half), F32)
    c = jnp.concatenate([cos, cos, ones], axis=1)
    s_lo = jnp.concatenate([zh, sin, zeros], axis=1)
    s_hi = jnp.concatenate([-sin, zh, zeros], axis=1)
    rep = LANES // HEAD_DIM
    return tuple(jnp.tile(a, (1, rep)) for a in (c, s_lo, s_hi))


def _rope(x, c, s_lo, s_hi):
    half = ROT_DIM // 2
    outs = []
    for j in range(x.shape[1] // LANES):
        xc = x[:, j * LANES:(j + 1) * LANES]
        outs.append(xc * c + pltpu.roll(xc, half, 1) * s_lo + pltpu.roll(xc, LANES - half, 1) * s_hi)
    return outs[0] if len(outs) == 1 else jnp.concatenate(outs, axis=1)


def _swa_block(has_prev, q_blk, k_blk, v_blk, tabs, sink_ref, k_prev, v_prev, o_ref, row0):
    w = WINDOW
    kvw = SWA_KV
    c, s_lo, s_hi = tabs
    q = (_rope(q_blk.astype(F32), c, s_lo, s_hi) * (HEAD_DIM ** -0.5 * LOG2E)).astype(BF16)
    k = _rope(k_blk.astype(F32), c, s_lo, s_hi)
    v = v_blk.astype(F32)
    kc = jnp.concatenate([k_prev, k], axis=0).astype(BF16)
    vc = jnp.concatenate([v_prev, v], axis=0).astype(BF16)

    rows = lax.broadcasted_iota(jnp.int32, (w, 2 * w), 0)
    cols = lax.broadcasted_iota(jnp.int32, (w, 2 * w), 1)
    diff = rows + w - cols
    bias = jnp.where((diff >= 0) & (diff < w) & ((cols >= w) | has_prev), 0.0, -jnp.inf)
    bias = jnp.concatenate([bias] * GROUP, axis=0)
    head_shift = HEAD_DIM.bit_length() - 1
    lane_head_q = lax.broadcasted_iota(jnp.int32, (w, kvw), 1) >> head_shift
    lane_head = lax.broadcasted_iota(jnp.int32, (2 * w, kvw), 1) >> head_shift

    for g in range(N_KV_HEADS):
        in_g_q = jnp.where(lane_head_q == g, 1.0, 0.0).astype(BF16)
        in_g = jnp.where(lane_head == g, 1.0, 0.0).astype(BF16)
        qg = jnp.concatenate([q[:, h * kvw:(h + 1) * kvw] * in_g_q for h in range(GROUP)], axis=0)
        sk = jnp.concatenate([jnp.full((w, LANES), sink_ref[0, g * GROUP + h] * LOG2E, F32)
                              for h in range(GROUP)], axis=0)
        s = _dot_nt(qg, kc) + bias
        m = jnp.maximum(jnp.broadcast_to(jnp.max(s, axis=-1, keepdims=True), sk.shape), sk)
        p = jnp.exp2(s - jnp.concatenate([m, m], axis=1)).astype(BF16)
        vg = vc * in_g + (1.0 - in_g)
        oa = _dot(p, vg)
        p_sink = jnp.exp2(sk - m)
        denom = pltpu.roll(oa, 2 * HEAD_DIM, 1) + jnp.concatenate([p_sink, p_sink], axis=1)
        o = (oa / denom).astype(o_ref.dtype)
        for h in range(GROUP):
            lo = g * HEAD_DIM
            o_ref[row0:row0 + w, h * kvw + lo:h * kvw + lo + HEAD_DIM] = o[h * w:(h + 1) * w, lo:lo + HEAD_DIM]
    return k, v


SAMPLE_T = 4
PAIR_ROWS = 2 * SAMPLE_T


def _swa_sample_kernel(q_ref, k_ref, v_ref, ck_ref, cv_ref, c_ref, slo_ref, shi_ref, sink_ref,
                       o_ref, nk_ref, nv_ref, *, pairs):
    w = WINDOW
    kvw = SWA_KV
    c, s_lo, s_hi = c_ref[...], slo_ref[...], shi_ref[...]
    nrow = N_HEADS * PAIR_ROWS
    row = lax.broadcasted_iota(jnp.int32, (nrow, w), 0)
    col = lax.broadcasted_iota(jnp.int32, (nrow, w), 1)
    t_shift = SAMPLE_T.bit_length() - 1
    sq = row & (SAMPLE_T - 1)
    par = (row >> t_shift) & 1
    mask_cache = col > sq
    mask_new = (col < PAIR_ROWS) & ((col >> t_shift) == par) & ((col & (SAMPLE_T - 1)) <= sq)
    lane = lax.broadcasted_iota(jnp.int32, (PAIR_ROWS, kvw), 1) >> (HEAD_DIM.bit_length() - 1)
    pos = lax.broadcasted_iota(jnp.int32, (kvw, w), 1)
    sk = sink_ref[...][:, :1]
    zpad = jnp.zeros((w - PAIR_ROWS, kvw), F32)

    for p in range(pairs):
        rs = slice(p * PAIR_ROWS, (p + 1) * PAIR_ROWS)
        q8 = _rope(q_ref[rs, :], c, s_lo, s_hi) * (HEAD_DIM ** -0.5)
        k8 = _rope(k_ref[rs, :], c, s_lo, s_hi)
        v8 = v_ref[rs, :]
        blocks = [jnp.where(lane == g, q8[:, h * kvw:(h + 1) * kvw], 0.0)
                  for g in range(N_KV_HEADS) for h in range(GROUP)]
        qall = jnp.concatenate(blocks, axis=0).astype(BF16)

        k_new = jnp.concatenate([k8, zpad], axis=0)
        v_new = jnp.concatenate([v8, zpad], axis=0)
        s_cache = [_dot(qall, ck_ref[2 * p + e].astype(BF16)) for e in range(2)]
        s_cache = jnp.where(mask_cache, jnp.where(par == 1, s_cache[1], s_cache[0]), -jnp.inf)
        s_new = jnp.where(mask_new, _dot_nt(qall, k_new.astype(BF16)), -jnp.inf)
        s = jnp.concatenate([s_cache, s_new], axis=1)
        m = jnp.maximum(jnp.max(s, axis=-1, keepdims=True), sk)
        pr = jnp.exp(s - m)
        denom = jnp.sum(pr, axis=-1, keepdims=True) + jnp.exp(sk - m)
        p_cache, p_new = pr[:, :w], pr[:, w:]
        p0 = jnp.where(par == 0, p_cache, 0.0).astype(BF16)
        p1 = jnp.where(par == 1, p_cache, 0.0).astype(BF16)
        oall = (_dot_nt(p0, cv_ref[2 * p].astype(BF16)) + _dot_nt(p1, cv_ref[2 * p + 1].astype(BF16))
                + _dot(p_new.astype(BF16), v_new.astype(BF16))) / denom

        outs = []
        for h in range(GROUP):
            acc = None
            for g in range(N_KV_HEADS):
                hh = g * GROUP + h
                blk = jnp.where(lane == g, oall[hh * PAIR_ROWS:(hh + 1) * PAIR_ROWS], 0.0)
                acc = blk if acc is None else acc + blk
            outs.append(acc)
        o_ref[rs, :] = jnp.concatenate(outs, axis=1).astype(o_ref.dtype)

        for new, cref, nref in ((k_new, ck_ref, nk_ref), (v_new, cv_ref, nv_ref)):
            new_t = new.T
            for e in range(2):
                bd = 2 * p + e
                shifted = pltpu.roll(cref[bd], w - SAMPLE_T, 1)
                tail = pltpu.roll(new_t, w - SAMPLE_T - e * SAMPLE_T, 1)
                nref[bd] = jnp.where(pos >= w - SAMPLE_T, tail, shifted)


def _swa_sample(zs, cache_k, cache_v, sink_rows, tables, pairs):
    n = zs.shape[0]
    bd, kvw, w = cache_k.shape
    rows = pairs * PAIR_ROWS
    tab_spec = pl.BlockSpec((PAIR_ROWS, LANES), lambda i: (0, 0))
    cache_spec = pl.BlockSpec((2 * pairs, kvw, w), lambda i: (i, 0, 0))
    return pl.pallas_call(
        functools.partial(_swa_sample_kernel, pairs=pairs),
        grid=(n // rows,),
        in_specs=[
            pl.BlockSpec((rows, SWA_Q), lambda i: (i, OFF_QS // SWA_Q)),
            pl.BlockSpec((rows, SWA_KV), lambda i: (i, OFF_KS // SWA_KV)),
            pl.BlockSpec((rows, SWA_KV), lambda i: (i, OFF_VS // SWA_KV)),
            cache_spec, cache_spec,
            tab_spec, tab_spec, tab_spec,
            pl.BlockSpec((N_HEADS * PAIR_ROWS, LANES), lambda i: (0, 0)),
        ],
        out_specs=[
            pl.BlockSpec((rows, SWA_Q), lambda i: (i, 0)),
            cache_spec, cache_spec,
        ],
        out_shape=[
            jax.ShapeDtypeStruct((n, SWA_Q), BF16),
            jax.ShapeDtypeStruct((bd, kvw, w), F32),
            jax.ShapeDtypeStruct((bd, kvw, w), F32),
        ],
        compiler_params=_cparams(("arbitrary",)),
        name="swa_sample",
    )(zs, zs, zs, cache_k, cache_v, *tables, sink_rows)


def _log_decay(ag, wa2, ba):
    x = _dot(ag.astype(BF16), wa2) + ba
    log_sig = jnp.minimum(x, 0.0) - jnp.log(1.0 + jnp.exp(-jnp.abs(x)))
    return log_sig * (1.0 / GLA_GATE_NORM)


def _gla_out(o, gnorm, rg):
    dv = o.shape[-1]
    ms = jnp.broadcast_to(jnp.sum(o * o, axis=-1, keepdims=True), (o.shape[0], LANES)) * (1.0 / dv)
    r = lax.rsqrt(ms + EPS)
    half = rg * 0.5
    gate = half + half * jnp.tanh(half)
    return o * jnp.concatenate([r] * (dv // LANES), axis=1) * gnorm * gate


def _gla_constants(tt):
    nb = tt // GLA_BLOCK
    halves = [tt >> (l + 1) for l in range(nb.bit_length() - 1)]
    step = np.arange(tt)
    blk = step // GLA_BLOCK
    t16 = ((blk[:, None] == blk[None, :]) & (step[None, :] <= step[:, None])).astype(np.float32)
    rows = [(blk[None, :] < np.arange(nb)[:, None])]
    for hs in halves:
        t_b = (np.arange(nb) * GLA_BLOCK) // (2 * hs) * (2 * hs) + hs
        rows.append(step[None, :] < t_b[:, None])
    rows.append(np.ones((GLA_BLOCK, tt), bool))
    sel = np.concatenate(rows, axis=0).astype(np.float32)
    group = [(step[:, None] // (2 * hs)) == (step[None, :] // (2 * hs)) for hs in halves[1:]]
    mlev = np.stack(group).astype(np.float32)
    return (jnp.asarray(t16, BF16), jnp.asarray(sel, BF16), jnp.asarray(t16), jnp.asarray(mlev)), halves


def _gla_prompt_kernel(q_ref, k_ref, v_ref, ag_ref, wa2_ref, ba_ref,
                       t16_ref, sel_ref, mdiag_ref, mlev_ref, o_ref, sout_ref, st_ref, *, halves):
    t = pl.program_id(1)
    nt = pl.num_programs(1)
    tt = q_ref.shape[0]
    nb = tt // GLA_BLOCK
    nl = len(halves)
    dk, dv = GLA_DK, GLA_DV

    @pl.when(t == 0)
    def _():
        st_ref[...] = jnp.zeros_like(st_ref)

    la = _log_decay(ag_ref[...], wa2_ref[...], ba_ref[...]) * LOG2E
    hi, lo = _split_bf16(la)
    c = _dot(t16_ref[...], hi) + _dot(t16_ref[...], lo)
    cum = _dot(sel_ref[...], hi) + _dot(sel_ref[...], lo)
    p_start = cum[:nb]
    total = cum[(1 + nl) * nb:(1 + nl) * nb + 1]
    exp_p = jnp.exp2(p_start)
    d_last = total - p_start
    after, before = [], []
    for l in range(nl):
        d = p_start - cum[(1 + l) * nb:(2 + l) * nb]
        after.append(jnp.exp2(jnp.minimum(d, 0.0)))
        before.append(-d)

    zeros = jnp.zeros((GLA_BLOCK, dk), BF16)
    names = ["qs", "kinv", "qt", "kd"] + [f"q{l}" for l in range(nl)] + [f"k{l}" for l in range(nl)]

    def scaled_operands(h):
        ks = slice(h * dk, (h + 1) * dk)
        parts = {nm: [] for nm in names}
        for i in range(nb):
            rs = slice(i * GLA_BLOCK, (i + 1) * GLA_BLOCK)
            row = slice(i, i + 1)
            c_b = c[rs, ks]
            k_b = k_ref[rs, ks].astype(F32)
            qs = q_ref[rs, ks].astype(F32) * (dk ** -0.5) * jnp.exp2(c_b)
            parts["qs"].append(qs.astype(BF16))
            parts["kinv"].append((k_b * jnp.exp2(-c_b)).astype(BF16))
            parts["qt"].append((qs * exp_p[row, ks]).astype(BF16))
            parts["kd"].append((k_b * jnp.exp2(d_last[row, ks] - c_b)).astype(BF16))
            for l, hs in enumerate(halves):
                if (i * GLA_BLOCK) % (2 * hs) >= hs:
                    parts[f"q{l}"].append((qs * after[l][row, ks]).astype(BF16))
                    parts[f"k{l}"].append(zeros)
                else:
                    parts[f"q{l}"].append(zeros)
                    parts[f"k{l}"].append((k_b * jnp.exp2(before[l][row, ks] - c_b)).astype(BF16))
        return {nm: jnp.concatenate(parts[nm], axis=0) for nm in names}

    in_block = mdiag_ref[...] != 0.0
    ops = scaled_operands(0)
    for h in range(GLA_HEADS):
        ks = slice(h * dk, (h + 1) * dk)
        vs = slice(h * dv, (h + 1) * dv)
        scores = [_dot_nt(ops["qs"], ops["kinv"])] + [_dot_nt(ops[f"q{l}"], ops[f"k{l}"]) for l in range(nl)]
        v_h = v_ref[:, vs]
        st = st_ref[h]
        o_state = _dot_nt(ops["qt"], st.astype(BF16))
        st_new = jnp.exp2(total[:, ks]) * st + _dot_tn(v_h, ops["kd"])
        if h + 1 < GLA_HEADS:
            ops = scaled_operands(h + 1)
        a = jnp.where(in_block, scores[0], 0.0) + scores[1]
        for l in range(1, nl):
            a = a + jnp.where(mlev_ref[l - 1] != 0.0, scores[l + 1], 0.0)
        st_ref[h] = st_new
        o_ref[:, vs] = (_dot(a.astype(BF16), v_h) + o_state).astype(o_ref.dtype)

    @pl.when(t == nt - 1)
    def _():
        for h in range(GLA_HEADS):
            sout_ref[h] = st_ref[h].T


def _gla_prompt(z3, zag3, wa2, ba, tt):
    b, t, _ = z3.shape
    h = GLA_HEADS
    consts, halves = _gla_constants(tt)
    const_specs = [pl.BlockSpec(c.shape, lambda i, s, nd=c.ndim: (0,) * nd) for c in consts]
    return pl.pallas_call(
        functools.partial(_gla_prompt_kernel, halves=tuple(halves)),
        grid=(b, t // tt),
        in_specs=[
            pl.BlockSpec((None, tt, GLA_DK_TOTAL), lambda i, s: (i, s, OFF_QG // GLA_DK_TOTAL)),
            pl.BlockSpec((None, tt, GLA_DK_TOTAL), lambda i, s: (i, s, OFF_KG // GLA_DK_TOTAL)),
            pl.BlockSpec((None, tt, GLA_DV_TOTAL), lambda i, s: (i, s, OFF_VG // GLA_DV_TOTAL)),
            pl.BlockSpec((None, tt, LANES), lambda i, s: (i, s, 0)),
            pl.BlockSpec((LANES, GLA_DK_TOTAL), lambda i, s: (0, 0)),
            pl.BlockSpec((1, GLA_DK_TOTAL), lambda i, s: (0, 0)),
            *const_specs,
        ],
        out_specs=[
            pl.BlockSpec((None, tt, GLA_DV_TOTAL), lambda i, s: (i, s, 0)),
            pl.BlockSpec((None, h, GLA_DK, GLA_DV), lambda i, s: (i, 0, 0, 0)),
        ],
        out_shape=[
            jax.ShapeDtypeStruct((b, t, GLA_DV_TOTAL), BF16),
            jax.ShapeDtypeStruct((b, h, GLA_DK, GLA_DV), F32),
        ],
        scratch_shapes=[pltpu.VMEM((h, GLA_DV, GLA_DK), F32)],
        compiler_params=_cparams(("arbitrary", "arbitrary")),
        name="gla_prompt",
    )(z3, z3, z3, zag3, wa2, ba, *consts)


def _gla_sample_kernel(q_ref, k_ref, v_ref, ag_ref, wa2_ref, ba_ref, s0_ref, o_ref, s1_ref, *, pairs):
    dk, dv = GLA_DK, GLA_DV
    r_k = lax.broadcasted_iota(jnp.int32, (PAIR_ROWS, dk), 0)
    r_v = lax.broadcasted_iota(jnp.int32, (PAIR_ROWS, dv), 0)
    step_k = r_k & (SAMPLE_T - 1)
    step_v = r_v & (SAMPLE_T - 1)
    odd_k = r_k >= SAMPLE_T
    odd_v = r_v >= SAMPLE_T
    la_all = _log_decay(ag_ref[...], wa2_ref[...], ba_ref[...])
    kpad = jnp.zeros((LANES - PAIR_ROWS, dk), F32)
    vpad = jnp.zeros((LANES - PAIR_ROWS, dv), BF16)
    ones = jnp.ones((LANES, LANES), BF16)

    for p, h in [(p, h) for p in range(pairs) for h in range(GLA_HEADS)]:
        rs = slice(p * PAIR_ROWS, (p + 1) * PAIR_ROWS)
        la = la_all[rs, h * dk:(h + 1) * dk]
        b = la + jnp.where(step_k >= 1, pltpu.roll(la, 1, 0), 0.0)
        b = b + jnp.where(step_k >= 2, pltpu.roll(b, 2, 0), 0.0)
        b_last = jnp.where(odd_k, b[PAIR_ROWS - 1:PAIR_ROWS, :], b[SAMPLE_T - 1:SAMPLE_T, :])
        q = q_ref[rs, h * dk:(h + 1) * dk] * (dk ** -0.5)
        k = k_ref[rs, h * dk:(h + 1) * dk]
        v = v_ref[rs, h * dv:(h + 1) * dv]
        q_i = q * jnp.exp(b)
        k_i = k * jnp.exp(-b)
        k_d = k * jnp.exp(b_last - b)
        o = jnp.sum(q_i * k_i, axis=-1, keepdims=True) * v
        for d in range(1, SAMPLE_T):
            a_d = jnp.sum(q_i * pltpu.roll(k_i, d, 0), axis=-1, keepdims=True)
            o = o + jnp.where(step_v >= d, a_d * pltpu.roll(v, d, 0), 0.0)
        q_b = q_i.astype(BF16)
        vb = jnp.concatenate([v.astype(BF16), vpad], axis=0)
        o_par = []
        for e in range(2):
            s_old = s0_ref[2 * p + e, h]
            o_par.append(_dot(q_b, s_old.astype(BF16)))
            sel = (r_k >= SAMPLE_T) if e == 1 else (r_k < SAMPLE_T)
            kd_e = jnp.concatenate([jnp.where(sel, k_d, 0.0), kpad], axis=0).astype(BF16)
            la_e = jnp.concatenate([jnp.where(sel, la, 0.0), kpad], axis=0)
            hi, lo = _split_bf16(la_e)
            decay = jnp.exp(_dot_tn(hi, ones) + _dot_tn(lo, ones))
            decay_full = jnp.concatenate([decay] * (dv // LANES), axis=1)
            s1_ref[2 * p + e, h] = decay_full * s_old + _dot_tn(kd_e, vb)
        o = o + jnp.where(odd_v, o_par[1], o_par[0])
        o_ref[rs, h * dv:(h + 1) * dv] = o


def _gla_sample(zs, zag, wa2, ba, state, pairs):
    n = zs.shape[0]
    bd, h, dk, dv = state.shape
    rows = pairs * PAIR_ROWS
    state_spec = pl.BlockSpec((2 * pairs, h, dk, dv), lambda i: (i, 0, 0, 0))
    return pl.pallas_call(
        functools.partial(_gla_sample_kernel, pairs=pairs),
        grid=(n // rows,),
        in_specs=[
            pl.BlockSpec((rows, GLA_DK_TOTAL), lambda i: (i, OFF_QG // GLA_DK_TOTAL)),
            pl.BlockSpec((rows, GLA_DK_TOTAL), lambda i: (i, OFF_KG // GLA_DK_TOTAL)),
            pl.BlockSpec((rows, GLA_DV_TOTAL), lambda i: (i, OFF_VG // GLA_DV_TOTAL)),
            pl.BlockSpec((rows, LANES), lambda i: (i, 0)),
            pl.BlockSpec((LANES, GLA_DK_TOTAL), lambda i: (0, 0)),
            pl.BlockSpec((1, GLA_DK_TOTAL), lambda i: (0, 0)),
            state_spec,
        ],
        out_specs=[
            pl.BlockSpec((rows, GLA_DV_TOTAL), lambda i: (i, 0)),
            state_spec,
        ],
        out_shape=[
            jax.ShapeDtypeStruct((n, GLA_DV_TOTAL), F32),
            jax.ShapeDtypeStruct(state.shape, F32),
        ],
        compiler_params=_cparams(("arbitrary",)),
        name="gla_sample",
    )(zs, zs, zs, zag, wa2, ba, state)


def _merge_gla_branch(og_ref, rg_ref, gg_ref, gn_ref, pgla_ref):
    gn = gn_ref[...]
    b = None
    for h in range(GLA_HEADS):
        vs = slice(h * GLA_DV, (h + 1) * GLA_DV)
        og = _gla_out(og_ref[:, vs].astype(F32), gn, rg_ref[:, vs].astype(F32)).astype(BF16)
        part = _dot(og, pgla_ref[vs, :])
        b = part if b is None else b + part
    return _sigmoid(gg_ref[...].astype(F32)) * b


def _merge_finish(yb, osw_ref, gs_ref, x_ref, pswa_ref, wo_ref, n2_ref, x1_ref, h2_ref):
    y = _sigmoid(gs_ref[...].astype(F32)) * _dot(osw_ref[...], pswa_ref[...]) + yb
    x1 = x_ref[...] + _dot(y.astype(BF16), wo_ref[...])
    x1_ref[...] = x1
    h2_ref[...] = _rms(x1, n2_ref[...]).astype(BF16)


def _merge_kernel(osw_ref, og_ref, rg_ref, gs_ref, gg_ref, x_ref, gn_ref, pswa_ref, pgla_ref, wo_ref, n2_ref,
                  x1_ref, h2_ref):
    yb = _merge_gla_branch(og_ref, rg_ref, gg_ref, gn_ref, pgla_ref)
    _merge_finish(yb, osw_ref, gs_ref, x_ref, pswa_ref, wo_ref, n2_ref, x1_ref, h2_ref)


def _merge_swa_kernel(sink_ref, q_ref, k_ref, v_ref, c_ref, slo_ref, shi_ref, *refs, tiles_per_batch):
    (og_ref, rg_ref, gs_ref, gg_ref, x_ref, gn_ref, pswa_ref, pgla_ref, wo_ref, n2_ref, x1_ref, h2_ref,
     klast_ref, vlast_ref, kprev_ref, vprev_ref, osw_ref) = refs
    w = WINDOW
    t_local = lax.rem(pl.program_id(0), tiles_per_batch)

    @pl.when(t_local == 0)
    def _():
        kprev_ref[...] = jnp.zeros_like(kprev_ref)
        vprev_ref[...] = jnp.zeros_like(vprev_ref)

    yb = _merge_gla_branch(og_ref, rg_ref, gg_ref, gn_ref, pgla_ref)
    k_prev, v_prev = kprev_ref[...], vprev_ref[...]
    for j in range(q_ref.shape[0] // w):
        rs = slice(j * w, (j + 1) * w)
        tabs = (c_ref[rs, :], slo_ref[rs, :], shi_ref[rs, :])
        has_prev = (t_local > 0) if j == 0 else True
        k_prev, v_prev = _swa_block(has_prev, q_ref[rs, :], k_ref[rs, :], v_ref[rs, :], tabs, sink_ref,
                                    k_prev, v_prev, osw_ref, j * w)
    kprev_ref[...] = k_prev
    vprev_ref[...] = v_prev
    _merge_finish(yb, osw_ref, gs_ref, x_ref, pswa_ref, wo_ref, n2_ref, x1_ref, h2_ref)

    @pl.when(t_local == tiles_per_batch - 1)
    def _():
        klast_ref[...] = kprev_ref[...]
        vlast_ref[...] = vprev_ref[...]


def _merge_specs(tm):
    resident = functools.partial(pl.BlockSpec, pipeline_mode=pl.Buffered(1))
    in_specs = [
        pl.BlockSpec((tm, GLA_DV_TOTAL), lambda i: (i, 0)),
        pl.BlockSpec((tm, GLA_DV_TOTAL), lambda i: (i, OFF_RG // GLA_DV_TOTAL)),
        pl.BlockSpec((tm, D_MODEL), lambda i: (i, OFF_GS // D_MODEL)),
        pl.BlockSpec((tm, D_MODEL), lambda i: (i, OFF_GG // D_MODEL)),
        pl.BlockSpec((tm, D_MODEL), lambda i: (i, 0)),
        pl.BlockSpec((1, GLA_DV), lambda i: (0, 0)),
        resident((SWA_Q, D_MODEL), lambda i: (0, 0)),
        resident((GLA_DV_TOTAL, D_MODEL), lambda i: (0, 0)),
        resident((D_MODEL, D_MODEL), lambda i: (0, 0)),
        pl.BlockSpec((1, D_MODEL), lambda i: (0, 0)),
    ]
    out_specs = [pl.BlockSpec((tm, D_MODEL), lambda i: (i, 0)), pl.BlockSpec((tm, D_MODEL), lambda i: (i, 0))]
    return in_specs, out_specs


def _merge_swa(z, o_gla, x2d, gnorm, p_swa, p_gla, w_o, norm2, sink, tables, batch, tm):
    n = x2d.shape[0]
    w = WINDOW
    tiles_per_batch = n // batch // tm
    in_specs, out_specs = _merge_specs(tm)
    tab_spec = pl.BlockSpec((tm, LANES), lambda i: (lax.rem(i, tiles_per_batch), 0))
    last_spec = pl.BlockSpec((None, w, SWA_KV), lambda i: (i // tiles_per_batch, 0, 0))
    return pl.pallas_call(
        functools.partial(_merge_swa_kernel, tiles_per_batch=tiles_per_batch),
        grid=(n // tm,),
        in_specs=[
            pl.BlockSpec(memory_space=pltpu.SMEM),
            pl.BlockSpec((tm, SWA_Q), lambda i: (i, OFF_QS // SWA_Q)),
            pl.BlockSpec((tm, SWA_KV), lambda i: (i, OFF_KS // SWA_KV)),
            pl.BlockSpec((tm, SWA_KV), lambda i: (i, OFF_VS // SWA_KV)),
            tab_spec, tab_spec, tab_spec,
        ] + in_specs,
        out_specs=out_specs + [last_spec, last_spec],
        out_shape=[
            jax.ShapeDtypeStruct((n, D_MODEL), F32),
            jax.ShapeDtypeStruct((n, D_MODEL), BF16),
            jax.ShapeDtypeStruct((batch, w, SWA_KV), F32),
            jax.ShapeDtypeStruct((batch, w, SWA_KV), F32),
        ],
        scratch_shapes=[pltpu.VMEM((w, SWA_KV), F32), pltpu.VMEM((w, SWA_KV), F32),
                        pltpu.VMEM((tm, SWA_Q), BF16)],
        compiler_params=_cparams(("arbitrary",)),
        name="merge_swa",
    )(sink, z, z, z, *tables, o_gla, z, z, z, x2d, gnorm, p_swa, p_gla, w_o, norm2)


def _merge(o_swa, o_gla, z, x2d, gnorm, p_swa, p_gla, w_o, norm2, tm):
    n = x2d.shape[0]
    in_specs, out_specs = _merge_specs(tm)
    return pl.pallas_call(
        _merge_kernel,
        grid=(n // tm,),
        in_specs=[pl.BlockSpec((tm, SWA_Q), lambda i: (i, 0))] + in_specs,
        out_specs=out_specs,
        out_shape=[
            jax.ShapeDtypeStruct((n, D_MODEL), F32),
            jax.ShapeDtypeStruct((n, D_MODEL), BF16),
        ],
        compiler_params=_cparams(("arbitrary",)),
        name="merge",
    )(o_swa, o_gla, z, z, z, x2d, gnorm, p_swa, p_gla, w_o, norm2)


def _mlp_step(h2_ref, x1_ref, load_wup, load_wdn, fn_ref, out_ref, acc_ref):
    f = pl.program_id(1)
    nf = pl.num_programs(1)

    @pl.when(f == 0)
    def _():
        acc_ref[...] = jnp.zeros_like(acc_ref)

    u = _dot(h2_ref[...], load_wup())
    u = jnp.square(jnp.maximum(u, 0.0)).astype(BF16)
    acc_ref[...] += _dot(u, load_wdn())

    @pl.when(f == nf - 1)
    def _():
        out_ref[...] = _rms(x1_ref[...] + acc_ref[...], fn_ref[...])


def _mlp_kernel(h2_ref, x1_ref, wup_ref, wdn_ref, fn_ref, out_ref, acc_ref):
    _mlp_step(h2_ref, x1_ref, lambda: wup_ref[...], lambda: wdn_ref[...], fn_ref, out_ref, acc_ref)


def _mlp_cast_kernel(h2_ref, x1_ref, wup_ref, wdn_ref, fn_ref, out_ref, wup_bf_ref, wdn_bf_ref, acc_ref):
    def cast_and_keep(src_ref, dst_ref):
        def load():
            w = src_ref[...].astype(BF16)
            dst_ref[...] = w
            return w
        return load

    _mlp_step(h2_ref, x1_ref, cast_and_keep(wup_ref, wup_bf_ref), cast_and_keep(wdn_ref, wdn_bf_ref),
              fn_ref, out_ref, acc_ref)


def _mlp_cast(h2, x1, w_up, w_down, final_norm, tf):
    n = h2.shape[0]
    return pl.pallas_call(
        _mlp_cast_kernel,
        grid=(1, D_FF // tf),
        in_specs=[
            pl.BlockSpec((n, D_MODEL), lambda i, f: (0, 0)),
            pl.BlockSpec((n, D_MODEL), lambda i, f: (0, 0)),
            pl.BlockSpec((None, D_MODEL, tf), lambda i, f: (0, 0, f)),
            pl.BlockSpec((None, tf, D_MODEL), lambda i, f: (0, f, 0)),
            pl.BlockSpec((1, D_MODEL), lambda i, f: (0, 0)),
        ],
        out_specs=[
            pl.BlockSpec((n, D_MODEL), lambda i, f: (0, 0)),
            pl.BlockSpec((D_MODEL, tf), lambda i, f: (0, f)),
            pl.BlockSpec((tf, D_MODEL), lambda i, f: (f, 0)),
        ],
        out_shape=[
            jax.ShapeDtypeStruct((n, D_MODEL), F32),
            jax.ShapeDtypeStruct((D_MODEL, D_FF), BF16),
            jax.ShapeDtypeStruct((D_FF, D_MODEL), BF16),
        ],
        scratch_shapes=[pltpu.VMEM((n, D_MODEL), F32)],
        compiler_params=_cparams(("arbitrary", "arbitrary")),
        name="mlp_cast",
    )(h2, x1, w_up, w_down, final_norm)


def _mlp(h2, x1, w_up, w_down, final_norm, tm, tf):
    n = h2.shape[0]
    return pl.pallas_call(
        _mlp_kernel,
        grid=(n // tm, D_FF // tf),
        in_specs=[
            pl.BlockSpec((tm, D_MODEL), lambda i, f: (i, 0)),
            pl.BlockSpec((tm, D_MODEL), lambda i, f: (i, 0)),
            pl.BlockSpec((D_MODEL, tf), lambda i, f: (0, f)),
            pl.BlockSpec((tf, D_MODEL), lambda i, f: (f, 0)),
            pl.BlockSpec((1, D_MODEL), lambda i, f: (0, 0)),
        ],
        out_specs=pl.BlockSpec((tm, D_MODEL), lambda i, f: (i, 0)),
        out_shape=jax.ShapeDtypeStruct((n, D_MODEL), F32),
        scratch_shapes=[pltpu.VMEM((tm, D_MODEL), F32)],
        compiler_params=_cparams(("arbitrary", "arbitrary")),
        name="mlp",
    )(h2, x1, w_up, w_down, final_norm)


def _pick_tile(n, pref):
    t = min(n, pref)
    while n % t:
        t //= 2
    return t


def _tile_plan(n_prompt, n_sample, t_prompt, b_sample):
    return {
        "proj_rows_p": _pick_tile(n_prompt, 512),
        "swa_pairs": _pick_tile(b_sample // 2, 4), "gla_pairs": _pick_tile(b_sample // 2, 2),
        "gla_rows": _pick_tile(t_prompt, GLA_TILE),
        "merge_rows_p": _pick_tile(n_prompt, 256), "merge_rows_s": _pick_tile(n_sample, 256),
        "mlp_rows_p": _pick_tile(n_prompt, 512),
        "mlp_ff": 1024, "mlp_cast_ff": 512,
    }


W_IN_SPLITS = (("qs", SWA_Q), ("ks", SWA_KV), ("vs", SWA_KV), ("qg", GLA_DK_TOTAL), ("kg", GLA_DK_TOTAL),
               ("vg", GLA_DV_TOTAL), ("rg", GLA_DV_TOTAL), ("ag", GLA_GATE_RANK), ("gs", D_MODEL), ("gg", D_MODEL))


REORDER_PIECE = HEAD_DIM
REORDER_PIECES = 8


def _w_in_row_table():
    src, o = {}, 0
    for name, width in W_IN_SPLITS:
        src[name] = o
        o += width
    rows = np.zeros(Z_COLS, np.int64)
    for name, dst in (("gs", OFF_GS), ("gg", OFF_GG), ("vg", OFF_VG), ("rg", OFF_RG), ("qg", OFF_QG),
                      ("kg", OFF_KG), ("ks", OFF_KS), ("vs", OFF_VS)):
        width = dict(W_IN_SPLITS)[name]
        rows[dst:dst + width] = src[name] + np.arange(width)
    for h in range(GROUP):
        for g in range(N_KV_HEADS):
            dst = OFF_QS + h * SWA_KV + g * HEAD_DIM
            rows[dst:dst + HEAD_DIM] = src["qs"] + (g * GROUP + h) * HEAD_DIM + np.arange(HEAD_DIM)
    pieces = rows.reshape(-1, REORDER_PIECE)
    assert (pieces == pieces[:, :1] + np.arange(REORDER_PIECE)).all()
    return jnp.asarray(pieces[:, 0], jnp.int32), src["ag"]


def _reorder_w_in_kernel(tbl_ref, *refs):
    piece_refs = refs[:REORDER_PIECES]
    ag_ref, x_ref, nw_ref, wm_ref, wag_ref, z_ref, zag_ref, h_ref = refs[REORDER_PIECES:]

    @pl.when(pl.program_id(0) == 0)
    def _():
        pad = jnp.zeros((LANES - GLA_GATE_RANK, wag_ref.shape[1]), BF16)
        wag = jnp.concatenate([ag_ref[0].astype(BF16), pad], axis=0)
        wag_ref[...] = wag
        h = _rms(x_ref[...], nw_ref[...]).astype(BF16)
        h_ref[...] = h
        zag_ref[...] = _dot_nt(h, wag)

    tile = jnp.concatenate([p[0].astype(BF16) for p in piece_refs], axis=0)
    wm_ref[...] = tile
    z_ref[...] = _dot_nt(h_ref[...], tile)


def _reorder_w_in(w_in_t, x_rows, norm1):
    d = w_in_t.shape[2]
    n = x_rows.shape[0]
    table, ag_row = _w_in_row_table()
    rows = REORDER_PIECE * REORDER_PIECES
    piece = lambda k: pl.BlockSpec((pl.Element(1), pl.Element(REORDER_PIECE), pl.Element(d)),
                                   lambda i, tbl: (0, pl.multiple_of(tbl[i * REORDER_PIECES + k], GLA_GATE_RANK), 0))
    grid_spec = pltpu.PrefetchScalarGridSpec(
        num_scalar_prefetch=1,
        grid=(Z_COLS // rows,),
        in_specs=[piece(k) for k in range(REORDER_PIECES)]
        + [pl.BlockSpec((pl.Element(1), pl.Element(GLA_GATE_RANK), pl.Element(d)),
                        lambda i, tbl: (0, ag_row, 0)),
           pl.BlockSpec((n, d), lambda i, tbl: (0, 0)),
           pl.BlockSpec((1, d), lambda i, tbl: (0, 0))],
        out_specs=[pl.BlockSpec((rows, d), lambda i, tbl: (i, 0)),
                   pl.BlockSpec((LANES, d), lambda i, tbl: (0, 0)),
                   pl.BlockSpec((n, rows), lambda i, tbl: (0, i)),
                   pl.BlockSpec((n, LANES), lambda i, tbl: (0, 0))],
        scratch_shapes=[pltpu.VMEM((n, d), BF16)],
    )
    return pl.pallas_call(
        _reorder_w_in_kernel,
        grid_spec=grid_spec,
        out_shape=[jax.ShapeDtypeStruct((Z_COLS, d), BF16), jax.ShapeDtypeStruct((LANES, d), BF16),
                   jax.ShapeDtypeStruct((n, Z_COLS), F32), jax.ShapeDtypeStruct((n, LANES), F32)],
        compiler_params=_cparams(("arbitrary",)),
        name="reorder_w_in",
    )(table, *([w_in_t] * (REORDER_PIECES + 1)), x_rows, norm1)


def kernel(x_prompt, x_sample, cache_swa_k, cache_swa_v, state_gla, norm1, w_in, w_a2, b_a, sink,
           gla_norm, p_swa, p_gla, w_o, norm2, w_up, w_down, final_norm):
    assert norm1.shape[0] == 1, "single-layer stack"
    bp, tp, d = x_prompt.shape
    bs, ts, _ = x_sample.shape
    assert ts == SAMPLE_T and bs % 2 == 0 and tp % WINDOW == 0
    w_buf = cache_swa_k.shape[2]
    assert w_buf == WINDOW

    wa2 = jnp.pad(w_a2[0], ((0, LANES - GLA_GATE_RANK), (0, 0))).astype(BF16)
    ba = b_a[0][None, :]
    n1, n2, fn = norm1[0][None, :], norm2[0][None, :], final_norm[None, :]
    gn = gla_norm[0][None, :]
    pswa = p_swa[0].reshape(N_KV_HEADS, GROUP, HEAD_DIM, d).transpose(1, 0, 2, 3).reshape(SWA_Q, d).astype(BF16)
    pgla, wo = p_gla[0].astype(BF16), w_o[0].astype(BF16)
    sink_smem = sink[0][None, :]
    sink_rows = jnp.broadcast_to(jnp.repeat(sink[0], PAIR_ROWS)[:, None], (N_HEADS * PAIR_ROWS, LANES))

    xp = x_prompt.reshape(bp * tp, d)
    xs = x_sample.reshape(bs * ts, d)
    np_, ns = xp.shape[0], xs.shape[0]

    tiles = _tile_plan(np_, ns, tp, bs)

    w_main, w_ag, zs, zags = _reorder_w_in(jnp.swapaxes(w_in, 1, 2), xs, n1)
    zp, zagp = _in_proj(xp, n1, w_main, w_ag, tiles["proj_rows_p"], Z_COLS // 2, BF16)

    zp3 = zp.reshape(bp, tp, Z_COLS)
    pos_s = PAST_LEN + jnp.arange(ts)
    tabs_s = _rope_tables(jnp.concatenate([pos_s, pos_s]))
    pos_minor = lambda c: jnp.transpose(c[0], (0, 2, 3, 1)).reshape(bs, SWA_KV, w_buf)
    pos_major = lambda c: jnp.transpose(c.reshape(bs, N_KV_HEADS, HEAD_DIM, w_buf), (0, 3, 1, 2))[None]
    o_swa_s, nk_s, nv_s = _swa_sample(zs, pos_minor(cache_swa_k), pos_minor(cache_swa_v), sink_rows, tabs_s,
                                      tiles["swa_pairs"])

    o_gla_p, s_p = _gla_prompt(zp3, zagp.reshape(bp, tp, LANES), wa2, ba, tiles["gla_rows"])
    o_gla_s, s_s = _gla_sample(zs, zags, wa2, ba, state_gla[0], tiles["gla_pairs"])

    x1p, h2p, k_last, v_last = _merge_swa(zp, o_gla_p.reshape(np_, GLA_DV_TOTAL), xp, gn, pswa, pgla, wo, n2,
                                          sink_smem, _rope_tables(jnp.arange(tp)), bp, tiles["merge_rows_p"])
    x1s, h2s = _merge(o_swa_s, o_gla_s, zs, xs, gn, pswa, pgla, wo, n2, tiles["merge_rows_s"])
    ys, wup, wdn = _mlp_cast(h2s, x1s, w_up, w_down, fn, tiles["mlp_cast_ff"])
    yp = _mlp(h2p, x1p, wup, wdn, fn, tiles["mlp_rows_p"], tiles["mlp_ff"])

    kv5 = lambda a, nb: a.reshape(1, nb, w_buf, N_KV_HEADS, HEAD_DIM)
    return (yp.reshape(bp, tp, d), ys.reshape(bs, ts, d),
            kv5(k_last, bp), kv5(v_last, bp), s_p[None],
            pos_major(nk_s), pos_major(nv_s), s_s[None])
```

```python
import functools

import jax
import jax.numpy as jnp
import numpy as np
from jax import lax
from jax.experimental import pallas as pl
from jax.experimental.pallas import tpu as pltpu

F32 = jnp.float32
BF16 = jnp.bfloat16

D_MODEL = 2048
PAST_LEN = 8192
N_HEADS = 16
N_KV_HEADS = 4
GROUP = N_HEADS // N_KV_HEADS
HEAD_DIM = 64
WINDOW = 128
ROT_DIM = HEAD_DIM // 4
ROPE_THETA = 500000.0
SWA_Q = N_HEADS * HEAD_DIM
SWA_KV = N_KV_HEADS * HEAD_DIM
GLA_HEADS = 4
GLA_DK = 256
GLA_DV = 512
GLA_DK_TOTAL = GLA_HEADS * GLA_DK
GLA_DV_TOTAL = GLA_HEADS * GLA_DV
GLA_GATE_RANK = 16
GLA_GATE_NORM = 16.0
D_FF = 4 * D_MODEL
EPS = 1e-6
LOG2E = 1.4426950408889634

LANES = 128
VMEM_LIMIT = 56 * 1024 * 1024

OFF_GS = 0
OFF_GG = OFF_GS + D_MODEL
OFF_VG = OFF_GG + D_MODEL
OFF_RG = OFF_VG + GLA_DV_TOTAL
OFF_QG = OFF_RG + GLA_DV_TOTAL
OFF_KG = OFF_QG + GLA_DK_TOTAL
OFF_QS = OFF_KG + GLA_DK_TOTAL
OFF_KS = OFF_QS + SWA_Q
OFF_VS = OFF_KS + SWA_KV
Z_COLS = OFF_VS + SWA_KV

GLA_BLOCK = 16
GLA_TILE = 256


def _cparams(sem):
    return pltpu.CompilerParams(dimension_semantics=sem, vmem_limit_bytes=VMEM_LIMIT)


def _rms(x, w):
    return x * lax.rsqrt(jnp.mean(x * x, axis=-1, keepdims=True) + EPS) * w


def _sigmoid(x):
    return 1.0 / (1.0 + jnp.exp(-x))


def _dot(a, b):
    return jnp.dot(a, b, preferred_element_type=F32)


def _dot_nt(a, b):
    return lax.dot_general(a, b, (((1,), (1,)), ((), ())), preferred_element_type=F32)


def _dot_tn(a, b):
    return lax.dot_general(a, b, (((0,), (0,)), ((), ())), preferred_element_type=F32)


def _split_bf16(x):
    hi = x.astype(BF16)
    lo = (x - hi.astype(F32)).astype(BF16)
    return hi, lo


def _in_proj_kernel(x_ref, nw_ref, w_ref, wag_ref, z_ref, zag_ref):
    h = _rms(x_ref[...], nw_ref[...]).astype(BF16)
    z_ref[...] = _dot_nt(h, w_ref[...]).astype(z_ref.dtype)

    @pl.when(pl.program_id(0) == 0)
    def _():
        zag_ref[...] = _dot_nt(h, wag_ref[...])


def _in_proj(x2d, norm1, w_main, w_ag, tm, tn, z_dtype):
    n = x2d.shape[0]
    nrow = n // tm
    w_mode = {"pipeline_mode": pl.Buffered(1)}
    return pl.pallas_call(
        _in_proj_kernel,
        grid=(Z_COLS // tn, nrow),
        in_specs=[
            pl.BlockSpec((tm, D_MODEL), lambda j, i: (i, 0)),
            pl.BlockSpec((1, D_MODEL), lambda j, i: (0, 0)),
            pl.BlockSpec((tn, D_MODEL), lambda j, i: (j, 0), **w_mode),
            pl.BlockSpec((LANES, D_MODEL), lambda j, i: (0, 0)),
        ],
        out_specs=[
            pl.BlockSpec((tm, tn), lambda j, i: (i, j)),
            pl.BlockSpec((tm, LANES), lambda j, i: (jnp.where(j == 0, i, nrow - 1), 0)),
        ],
        out_shape=[
            jax.ShapeDtypeStruct((n, Z_COLS), z_dtype),
            jax.ShapeDtypeStruct((n, LANES), F32),
        ],
        compiler_params=_cparams(("arbitrary", "arbitrary")),
        name="in_proj",
    )(x2d, norm1, w_main, w_ag)


def _rope_tables(pos):
    half = ROT_DIM // 2
    inv = ROPE_THETA ** (-jnp.arange(half, dtype=F32) * 2.0 / ROT_DIM)
    ang = pos.astype(F32)[:, None] * inv[None, :]
    cos, sin = jnp.cos(ang), jnp.sin(ang)
    t = pos.shape[0]
    ones = jnp.ones((t, HEAD_DIM - ROT_DIM), F32)
    zeros = jnp.zeros((t, HEAD_DIM - ROT_DIM), F32)
    zh = jnp.zeros((t, half), F32)
    c = jnp.concatenate([cos, cos, ones], axis=1)
    s_lo = jnp.concatenate([zh, sin, zeros], axis=1)
    s_hi = jnp.concatenate([-sin, zh, zeros], axis=1)
    rep = LANES // HEAD_DIM
    return tuple(jnp.tile(a, (1, rep)) for a in (c, s_lo, s_hi))


def _rope(x, c, s_lo, s_hi):
    half = ROT_DIM // 2
    outs = []
    for j in range(x.shape[1] // LANES):
        xc = x[:, j * LANES:(j + 1) * LANES]
        outs.append(xc * c + pltpu.roll(xc, half, 1) * s_lo + pltpu.roll(xc, LANES - half, 1) * s_hi)
    return outs[0] if len(outs) == 1 else jnp.concatenate(outs, axis=1)


def _swa_block(has_prev, q_blk, k_blk, v_blk, tabs, sink_ref, k_prev, v_prev, o_ref, row0):
    w = WINDOW
    kvw = SWA_KV
    c, s_lo, s_hi = tabs
    q = (_rope(q_blk.astype(F32), c, s_lo, s_hi) * (HEAD_DIM ** -0.5 * LOG2E)).astype(BF16)
    k = _rope(k_blk.astype(F32), c, s_lo, s_hi)
    v = v_blk.astype(F32)
    kc = jnp.concatenate([k_prev, k], axis=0).astype(BF16)
    vc = jnp.concatenate([v_prev, v], axis=0).astype(BF16)

    rows = lax.broadcasted_iota(jnp.int32, (w, 2 * w), 0)
    cols = lax.broadcasted_iota(jnp.int32, (w, 2 * w), 1)
    diff = rows + w - cols
    bias = jnp.where((diff >= 0) & (diff < w) & ((cols >= w) | has_prev), 0.0, -jnp.inf)
    bias = jnp.concatenate([bias] * GROUP, axis=0)
    head_shift = HEAD_DIM.bit_length() - 1
    lane_head_q = lax.broadcasted_iota(jnp.int32, (w, kvw), 1) >> head_shift
    lane_head = lax.broadcasted_iota(jnp.int32, (2 * w, kvw), 1) >> head_shift

    for g in range(N_KV_HEADS):
        in_g_q = jnp.where(lane_head_q == g, 1.0, 0.0).astype(BF16)
        in_g = jnp.where(lane_head == g, 1.0, 0.0).astype(BF16)
        qg = jnp.concatenate([q[:, h * kvw:(h + 1) * kvw] * in_g_q for h in range(GROUP)], axis=0)
        sk = jnp.concatenate([jnp.full((w, LANES), sink_ref[0, g * GROUP + h] * LOG2E, F32)
                              for h in range(GROUP)], axis=0)
        s = _dot_nt(qg, kc) + bias
        m = jnp.maximum(jnp.broadcast_to(jnp.max(s, axis=-1, keepdims=True), sk.shape), sk)
        p = jnp.exp2(s - jnp.concatenate([m, m], axis=1)).astype(BF16)
        vg = vc * in_g + (1.0 - in_g)
        oa = _dot(p, vg)
        p_sink = jnp.exp2(sk - m)
        denom = pltpu.roll(oa, 2 * HEAD_DIM, 1) + jnp.concatenate([p_sink, p_sink], axis=1)
        o = (oa / denom).astype(o_ref.dtype)
        for h in range(GROUP):
            lo = g * HEAD_DIM
            o_ref[row0:row0 + w, h * kvw + lo:h * kvw + lo + HEAD_DIM] = o[h * w:(h + 1) * w, lo:lo + HEAD_DIM]
    return k, v


SAMPLE_T = 4
PAIR_ROWS = 2 * SAMPLE_T


def _swa_sample_kernel(q_ref, k_ref, v_ref, ck_ref, cv_ref, c_ref, slo_ref, shi_ref, sink_ref,
                       o_ref, nk_ref, nv_ref, *, pairs):
    w = WINDOW
    kvw = SWA_KV
    c, s_lo, s_hi = c_ref[...], slo_ref[...], shi_ref[...]
    nrow = N_HEADS * PAIR_ROWS
    row = lax.broadcasted_iota(jnp.int32, (nrow, w), 0)
    col = lax.broadcasted_iota(jnp.int32, (nrow, w), 1)
    t_shift = SAMPLE_T.bit_length() - 1
    sq = row & (SAMPLE_T - 1)
    par = (row >> t_shift) & 1
    mask_cache = col > sq
    mask_new = (col < PAIR_ROWS) & ((col >> t_shift) == par) & ((col & (SAMPLE_T - 1)) <= sq)
    lane = lax.broadcasted_iota(jnp.int32, (PAIR_ROWS, kvw), 1) >> (HEAD_DIM.bit_length() - 1)
    pos = lax.broadcasted_iota(jnp.int32, (kvw, w), 1)
    sk = sink_ref[...][:, :1]
    zpad = jnp.zeros((w - PAIR_ROWS, kvw), F32)

    for p in range(pairs):
        rs = slice(p * PAIR_ROWS, (p + 1) * PAIR_ROWS)
        q8 = _rope(q_ref[rs, :], c, s_lo, s_hi) * (HEAD_DIM ** -0.5)
        k8 = _rope(k_ref[rs, :], c, s_lo, s_hi)
        v8 = v_ref[rs, :]
        blocks = [jnp.where(lane == g, q8[:, h * kvw:(h + 1) * kvw], 0.0)
                  for g in range(N_KV_HEADS) for h in range(GROUP)]
        qall = jnp.concatenate(blocks, axis=0).astype(BF16)

        k_new = jnp.concatenate([k8, zpad], axis=0)
        v_new = jnp.concatenate([v8, zpad], axis=0)
        s_cache = [_dot(qall, ck_ref[2 * p + e].astype(BF16)) for e in range(2)]
        s_cache = jnp.where(mask_cache, jnp.where(par == 1, s_cache[1], s_cache[0]), -jnp.inf)
        s_new = jnp.where(mask_new, _dot_nt(qall, k_new.astype(BF16)), -jnp.inf)
        s = jnp.concatenate([s_cache, s_new], axis=1)
        m = jnp.maximum(jnp.max(s, axis=-1, keepdims=True), sk)
        pr = jnp.exp(s - m)
        denom = jnp.sum(pr, axis=-1, keepdims=True) + jnp.exp(sk - m)
        p_cache, p_new = pr[:, :w], pr[:, w:]
        p0 = jnp.where(par == 0, p_cache, 0.0).astype(BF16)
        p1 = jnp.where(par == 1, p_cache, 0.0).astype(BF16)
        oall = (_dot_nt(p0, cv_ref[2 * p].astype(BF16)) + _dot_nt(p1, cv_ref[2 * p + 1].astype(BF16))
                + _dot(p_new.astype(BF16), v_new.astype(BF16))) / denom

        outs = []
        for h in range(GROUP):
            acc = None
            for g in range(N_KV_HEADS):
                hh = g * GROUP + h
                blk = jnp.where(lane == g, oall[hh * PAIR_ROWS:(hh + 1) * PAIR_ROWS], 0.0)
                acc = blk if acc is None else acc + blk
            outs.append(acc)
        o_ref[rs, :] = jnp.concatenate(outs, axis=1).astype(o_ref.dtype)

        for new, cref, nref in ((k_new, ck_ref, nk_ref), (v_new, cv_ref, nv_ref)):
            new_t = new.T
            for e in range(2):
                bd = 2 * p + e
                shifted = pltpu.roll(cref[bd], w - SAMPLE_T, 1)
                tail = pltpu.roll(new_t, w - SAMPLE_T - e * SAMPLE_T, 1)
                nref[bd] = jnp.where(pos >= w - SAMPLE_T, tail, shifted)


def _swa_sample(zs, cache_k, cache_v, sink_rows, tables, pairs):
    n = zs.shape[0]
    bd, kvw, w = cache_k.shape
    rows = pairs * PAIR_ROWS
    tab_spec = pl.BlockSpec((PAIR_ROWS, LANES), lambda i: (0, 0))
    cache_spec = pl.BlockSpec((2 * pairs, kvw, w), lambda i: (i, 0, 0))
    return pl.pallas_call(
        functools.partial(_swa_sample_kernel, pairs=pairs),
        grid=(n // rows,),
        in_specs=[
            pl.BlockSpec((rows, SWA_Q), lambda i: (i, OFF_QS // SWA_Q)),
            pl.BlockSpec((rows, SWA_KV), lambda i: (i, OFF_KS // SWA_KV)),
            pl.BlockSpec((rows, SWA_KV), lambda i: (i, OFF_VS // SWA_KV)),
            cache_spec, cache_spec,
            tab_spec, tab_spec, tab_spec,
            pl.BlockSpec((N_HEADS * PAIR_ROWS, LANES), lambda i: (0, 0)),
        ],
        out_specs=[
            pl.BlockSpec((rows, SWA_Q), lambda i: (i, 0)),
            cache_spec, cache_spec,
        ],
        out_shape=[
            jax.ShapeDtypeStruct((n, SWA_Q), BF16),
            jax.ShapeDtypeStruct((bd, kvw, w), F32),
            jax.ShapeDtypeStruct((bd, kvw, w), F32),
        ],
        compiler_params=_cparams(("arbitrary",)),
        name="swa_sample",
    )(zs, zs, zs, cache_k, cache_v, *tables, sink_rows)


def _log_decay(ag, wa2, ba):
    x = _dot(ag.astype(BF16), wa2) + ba
    log_sig = jnp.minimum(x, 0.0) - jnp.log(1.0 + jnp.exp(-jnp.abs(x)))
    return log_sig * (1.0 / GLA_GATE_NORM)


def _gla_out(o, gnorm, rg):
    dv = o.shape[-1]
    ms = jnp.broadcast_to(jnp.sum(o * o, axis=-1, keepdims=True), (o.shape[0], LANES)) * (1.0 / dv)
    r = lax.rsqrt(ms + EPS)
    half = rg * 0.5
    gate = half + half * jnp.tanh(half)
    return o * jnp.concatenate([r] * (dv // LANES), axis=1) * gnorm * gate


def _gla_constants(tt):
    nb = tt // GLA_BLOCK
    halves = [tt >> (l + 1) for l in range(nb.bit_length() - 1)]
    step = np.arange(tt)
    blk = step // GLA_BLOCK
    t16 = ((blk[:, None] == blk[None, :]) & (step[None, :] <= step[:, None])).astype(np.float32)
    rows = [(blk[None, :] < np.arange(nb)[:, None])]
    for hs in halves:
        t_b = (np.arange(nb) * GLA_BLOCK) // (2 * hs) * (2 * hs) + hs
        rows.append(step[None, :] < t_b[:, None])
    rows.append(np.ones((GLA_BLOCK, tt), bool))
    sel = np.concatenate(rows, axis=0).astype(np.float32)
    group = [(step[:, None] // (2 * hs)) == (step[None, :] // (2 * hs)) for hs in halves[1:]]
    mlev = np.stack(group).astype(np.float32)
    return (jnp.asarray(t16, BF16), jnp.asarray(sel, BF16), jnp.asarray(t16), jnp.asarray(mlev)), halves


def _gla_prompt_kernel(q_ref, k_ref, v_ref, ag_ref, wa2_ref, ba_ref,
                       t16_ref, sel_ref, mdiag_ref, mlev_ref, o_ref, sout_ref, st_ref, *, halves):
    t = pl.program_id(1)
    nt = pl.num_programs(1)
    tt = q_ref.shape[0]
    nb = tt // GLA_BLOCK
    nl = len(halves)
    dk, dv = GLA_DK, GLA_DV

    @pl.when(t == 0)
    def _():
        st_ref[...] = jnp.zeros_like(st_ref)

    la = _log_decay(ag_ref[...], wa2_ref[...], ba_ref[...]) * LOG2E
    hi, lo = _split_bf16(la)
    stacked = jnp.concatenate([t16_ref[...], sel_ref[...]], axis=0)
    sums = _dot(stacked, hi) + _dot(stacked, lo)
    c = sums[:tt]
    cum = sums[tt:]
    p_start = cum[:nb]
    total = cum[(1 + nl) * nb:(1 + nl) * nb + 1]
    exp_p = jnp.exp2(p_start)
    d_last = total - p_start
    after, before = [], []
    for l in range(nl):
        d = p_start - cum[(1 + l) * nb:(2 + l) * nb]
        after.append(jnp.exp2(jnp.minimum(d, 0.0)))
        before.append(-d)

    zeros = jnp.zeros((GLA_BLOCK, dk), BF16)
    names = ["qs", "kinv", "qt", "kd"] + [f"q{l}" for l in range(nl)] + [f"k{l}" for l in range(nl)]

    def scaled_operands(h):
        ks = slice(h * dk, (h + 1) * dk)
        parts = {nm: [] for nm in names}
        for i in range(nb):
            rs = slice(i * GLA_BLOCK, (i + 1) * GLA_BLOCK)
            row = slice(i, i + 1)
            c_b = c[rs, ks]
            k_b = k_ref[rs, ks].astype(F32)
            qs = q_ref[rs, ks].astype(F32) * (dk ** -0.5) * jnp.exp2(c_b)
            parts["qs"].append(qs.astype(BF16))
            parts["kinv"].append((k_b * jnp.exp2(-c_b)).astype(BF16))
            parts["qt"].append((qs * exp_p[row, ks]).astype(BF16))
            parts["kd"].append((k_b * jnp.exp2(d_last[row, ks] - c_b)).astype(BF16))
            for l, hs in enumerate(halves):
                if (i * GLA_BLOCK) % (2 * hs) >= hs:
                    parts[f"q{l}"].append((qs * after[l][row, ks]).astype(BF16))
                    parts[f"k{l}"].append(zeros)
                else:
                    parts[f"q{l}"].append(zeros)
                    parts[f"k{l}"].append((k_b * jnp.exp2(before[l][row, ks] - c_b)).astype(BF16))
        return {nm: jnp.concatenate(parts[nm], axis=0) for nm in names}

    in_block = mdiag_ref[...] != 0.0
    ops = scaled_operands(0)
    for h in range(GLA_HEADS):
        ks = slice(h * dk, (h + 1) * dk)
        vs = slice(h * dv, (h + 1) * dv)
        scores = [_dot_nt(ops["qs"], ops["kinv"])] + [_dot_nt(ops[f"q{l}"], ops[f"k{l}"]) for l in range(nl)]
        v_h = v_ref[:, vs]
        st = st_ref[h]
        o_state = _dot_nt(ops["qt"], st.astype(BF16))
        st_new = jnp.exp2(total[:, ks]) * st + _dot_tn(v_h, ops["kd"])
        if h + 1 < GLA_HEADS:
            ops = scaled_operands(h + 1)
        a = jnp.where(in_block, scores[0], 0.0) + scores[1]
        for l in range(1, nl):
            a = a + jnp.where(mlev_ref[l - 1] != 0.0, scores[l + 1], 0.0)
        st_ref[h] = st_new
        o_ref[:, vs] = (_dot(a.astype(BF16), v_h) + o_state).astype(o_ref.dtype)

    @pl.when(t == nt - 1)
    def _():
        for h in range(GLA_HEADS):
            sout_ref[h] = st_ref[h].T


def _gla_prompt(z3, zag3, wa2, ba, tt):
    b, t, _ = z3.shape
    h = GLA_HEADS
    consts, halves = _gla_constants(tt)
    const_specs = [pl.BlockSpec(c.shape, lambda i, s, nd=c.ndim: (0,) * nd) for c in consts]
    return pl.pallas_call(
        functools.partial(_gla_prompt_kernel, halves=tuple(halves)),
        grid=(b, t // tt),
        in_specs=[
            pl.BlockSpec((None, tt, GLA_DK_TOTAL), lambda i, s: (i, s, OFF_QG // GLA_DK_TOTAL)),
            pl.BlockSpec((None, tt, GLA_DK_TOTAL), lambda i, s: (i, s, OFF_KG // GLA_DK_TOTAL)),
            pl.BlockSpec((None, tt, GLA_DV_TOTAL), lambda i, s: (i, s, OFF_VG // GLA_DV_TOTAL)),
            pl.BlockSpec((None, tt, LANES), lambda i, s: (i, s, 0)),
            pl.BlockSpec((LANES, GLA_DK_TOTAL), lambda i, s: (0, 0)),
            pl.BlockSpec((1, GLA_DK_TOTAL), lambda i, s: (0, 0)),
            *const_specs,
        ],
        out_specs=[
            pl.BlockSpec((None, tt, GLA_DV_TOTAL), lambda i, s: (i, s, 0)),
            pl.BlockSpec((None, h, GLA_DK, GLA_DV), lambda i, s: (i, 0, 0, 0)),
        ],
        out_shape=[
            jax.ShapeDtypeStruct((b, t, GLA_DV_TOTAL), BF16),
            jax.ShapeDtypeStruct((b, h, GLA_DK, GLA_DV), F32),
        ],
        scratch_shapes=[pltpu.VMEM((h, GLA_DV, GLA_DK), F32)],
        compiler_params=_cparams(("arbitrary", "arbitrary")),
        name="gla_prompt",
    )(z3, z3, z3, zag3, wa2, ba, *consts)


def _gla_sample_kernel(q_ref, k_ref, v_ref, ag_ref, wa2_ref, ba_ref, s0_ref, o_ref, s1_ref, *, pairs):
    dk, dv = GLA_DK, GLA_DV
    r_k = lax.broadcasted_iota(jnp.int32, (PAIR_ROWS, dk), 0)
    r_v = lax.broadcasted_iota(jnp.int32, (PAIR_ROWS, dv), 0)
    step_k = r_k & (SAMPLE_T - 1)
    step_v = r_v & (SAMPLE_T - 1)
    odd_k = r_k >= SAMPLE_T
    odd_v = r_v >= SAMPLE_T
    la_all = _log_decay(ag_ref[...], wa2_ref[...], ba_ref[...])
    kpad = jnp.zeros((LANES - PAIR_ROWS, dk), F32)
    vpad = jnp.zeros((LANES - PAIR_ROWS, dv), BF16)
    ones = jnp.ones((LANES, LANES), BF16)

    for p, h in [(p, h) for p in range(pairs) for h in range(GLA_HEADS)]:
        rs = slice(p * PAIR_ROWS, (p + 1) * PAIR_ROWS)
        la = la_all[rs, h * dk:(h + 1) * dk]
        b = la + jnp.where(step_k >= 1, pltpu.roll(la, 1, 0), 0.0)
        b = b + jnp.where(step_k >= 2, pltpu.roll(b, 2, 0), 0.0)
        b_last = jnp.where(odd_k, b[PAIR_ROWS - 1:PAIR_ROWS, :], b[SAMPLE_T - 1:SAMPLE_T, :])
        q = q_ref[rs, h * dk:(h + 1) * dk] * (dk ** -0.5)
        k = k_ref[rs, h * dk:(h + 1) * dk]
        v = v_ref[rs, h * dv:(h + 1) * dv]
        q_i = q * jnp.exp(b)
        k_i = k * jnp.exp(-b)
        k_d = k * jnp.exp(b_last - b)
        o = jnp.sum(q_i * k_i, axis=-1, keepdims=True) * v
        for d in range(1, SAMPLE_T):
            a_d = jnp.sum(q_i * pltpu.roll(k_i, d, 0), axis=-1, keepdims=True)
            o = o + jnp.where(step_v >= d, a_d * pltpu.roll(v, d, 0), 0.0)
        q_b = q_i.astype(BF16)
        vb = jnp.concatenate([v.astype(BF16), vpad], axis=0)
        o_par = []
        for e in range(2):
            s_old = s0_ref[2 * p + e, h]
            o_par.append(_dot(q_b, s_old.astype(BF16)))
            sel = (r_k >= SAMPLE_T) if e == 1 else (r_k < SAMPLE_T)
            kd_e = jnp.concatenate([jnp.where(sel, k_d, 0.0), kpad], axis=0).astype(BF16)
            la_e = jnp.concatenate([jnp.where(sel, la, 0.0), kpad], axis=0)
            hi, lo = _split_bf16(la_e)
            decay = jnp.exp(_dot_tn(hi, ones) + _dot_tn(lo, ones))
            decay_full = jnp.concatenate([decay] * (dv // LANES), axis=1)
            s1_ref[2 * p + e, h] = decay_full * s_old + _dot_tn(kd_e, vb)
        o = o + jnp.where(odd_v, o_par[1], o_par[0])
        o_ref[rs, h * dv:(h + 1) * dv] = o


def _gla_sample(zs, zag, wa2, ba, state, pairs):
    n = zs.shape[0]
    bd, h, dk, dv = state.shape
    rows = pairs * PAIR_ROWS
    state_spec = pl.BlockSpec((2 * pairs, h, dk, dv), lambda i: (i, 0, 0, 0))
    return pl.pallas_call(
        functools.partial(_gla_sample_kernel, pairs=pairs),
        grid=(n // rows,),
        in_specs=[
            pl.BlockSpec((rows, GLA_DK_TOTAL), lambda i: (i, OFF_QG // GLA_DK_TOTAL)),
            pl.BlockSpec((rows, GLA_DK_TOTAL), lambda i: (i, OFF_KG // GLA_DK_TOTAL)),
            pl.BlockSpec((rows, GLA_DV_TOTAL), lambda i: (i, OFF_VG // GLA_DV_TOTAL)),
            pl.BlockSpec((rows, LANES), lambda i: (i, 0)),
            pl.BlockSpec((LANES, GLA_DK_TOTAL), lambda i: (0, 0)),
            pl.BlockSpec((1, GLA_DK_TOTAL), lambda i: (0, 0)),
            state_spec,
        ],
        out_specs=[
            pl.BlockSpec((rows, GLA_DV_TOTAL), lambda i: (i, 0)),
            state_spec,
        ],
        out_shape=[
            jax.ShapeDtypeStruct((n, GLA_DV_TOTAL), F32),
            jax.ShapeDtypeStruct(state.shape, F32),
        ],
        compiler_params=_cparams(("arbitrary",)),
        name="gla_sample",
    )(zs, zs, zs, zag, wa2, ba, state)


def _merge_gla_branch(og_ref, rg_ref, gg_ref, gn_ref, pgla_ref):
    gn = gn_ref[...]
    b = None
    for h in range(GLA_HEADS):
        vs = slice(h * GLA_DV, (h + 1) * GLA_DV)
        og = _gla_out(og_ref[:, vs].astype(F32), gn, rg_ref[:, vs].astype(F32)).astype(BF16)
        part = _dot(og, pgla_ref[vs, :])
        b = part if b is None else b + part
    return _sigmoid(gg_ref[...].astype(F32)) * b


def _merge_finish(yb, osw_ref, gs_ref, x_ref, pswa_ref, wo_ref, n2_ref, x1_ref, h2_ref):
    y = _sigmoid(gs_ref[...].astype(F32)) * _dot(osw_ref[...], pswa_ref[...]) + yb
    x1 = x_ref[...] + _dot(y.astype(BF16), wo_ref[...])
    x1_ref[...] = x1
    h2_ref[...] = _rms(x1, n2_ref[...]).astype(BF16)


def _merge_kernel(osw_ref, og_ref, rg_ref, gs_ref, gg_ref, x_ref, gn_ref, pswa_ref, pgla_ref, wo_ref, n2_ref,
                  x1_ref, h2_ref):
    yb = _merge_gla_branch(og_ref, rg_ref, gg_ref, gn_ref, pgla_ref)
    _merge_finish(yb, osw_ref, gs_ref, x_ref, pswa_ref, wo_ref, n2_ref, x1_ref, h2_ref)


def _merge_swa_kernel(sink_ref, q_ref, k_ref, v_ref, c_ref, slo_ref, shi_ref, *refs, tiles_per_batch):
    (og_ref, rg_ref, gs_ref, gg_ref, x_ref, gn_ref, pswa_ref, pgla_ref, wo_ref, n2_ref, x1_ref, h2_ref,
     klast_ref, vlast_ref, kprev_ref, vprev_ref, osw_ref) = refs
    w = WINDOW
    t_local = lax.rem(pl.program_id(0), tiles_per_batch)

    @pl.when(t_local == 0)
    def _():
        kprev_ref[...] = jnp.zeros_like(kprev_ref)
        vprev_ref[...] = jnp.zeros_like(vprev_ref)

    yb = _merge_gla_branch(og_ref, rg_ref, gg_ref, gn_ref, pgla_ref)
    k_prev, v_prev = kprev_ref[...], vprev_ref[...]
    for j in range(q_ref.shape[0] // w):
        rs = slice(j * w, (j + 1) * w)
        tabs = (c_ref[rs, :], slo_ref[rs, :], shi_ref[rs, :])
        has_prev = (t_local > 0) if j == 0 else True
        k_prev, v_prev = _swa_block(has_prev, q_ref[rs, :], k_ref[rs, :], v_ref[rs, :], tabs, sink_ref,
                                    k_prev, v_prev, osw_ref, j * w)
    kprev_ref[...] = k_prev
    vprev_ref[...] = v_prev
    _merge_finish(yb, osw_ref, gs_ref, x_ref, pswa_ref, wo_ref, n2_ref, x1_ref, h2_ref)

    @pl.when(t_local == tiles_per_batch - 1)
    def _():
        klast_ref[...] = kprev_ref[...]
        vlast_ref[...] = vprev_ref[...]


def _merge_specs(tm):
    resident = functools.partial(pl.BlockSpec, pipeline_mode=pl.Buffered(1))
    in_specs = [
        pl.BlockSpec((tm, GLA_DV_TOTAL), lambda i: (i, 0)),
        pl.BlockSpec((tm, GLA_DV_TOTAL), lambda i: (i, OFF_RG // GLA_DV_TOTAL)),
        pl.BlockSpec((tm, D_MODEL), lambda i: (i, OFF_GS // D_MODEL)),
        pl.BlockSpec((tm, D_MODEL), lambda i: (i, OFF_GG // D_MODEL)),
        pl.BlockSpec((tm, D_MODEL), lambda i: (i, 0)),
        pl.BlockSpec((1, GLA_DV), lambda i: (0, 0)),
        resident((SWA_Q, D_MODEL), lambda i: (0, 0)),
        resident((GLA_DV_TOTAL, D_MODEL), lambda i: (0, 0)),
        resident((D_MODEL, D_MODEL), lambda i: (0, 0)),
        pl.BlockSpec((1, D_MODEL), lambda i: (0, 0)),
    ]
    out_specs = [pl.BlockSpec((tm, D_MODEL), lambda i: (i, 0)), pl.BlockSpec((tm, D_MODEL), lambda i: (i, 0))]
    return in_specs, out_specs


def _merge_swa(z, o_gla, x2d, gnorm, p_swa, p_gla, w_o, norm2, sink, tables, batch, tm):
    n = x2d.shape[0]
    w = WINDOW
    tiles_per_batch = n // batch // tm
    in_specs, out_specs = _merge_specs(tm)
    tab_spec = pl.BlockSpec((tm, LANES), lambda i: (lax.rem(i, tiles_per_batch), 0))
    last_spec = pl.BlockSpec((None, w, SWA_KV), lambda i: (i // tiles_per_batch, 0, 0))
    return pl.pallas_call(
        functools.partial(_merge_swa_kernel, tiles_per_batch=tiles_per_batch),
        grid=(n // tm,),
        in_specs=[
            pl.BlockSpec(memory_space=pltpu.SMEM),
            pl.BlockSpec((tm, SWA_Q), lambda i: (i, OFF_QS // SWA_Q)),
            pl.BlockSpec((tm, SWA_KV), lambda i: (i, OFF_KS // SWA_KV)),
            pl.BlockSpec((tm, SWA_KV), lambda i: (i, OFF_VS // SWA_KV)),
            tab_spec, tab_spec, tab_spec,
        ] + in_specs,
        out_specs=out_specs + [last_spec, last_spec],
        out_shape=[
            jax.ShapeDtypeStruct((n, D_MODEL), F32),
            jax.ShapeDtypeStruct((n, D_MODEL), BF16),
            jax.ShapeDtypeStruct((batch, w, SWA_KV), F32),
            jax.ShapeDtypeStruct((batch, w, SWA_KV), F32),
        ],
        scratch_shapes=[pltpu.VMEM((w, SWA_KV), F32), pltpu.VMEM((w, SWA_KV), F32),
                        pltpu.VMEM((tm, SWA_Q), BF16)],
        compiler_params=_cparams(("arbitrary",)),
        name="merge_swa",
    )(sink, z, z, z, *tables, o_gla, z, z, z, x2d, gnorm, p_swa, p_gla, w_o, norm2)


def _merge(o_swa, o_gla, z, x2d, gnorm, p_swa, p_gla, w_o, norm2, tm):
    n = x2d.shape[0]
    in_specs, out_specs = _merge_specs(tm)
    return pl.pallas_call(
        _merge_kernel,
        grid=(n // tm,),
        in_specs=[pl.BlockSpec((tm, SWA_Q), lambda i: (i, 0))] + in_specs,
        out_specs=out_specs,
        out_shape=[
            jax.ShapeDtypeStruct((n, D_MODEL), F32),
            jax.ShapeDtypeStruct((n, D_MODEL), BF16),
        ],
        compiler_params=_cparams(("arbitrary",)),
        name="merge",
    )(o_swa, o_gla, z, z, z, x2d, gnorm, p_swa, p_gla, w_o, norm2)


def _mlp_step(h2_ref, x1_ref, load_wup, load_wdn, fn_ref, out_ref, acc_ref):
    f = pl.program_id(1)
    nf = pl.num_programs(1)

    @pl.when(f == 0)
    def _():
        acc_ref[...] = jnp.zeros_like(acc_ref)

    u = _dot(h2_ref[...], load_wup())
    u = jnp.square(jnp.maximum(u, 0.0)).astype(BF16)
    acc_ref[...] += _dot(u, load_wdn())

    @pl.when(f == nf - 1)
    def _():
        out_ref[...] = _rms(x1_ref[...] + acc_ref[...], fn_ref[...])


def _mlp_kernel(h2_ref, x1_ref, wup_ref, wdn_ref, fn_ref, out_ref, acc_ref):
    _mlp_step(h2_ref, x1_ref, lambda: wup_ref[...], lambda: wdn_ref[...], fn_ref, out_ref, acc_ref)


def _mlp_cast_kernel(h2_ref, x1_ref, wup_ref, wdn_ref, fn_ref, out_ref, wup_bf_ref, wdn_bf_ref, acc_ref):
    def cast_and_keep(src_ref, dst_ref):
        def load():
            w = src_ref[...].astype(BF16)
            dst_ref[...] = w
            return w
        return load

    _mlp_step(h2_ref, x1_ref, cast_and_keep(wup_ref, wup_bf_ref), cast_and_keep(wdn_ref, wdn_bf_ref),
              fn_ref, out_ref, acc_ref)


def _mlp_cast(h2, x1, w_up, w_down, final_norm, tf):
    n = h2.shape[0]
    return pl.pallas_call(
        _mlp_cast_kernel,
        grid=(1, D_FF // tf),
        in_specs=[
            pl.BlockSpec((n, D_MODEL), lambda i, f: (0, 0)),
            pl.BlockSpec((n, D_MODEL), lambda i, f: (0, 0)),
            pl.BlockSpec((None, D_MODEL, tf), lambda i, f: (0, 0, f)),
            pl.BlockSpec((None, tf, D_MODEL), lambda i, f: (0, f, 0)),
            pl.BlockSpec((1, D_MODEL), lambda i, f: (0, 0)),
        ],
        out_specs=[
            pl.BlockSpec((n, D_MODEL), lambda i, f: (0, 0)),
            pl.BlockSpec((D_MODEL, tf), lambda i, f: (0, f)),
            pl.BlockSpec((tf, D_MODEL), lambda i, f: (f, 0)),
        ],
        out_shape=[
            jax.ShapeDtypeStruct((n, D_MODEL), F32),
            jax.ShapeDtypeStruct((D_MODEL, D_FF), BF16),
            jax.ShapeDtypeStruct((D_FF, D_MODEL), BF16),
        ],
        scratch_shapes=[pltpu.VMEM((n, D_MODEL), F32)],
        compiler_params=_cparams(("arbitrary", "arbitrary")),
        name="mlp_cast",
    )(h2, x1, w_up, w_down, final_norm)


def _mlp(h2, x1, w_up, w_down, final_norm, tm, tf):
    n = h2.shape[0]
    return pl.pallas_call(
        _mlp_kernel,
        grid=(n // tm, D_FF // tf),
        in_specs=[
            pl.BlockSpec((tm, D_MODEL), lambda i, f: (i, 0)),
            pl.BlockSpec((tm, D_MODEL), lambda i, f: (i, 0)),
            pl.BlockSpec((D_MODEL, tf), lambda i, f: (0, f)),
            pl.BlockSpec((tf, D_MODEL), lambda i, f: (f, 0)),
            pl.BlockSpec((1, D_MODEL), lambda i, f: (0, 0)),
        ],
        out_specs=pl.BlockSpec((tm, D_MODEL), lambda i, f: (i, 0)),
        out_shape=jax.ShapeDtypeStruct((n, D_MODEL), F32),
        scratch_shapes=[pltpu.VMEM((tm, D_MODEL), F32)],
        compiler_params=_cparams(("arbitrary", "arbitrary")),
        name="mlp",
    )(h2, x1, w_up, w_down, final_norm)


def _pick_tile(n, pref):
    t = min(n, pref)
    while n % t:
        t //= 2
    return t


def _tile_plan(n_prompt, n_sample, t_prompt, b_sample):
    return {
        "proj_rows_p": _pick_tile(n_prompt, 512),
        "swa_pairs": _pick_tile(b_sample // 2, 4), "gla_pairs": _pick_tile(b_sample // 2, 2),
        "gla_rows": _pick_tile(t_prompt, GLA_TILE),
        "merge_rows_p": _pick_tile(n_prompt, 256), "merge_rows_s": _pick_tile(n_sample, 256),
        "mlp_rows_p": _pick_tile(n_prompt, 512),
        "mlp_ff": 1024, "mlp_cast_ff": 512,
    }


W_IN_SPLITS = (("qs", SWA_Q), ("ks", SWA_KV), ("vs", SWA_KV), ("qg", GLA_DK_TOTAL), ("kg", GLA_DK_TOTAL),
               ("vg", GLA_DV_TOTAL), ("rg", GLA_DV_TOTAL), ("ag", GLA_GATE_RANK), ("gs", D_MODEL), ("gg", D_MODEL))


REORDER_PIECE = HEAD_DIM
REORDER_PIECES = 8


def _w_in_row_table():
    src, o = {}, 0
    for name, width in W_IN_SPLITS:
        src[name] = o
        o += width
    rows = np.zeros(Z_COLS, np.int64)
    for name, dst in (("gs", OFF_GS), ("gg", OFF_GG), ("vg", OFF_VG), ("rg", OFF_RG), ("qg", OFF_QG),
                      ("kg", OFF_KG), ("ks", OFF_KS), ("vs", OFF_VS)):
        width = dict(W_IN_SPLITS)[name]
        rows[dst:dst + width] = src[name] + np.arange(width)
    for h in range(GROUP):
        for g in range(N_KV_HEADS):
            dst = OFF_QS + h * SWA_KV + g * HEAD_DIM
            rows[dst:dst + HEAD_DIM] = src["qs"] + (g * GROUP + h) * HEAD_DIM + np.arange(HEAD_DIM)
    pieces = rows.reshape(-1, REORDER_PIECE)
    assert (pieces == pieces[:, :1] + np.arange(REORDER_PIECE)).all()
    return jnp.asarray(pieces[:, 0], jnp.int32), src["ag"]


def _reorder_w_in_kernel(tbl_ref, *refs):
    piece_refs = refs[:REORDER_PIECES]
    ag_ref, x_ref, nw_ref, wm_ref, wag_ref, z_ref, zag_ref, h_ref = refs[REORDER_PIECES:]

    @pl.when(pl.program_id(0) == 0)
    def _():
        pad = jnp.zeros((LANES - GLA_GATE_RANK, wag_ref.shape[1]), BF16)
        wag = jnp.concatenate([ag_ref[0].astype(BF16), pad], axis=0)
        wag_ref[...] = wag
        h = _rms(x_ref[...], nw_ref[...]).astype(BF16)
        h_ref[...] = h
        zag_ref[...] = _dot_nt(h, wag)

    tile = jnp.concatenate([p[0].astype(BF16) for p in piece_refs], axis=0)
    wm_ref[...] = tile
    z_ref[...] = _dot_nt(h_ref[...], tile)


def _reorder_w_in(w_in_t, x_rows, norm1):
    d = w_in_t.shape[2]
    n = x_rows.shape[0]
    table, ag_row = _w_in_row_table()
    rows = REORDER_PIECE * REORDER_PIECES
    piece = lambda k: pl.BlockSpec((pl.Element(1), pl.Element(REORDER_PIECE), pl.Element(d)),
                                   lambda i, tbl: (0, pl.multiple_of(tbl[i * REORDER_PIECES + k], GLA_GATE_RANK), 0))
    grid_spec = pltpu.PrefetchScalarGridSpec(
        num_scalar_prefetch=1,
        grid=(Z_COLS // rows,),
        in_specs=[piece(k) for k in range(REORDER_PIECES)]
        + [pl.BlockSpec((pl.Element(1), pl.Element(GLA_GATE_RANK), pl.Element(d)),
                        lambda i, tbl: (0, ag_row, 0)),
           pl.BlockSpec((n, d), lambda i, tbl: (0, 0)),
           pl.BlockSpec((1, d), lambda i, tbl: (0, 0))],
        out_specs=[pl.BlockSpec((rows, d), lambda i, tbl: (i, 0)),
                   pl.BlockSpec((LANES, d), lambda i, tbl: (0, 0)),
                   pl.BlockSpec((n, rows), lambda i, tbl: (0, i)),
                   pl.BlockSpec((n, LANES), lambda i, tbl: (0, 0))],
        scratch_shapes=[pltpu.VMEM((n, d), BF16)],
    )
    return pl.pallas_call(
        _reorder_w_in_kernel,
        grid_spec=grid_spec,
        out_shape=[jax.ShapeDtypeStruct((Z_COLS, d), BF16), jax.ShapeDtypeStruct((LANES, d), BF16),
                   jax.ShapeDtypeStruct((n, Z_COLS), F32), jax.ShapeDtypeStruct((n, LANES), F32)],
        compiler_params=_cparams(("arbitrary",)),
        name="reorder_w_in",
    )(table, *([w_in_t] * (REORDER_PIECES + 1)), x_rows, norm1)


def kernel(x_prompt, x_sample, cache_swa_k, cache_swa_v, state_gla, norm1, w_in, w_a2, b_a, sink,
           gla_norm, p_swa, p_gla, w_o, norm2, w_up, w_down, final_norm):
    assert norm1.shape[0] == 1, "single-layer stack"
    bp, tp, d = x_prompt.shape
    bs, ts, _ = x_sample.shape
    assert ts == SAMPLE_T and bs % 2 == 0 and tp % WINDOW == 0
    w_buf = cache_swa_k.shape[2]
    assert w_buf == WINDOW

    wa2 = jnp.pad(w_a2[0], ((0, LANES - GLA_GATE_RANK), (0, 0))).astype(BF16)
    ba = b_a[0][None, :]
    n1, n2, fn = norm1[0][None, :], norm2[0][None, :], final_norm[None, :]
    gn = gla_norm[0][None, :]
    pswa = p_swa[0].reshape(N_KV_HEADS, GROUP, HEAD_DIM, d).transpose(1, 0, 2, 3).reshape(SWA_Q, d).astype(BF16)
    pgla, wo = p_gla[0].astype(BF16), w_o[0].astype(BF16)
    sink_smem = sink[0][None, :]
    sink_rows = jnp.broadcast_to(jnp.repeat(sink[0], PAIR_ROWS)[:, None], (N_HEADS * PAIR_ROWS, LANES))

    xp = x_prompt.reshape(bp * tp, d)
    xs = x_sample.reshape(bs * ts, d)
    np_, ns = xp.shape[0], xs.shape[0]

    tiles = _tile_plan(np_, ns, tp, bs)

    w_main, w_ag, zs, zags = _reorder_w_in(jnp.swapaxes(w_in, 1, 2), xs, n1)
    zp, zagp = _in_proj(xp, n1, w_main, w_ag, tiles["proj_rows_p"], Z_COLS // 2, BF16)

    zp3 = zp.reshape(bp, tp, Z_COLS)
    pos_s = PAST_LEN + jnp.arange(ts)
    tabs_s = _rope_tables(jnp.concatenate([pos_s, pos_s]))
    pos_minor = lambda c: jnp.transpose(c[0], (0, 2, 3, 1)).reshape(bs, SWA_KV, w_buf)
    pos_major = lambda c: jnp.transpose(c.reshape(bs, N_KV_HEADS, HEAD_DIM, w_buf), (0, 3, 1, 2))[None]
    o_swa_s, nk_s, nv_s = _swa_sample(zs, pos_minor(cache_swa_k), pos_minor(cache_swa_v), sink_rows, tabs_s,
                                      tiles["swa_pairs"])

    o_gla_p, s_p = _gla_prompt(zp3, zagp.reshape(bp, tp, LANES), wa2, ba, tiles["gla_rows"])
    o_gla_s, s_s = _gla_sample(zs, zags, wa2, ba, state_gla[0], tiles["gla_pairs"])

    x1p, h2p, k_last, v_last = _merge_swa(zp, o_gla_p.reshape(np_, GLA_DV_TOTAL), xp, gn, pswa, pgla, wo, n2,
                                          sink_smem, _rope_tables(jnp.arange(tp)), bp, tiles["merge_rows_p"])
    x1s, h2s = _merge(o_swa_s, o_gla_s, zs, xs, gn, pswa, pgla, wo, n2, tiles["merge_rows_s"])
    ys, wup, wdn = _mlp_cast(h2s, x1s, w_up, w_down, fn, tiles["mlp_cast_ff"])
    yp = _mlp(h2p, x1p, wup, wdn, fn, tiles["mlp_rows_p"], tiles["mlp_ff"])

    kv5 = lambda a, nb: a.reshape(1, nb, w_buf, N_KV_HEADS, HEAD_DIM)
    return (yp.reshape(bp, tp, d), ys.reshape(bs, ts, d),
            kv5(k_last, bp), kv5(v_last, bp), s_p[None],
            pos_major(nk_s), pos_major(nv_s), s_s[None])
```
